```python
import math
import jax
import jax.numpy as jnp
from jax import lax
import numpy as np

D_MODEL = 1024
BATCH = 8
SEQ = 2048
DEPTH = 2
DEC_BATCH = 128
DEC_SEQ = 8
PAST_LEN = 2048
PAGE_SIZE = 128

N_MIXERS = 4
HEADS = 4
HD = D_MODEL // (N_MIXERS * HEADS)
GROUP_W = HEADS * HD
DIFF_QK = HD // 2
ROPE_THETA = 500000.0
ROPE_FRACTION = 4
NSA_STRIDE = 16
NSA_CMP_LEN = 2 * NSA_STRIDE
NSA_CMP_HID = 2 * HD
NSA_SEL_BLOCK = 64
NSA_TOPN = 16
NSA_WINDOW = 512
NSA_FORCE = 1.0e4
MOBA_BLOCK = 256
MOBA_TOPK = 3
CTX_ALIGN = MOBA_BLOCK
Q_BLOCK = 128
GATHER_Q_BLOCK = 32
N_GROUPS = 4
EXPERTS_PER_GROUP = 4
N_EXPERTS = N_GROUPS * EXPERTS_PER_GROUP
EXPERT_HID = D_MODEL // 4
TOPK_IN_GROUP = 2
ALPHA = (2 * DEPTH) ** 0.25
BETA = (8 * DEPTH) ** -0.25
LN_EPS = 1e-5
RMS_EPS = 1e-5
PROJ_SPLITS = (GROUP_W, GROUP_W, GROUP_W,
               GROUP_W, 4 * HD, 2 * HD, HEADS * 3,
               GROUP_W, GROUP_W, GROUP_W,
               GROUP_W, GROUP_W, GROUP_W)
PROJ_W = sum(PROJ_SPLITS)

kernel_name = 'hybrid_sb_nsa_diff_moba_hmoe_step'


def layer_norm(x, g, b):
    xf = x.astype(jnp.float32)
    xc = xf - jnp.mean(xf, axis=-1, keepdims=True)
    var = jnp.mean(xc * xc, axis=-1, keepdims=True)
    return (xc * lax.rsqrt(var + LN_EPS) * g.astype(jnp.float32) + b.astype(jnp.float32)).astype(x.dtype)


def rms_norm(x, g):
    xf = x.astype(jnp.float32)
    return (xf * lax.rsqrt(jnp.mean(xf * xf, axis=-1, keepdims=True) + RMS_EPS) * g.astype(jnp.float32)).astype(x.dtype)


def rope(x, pos):
    dim = x.shape[-1]
    rot = dim // ROPE_FRACTION
    half = rot // 2
    inv = jnp.power(jnp.float32(ROPE_THETA), -jnp.arange(half, dtype=jnp.float32) * (2.0 / rot))
    ang = pos.astype(jnp.float32)[:, None] * inv[None, :]
    shape = (1, pos.shape[0]) + (1,) * (x.ndim - 3) + (half,)
    cos = jnp.cos(ang).reshape(shape).astype(x.dtype)
    sin = jnp.sin(ang).reshape(shape).astype(x.dtype)
    x1, x2 = x[..., :half], x[..., half:rot]
    return jnp.concatenate([x1 * cos - x2 * sin, x2 * cos + x1 * sin, x[..., rot:]], axis=-1)


def masked_softmax(logits, mask):
    l = jnp.where(mask, logits.astype(jnp.float32), -jnp.inf)
    m = jnp.max(l, axis=-1, keepdims=True)
    p = jnp.exp(l - jnp.where(jnp.isfinite(m), m, 0.0))
    s = jnp.sum(p, axis=-1, keepdims=True)
    return p / jnp.where(s > 0, s, 1.0)


def pad_rows(a, before, after):
    return jnp.pad(a, [(0, 0), (before, after)] + [(0, 0)] * (a.ndim - 2))


def gather_blocks(blocks, idx):
    return jax.vmap(lambda b_, i_: b_[i_])(blocks, idx)


def paged_rows(pool, l, page_table):
    rows = pool[l, page_table]
    return rows.reshape((rows.shape[0], rows.shape[1] * rows.shape[2]) + rows.shape[3:])


def sweep(fn, block, q_pos, *qs):
    nb = q_pos.shape[0] // block

    def split(a):
        return jnp.moveaxis(a.reshape((a.shape[0], nb, block) + a.shape[2:]), 1, 0)

    out = lax.map(lambda args: fn(*args), (q_pos.reshape(nb, block),) + tuple(split(a) for a in qs))
    out = jnp.moveaxis(out, 0, 1)
    return out.reshape((out.shape[0], nb * block) + out.shape[3:])


def project(x, pos, w_in):
    B, T, _ = x.shape
    parts = jnp.split(jnp.einsum('btd,dp->btp', x, w_in), np.cumsum(PROJ_SPLITS)[:-1].tolist(), axis=-1)

    def heads(a, *s):
        return a.reshape((B, T) + s)

    sb_q = heads(parts[0], HEADS, HD)
    sb_kv = jnp.stack([heads(parts[1], HEADS, HD), heads(parts[2], HEADS, HD)], axis=2)
    nsa_q = rope(heads(parts[3], HEADS, HD), pos)
    nkv = heads(parts[4], 4, HD)
    nsa_kv = jnp.stack([rope(nkv[:, :, 0], pos), nkv[:, :, 1], rope(nkv[:, :, 2], pos), nkv[:, :, 3]], axis=2)
    wkv = heads(parts[5], 2, HD)
    win_kv = jnp.stack([rope(wkv[:, :, 0], pos), wkv[:, :, 1]], axis=2)
    nsa_g = jax.nn.sigmoid(heads(parts[6], HEADS, 3))
    diff_q = rope(heads(parts[7], HEADS, 2, DIFF_QK), pos)
    diff_k = rope(heads(parts[8], HEADS, 2, DIFF_QK), pos).reshape(B, T, HEADS, HD)
    diff_kv = jnp.stack([diff_k, heads(parts[9], HEADS, HD)], axis=2)
    moba_q = rope(heads(parts[10], HEADS, HD), pos)
    moba_kv = jnp.stack([rope(heads(parts[11], HEADS, HD), pos), heads(parts[12], HEADS, HD)], axis=2)
    return sb_q, sb_kv, nsa_q, nsa_g, nsa_kv, win_kv, diff_q, diff_kv, moba_q, moba_kv


def sb_block(qp, q, k, v):
    L = k.shape[1]
    z = jnp.einsum('bthd,bshd->bths', q, k).astype(jnp.float32) * (HD ** -0.5)
    mask = (jnp.arange(L)[None, :] < qp[:, None])[None, :, None, :]
    log_fail = jnp.where(mask, jax.nn.log_sigmoid(-z), 0.0)
    later = lax.cumsum(log_fail, axis=log_fail.ndim - 1, reverse=True) - log_fail
    w = jnp.where(mask, jnp.exp(jax.nn.log_sigmoid(z) + later), 0.0)
    return jnp.einsum('bths,bshd->bthd', w.astype(v.dtype), v)


def nsa_context(kv, pe, w1, b1, w2):
    B, Lp = kv.shape[0], kv.shape[1]
    raw = pad_rows(kv[:, :, :2], 0, NSA_STRIDE)
    halves = raw.reshape(B, Lp // NSA_STRIDE + 1, NSA_STRIDE, 2, HD)
    tok = jnp.concatenate([halves[:, :-1], halves[:, 1:]], axis=2) + jnp.transpose(pe, (1, 0, 2))[None, None]
    hid = jax.nn.gelu(jnp.einsum('bnlcd,cldh->bnch', tok, w1) + b1)
    comp = jnp.einsum('bnch,chd->bncd', hid, w2)
    sel = kv[:, :, 2:].reshape(B, Lp // NSA_SEL_BLOCK, NSA_SEL_BLOCK, 2, HD)
    return comp[:, :, 0], comp[:, :, 1], sel[..., 0, :], sel[..., 1, :]


def nsa_block(qp, q, g, ck, cv, skb, svb, wk, wv, w_pos0):
    B, t, H, d = q.shape
    scale = d ** -0.5
    NC, NS = ck.shape[1], skb.shape[1]
    c_end = jnp.arange(NC) * NSA_STRIDE + NSA_CMP_LEN - 1
    c_mask = (c_end[None, :] <= qp[:, None])[None, :, None, :]
    p_cmp = masked_softmax(jnp.einsum('bthd,bnd->bthn', q, ck) * scale, c_mask)
    o_cmp = jnp.einsum('bthn,bnd->bthd', p_cmp.astype(cv.dtype), cv)
    imp = p_cmp.sum(axis=2).reshape(B, t, NS, NC // NS).sum(axis=-1)
    j = jnp.arange(NS)[None, :]
    qb = (qp // NSA_SEL_BLOCK)[:, None]
    valid = j <= qb
    forced = (j == 0) | (j == qb) | (j == qb - 1)
    score = jnp.where(valid, jnp.where(forced, NSA_FORCE, imp), -jnp.inf)
    n = min(NSA_TOPN, NS)
    _, idx = lax.top_k(score, n)
    ok = jnp.take_along_axis(jnp.broadcast_to(valid, score.shape), idx, axis=-1)
    flat = idx.reshape(B, t * n)
    gk = gather_blocks(skb, flat).reshape(B, t, n * NSA_SEL_BLOCK, d)
    gv = gather_blocks(svb, flat).reshape(B, t, n * NSA_SEL_BLOCK, d)
    kpos = (idx[..., None] * NSA_SEL_BLOCK + jnp.arange(NSA_SEL_BLOCK)).reshape(B, t, n * NSA_SEL_BLOCK)
    s_mask = (jnp.repeat(ok, NSA_SEL_BLOCK, axis=-1) & (kpos <= qp[None, :, None]))[:, :, None, :]
    p_slc = masked_softmax(jnp.einsum('bthd,btkd->bthk', q, gk) * scale, s_mask)
    o_slc = jnp.einsum('bthk,btkd->bthd', p_slc.astype(gv.dtype), gv)
    span = NSA_WINDOW - 1 + t
    start = qp[0] - (NSA_WINDOW - 1) - w_pos0
    bk = lax.dynamic_slice_in_dim(wk, start, span, axis=1)
    bv = lax.dynamic_slice_in_dim(wv, start, span, axis=1)
    wpos = (w_pos0 + start + jnp.arange(span))[None, :]
    w_mask = ((wpos >= 0) & (wpos <= qp[:, None]) & (wpos > qp[:, None] - NSA_WINDOW))[None, :, None, :]
    p_win = masked_softmax(jnp.einsum('bthd,bsd->bths', q, bk) * scale, w_mask)
    o_win = jnp.einsum('bths,bsd->bthd', p_win.astype(bv.dtype), bv)
    return g[..., 0:1] * o_cmp + g[..., 1:2] * o_slc + g[..., 2:3] * o_win


def diff_block(qp, q, k, v, lam):
    L = k.shape[1]
    s = jnp.einsum('bthcd,bshcd->bthcs', q, k) * (DIFF_QK ** -0.5)
    mask = (jnp.arange(L)[None, :] <= qp[:, None])[None, :, None, None, :]
    p = masked_softmax(s, mask)
    w = p[..., 0, :] - lam * p[..., 1, :]
    return jnp.einsum('bths,bshd->bthd', w.astype(v.dtype), v)


def moba_block(qp, q, k, v, kblk, vblk, kmean):
    B, t, H, d = q.shape
    NB = kmean.shape[1]
    scale = d ** -0.5
    own = qp // MOBA_BLOCK
    gate = jnp.einsum('bthd,bnhd->bthn', q, kmean).astype(jnp.float32)
    past = (jnp.arange(NB)[None, :] < own[:, None])[None, :, None, :]
    n = min(MOBA_TOPK, NB)
    _, idx = lax.top_k(jnp.where(past, gate, -jnp.inf), n)
    ok = jnp.take_along_axis(jnp.broadcast_to(past, gate.shape), idx, axis=-1)
    flat = jnp.transpose(idx, (0, 2, 1, 3)).reshape(B * H, t * n)
    gk = gather_blocks(kblk, flat).reshape(B, H, t, n * MOBA_BLOCK, d)
    gv = gather_blocks(vblk, flat).reshape(B, H, t, n * MOBA_BLOCK, d)
    s_sel = jnp.einsum('bthd,bhtkd->bthk', q, gk) * scale
    m_sel = jnp.repeat(ok, MOBA_BLOCK, axis=-1)
    start = own[0] * MOBA_BLOCK
    ko = lax.dynamic_slice_in_dim(k, start, MOBA_BLOCK, axis=1)
    vo = lax.dynamic_slice_in_dim(v, start, MOBA_BLOCK, axis=1)
    opos = (start + jnp.arange(MOBA_BLOCK))[None, :]
    m_own = ((opos <= qp[:, None]) & (opos >= (own * MOBA_BLOCK)[:, None]))[None, :, None, :]
    s_own = jnp.einsum('bthd,bshd->bths', q, ko) * scale
    p = masked_softmax(jnp.concatenate([s_sel, s_own], axis=-1),
                       jnp.concatenate([m_sel, jnp.broadcast_to(m_own, s_own.shape)], axis=-1)).astype(v.dtype)
    ksel = n * MOBA_BLOCK
    return (jnp.einsum('bthk,bhtkd->bthd', p[..., :ksel], gv)
            + jnp.einsum('bths,bshd->bthd', p[..., ksel:], vo))


def token_mixers(q_sb, q_nsa, g_nsa, q_diff, q_moba, ctx, win_ctx, w_pos0, q_pos, dense_blk, gather_blk, l,
                 w_out, pe, w1, b1, w2, lam_p, subln):
    sb_kv, nsa_kv, diff_kv, moba_kv = ctx
    B, L = sb_kv.shape[0], sb_kv.shape[1]
    T = q_pos.shape[0]
    Lp = -(-L // CTX_ALIGN) * CTX_ALIGN
    k_sb, v_sb = sb_kv[:, :, 0], sb_kv[:, :, 1]
    o_sb = sweep(lambda qp, q: sb_block(qp, q, k_sb, v_sb), dense_blk, q_pos, q_sb)
    ck, cv, skb, svb = nsa_context(pad_rows(nsa_kv, 0, Lp - L), pe, w1, b1, w2)
    wk = pad_rows(win_ctx[:, :, 0], NSA_WINDOW - 1, 0)
    wv = pad_rows(win_ctx[:, :, 1], NSA_WINDOW - 1, 0)
    o_nsa = sweep(lambda qp, q, g: nsa_block(qp, q, g, ck, cv, skb, svb, wk, wv, w_pos0),
                  gather_blk, q_pos, q_nsa, g_nsa)
    lam_init = 0.8 - 0.6 * math.exp(-0.3 * l)
    lp = lam_p.astype(jnp.float32)
    lam = jnp.exp(jnp.sum(lp[0] * lp[1])) - jnp.exp(jnp.sum(lp[2] * lp[3])) + lam_init
    k_diff = diff_kv[:, :, 0].reshape(B, L, HEADS, 2, DIFF_QK)
    v_diff = diff_kv[:, :, 1]
    o_diff = sweep(lambda qp, q: diff_block(qp, q, k_diff, v_diff, lam), dense_blk, q_pos, q_diff)
    o_diff = rms_norm(o_diff, subln) * (1.0 - lam_init)
    mk = pad_rows(moba_kv[:, :, 0], 0, Lp - L)
    mv = pad_rows(moba_kv[:, :, 1], 0, Lp - L)
    nb = Lp // MOBA_BLOCK
    kmean = mk.astype(jnp.float32).reshape(B, nb, MOBA_BLOCK, HEADS, HD).mean(axis=2).astype(mk.dtype)

    def to_blocks(a):
        return jnp.transpose(a.reshape(B, nb, MOBA_BLOCK, HEADS, HD), (0, 3, 1, 2, 4)).reshape(B * HEADS, nb, MOBA_BLOCK, HD)

    kblk, vblk = to_blocks(mk), to_blocks(mv)
    o_moba = sweep(lambda qp, q: moba_block(qp, q, mk, mv, kblk, vblk, kmean), gather_blk, q_pos, q_moba)
    mixed = jnp.concatenate([o.reshape(B, T, GROUP_W) for o in (o_sb, o_nsa, o_diff, o_moba)], axis=-1)
    return jnp.einsum('btm,md->btd', mixed, w_out)


def moe(x, rgw, rgb, rew, reb, w_gate, w_up, w_down):
    shape = x.shape
    t = x.reshape(-1, shape[-1])
    lg = jnp.einsum('nd,dg->ng', t, rgw).astype(jnp.float32) + rgb.astype(jnp.float32)
    pg = jax.nn.softmax(lg, axis=-1)
    gsel = jnp.argmax(lg, axis=-1)
    le = (jnp.einsum('nd,de->ne', t, rew).astype(jnp.float32) + reb.astype(jnp.float32)).reshape(-1, N_GROUPS, EXPERTS_PER_GROUP)
    le_g = jnp.take_along_axis(le, gsel[:, None, None], axis=1)[:, 0]
    top_v, top_i = lax.top_k(le_g, TOPK_IN_GROUP)
    w = jax.nn.softmax(top_v, axis=-1) * jnp.take_along_axis(pg, gsel[:, None], axis=1)
    eid = gsel[:, None] * EXPERTS_PER_GROUP + top_i
    combine = jnp.sum(jax.nn.one_hot(eid, N_EXPERTS, dtype=jnp.float32) * w[..., None], axis=1)
    h = jax.nn.silu(jnp.einsum('nd,edf->nef', t, w_gate)) * jnp.einsum('nd,edf->nef', t, w_up)
    h = h * combine[..., None].astype(h.dtype)
    return jnp.einsum('nef,efd->nd', h, w_down).reshape(shape)


def run_layer(x, q_pos, pools, page_table, win_past, w_pos0, dense_blk, gather_blk, n_win_keep, l, prm):
    (w_in, w_out, pe, w1, b1, w2, lam_p, subln, ln_g, ln_b, rgw, rgb, rew, reb, wg, wu, wd) = prm
    sb_q, sb_kv, nsa_q, nsa_g, nsa_kv, win_kv, diff_q, diff_kv, moba_q, moba_kv = project(x, q_pos, w_in)
    new_rows = (sb_kv, nsa_kv, diff_kv, moba_kv)
    if pools is None:
        ctx, win_ctx = new_rows, win_kv
    else:
        ctx = tuple(jnp.concatenate([paged_rows(p, l, page_table), r], axis=1) for p, r in zip(pools, new_rows))
        win_ctx = jnp.concatenate([win_past, win_kv], axis=1)
    mix = token_mixers(sb_q, nsa_q, nsa_g, diff_q, moba_q, ctx, win_ctx, w_pos0, q_pos, dense_blk, gather_blk, l,
                       w_out, pe, w1, b1, w2, lam_p, subln)
    h = layer_norm(ALPHA * x + mix, ln_g[0], ln_b[0])
    y = layer_norm(ALPHA * h + moe(h, rgw, rgb, rew, reb, wg, wu, wd), ln_g[1], ln_b[1])
    return y, new_rows, win_ctx[:, win_ctx.shape[1] - n_win_keep:]


def setup_inputs(seed: int = 0) -> dict:
    key = jax.random.key(seed)
    ks = jax.random.split(key, 32)

    def nrm(k, shape, s):
        return jax.random.normal(k, shape, jnp.float32) * s

    n_pages = PAST_LEN // PAGE_SIZE
    n_used = DEC_BATCH * n_pages
    n_phys = n_used + max(1, n_used // 4)
    page_table = jax.random.permutation(ks[0], n_phys)[:n_used].reshape(DEC_BATCH, n_pages).astype(jnp.int32)
    w_eff = min(NSA_WINDOW, PAST_LEN)
    mix_w = N_MIXERS * GROUP_W
    return {
        'x_prompt': nrm(ks[1], (BATCH, SEQ, D_MODEL), 1.0),
        'x_sample': nrm(ks[2], (DEC_BATCH, DEC_SEQ, D_MODEL), 1.0),
        'cache_sb_kv': nrm(ks[3], (DEPTH, n_phys, PAGE_SIZE, 2, HEADS, HD), 1.0),
        'cache_nsa_kv': nrm(ks[4], (DEPTH, n_phys, PAGE_SIZE, 4, HD), 1.0),
        'cache_diff_kv': nrm(ks[5], (DEPTH, n_phys, PAGE_SIZE, 2, HEADS, HD), 1.0),
        'cache_moba_kv': nrm(ks[6], (DEPTH, n_phys, PAGE_SIZE, 2, HEADS, HD), 1.0),
        'state_nsa_win': nrm(ks[7], (DEPTH, DEC_BATCH, w_eff, 2, HD), 1.0),
        'page_table': page_table,
        'w_in': nrm(ks[8], (DEPTH, D_MODEL, PROJ_W), D_MODEL ** -0.5),
        'w_out': nrm(ks[9], (DEPTH, mix_w, D_MODEL), BETA * mix_w ** -0.5),
        'nsa_cmp_pe': nrm(ks[10], (DEPTH, 2, NSA_CMP_LEN, HD), 0.02),
        'nsa_cmp_w1': nrm(ks[11], (DEPTH, 2, NSA_CMP_LEN, HD, NSA_CMP_HID), (NSA_CMP_LEN * HD) ** -0.5),
        'nsa_cmp_b1': nrm(ks[12], (DEPTH, 2, NSA_CMP_HID), 0.01),
        'nsa_cmp_w2': nrm(ks[13], (DEPTH, 2, NSA_CMP_HID, HD), NSA_CMP_HID ** -0.5),
        'diff_lambda': nrm(ks[14], (DEPTH, 4, DIFF_QK), 0.1),
        'diff_subln': 1.0 + nrm(ks[15], (DEPTH, HD), 0.02),
        'ln_gain': 1.0 + nrm(ks[16], (DEPTH, 2, D_MODEL), 0.02),
        'ln_bias': nrm(ks[17], (DEPTH, 2, D_MODEL), 0.02),
        'router_group_w': nrm(ks[18], (DEPTH, D_MODEL, N_GROUPS), D_MODEL ** -0.5),
        'router_group_b': nrm(ks[19], (DEPTH, N_GROUPS), 0.01),
        'router_expert_w': nrm(ks[20], (DEPTH, D_MODEL, N_EXPERTS), D_MODEL ** -0.5),
        'router_expert_b': nrm(ks[21], (DEPTH, N_EXPERTS), 0.01),
        'expert_w_gate': nrm(ks[22], (DEPTH, N_EXPERTS, D_MODEL, EXPERT_HID), D_MODEL ** -0.5),
        'expert_w_up': nrm(ks[23], (DEPTH, N_EXPERTS, D_MODEL, EXPERT_HID), D_MODEL ** -0.5),
        'expert_w_down': nrm(ks[24], (DEPTH, N_EXPERTS, EXPERT_HID, D_MODEL), BETA * EXPERT_HID ** -0.5),
    }


def reference(x_prompt, x_sample, cache_sb_kv, cache_nsa_kv, cache_diff_kv, cache_moba_kv, state_nsa_win,
              page_table, w_in, w_out, nsa_cmp_pe, nsa_cmp_w1, nsa_cmp_b1, nsa_cmp_w2, diff_lambda, diff_subln,
              ln_gain, ln_bias, router_group_w, router_group_b, router_expert_w, router_expert_b,
              expert_w_gate, expert_w_up, expert_w_down):
    t_p, t_s = x_prompt.shape[1], x_sample.shape[1]
    past_len = page_table.shape[1] * PAGE_SIZE
    w_eff = state_nsa_win.shape[2]
    pos_p = jnp.arange(t_p, dtype=jnp.int32)
    pos_s = past_len + jnp.arange(t_s, dtype=jnp.int32)
    pools = (cache_sb_kv, cache_nsa_kv, cache_diff_kv, cache_moba_kv)
    y_p, y_s = x_prompt, x_sample
    new_p, new_s = [], []
    for l in range(DEPTH):
        prm = (w_in[l], w_out[l], nsa_cmp_pe[l], nsa_cmp_w1[l], nsa_cmp_b1[l], nsa_cmp_w2[l], diff_lambda[l],
               diff_subln[l], ln_gain[l], ln_bias[l], router_group_w[l], router_group_b[l], router_expert_w[l],
               router_expert_b[l], expert_w_gate[l], expert_w_up[l], expert_w_down[l])
        y_p, rows_p, wbuf_p = run_layer(y_p, pos_p, None, None, None, -(NSA_WINDOW - 1),
                                        min(Q_BLOCK, t_p), min(GATHER_Q_BLOCK, t_p), min(NSA_WINDOW, t_p), l, prm)
        y_s, rows_s, wbuf_s = run_layer(y_s, pos_s, pools, page_table, state_nsa_win[l],
                                        past_len - w_eff - (NSA_WINDOW - 1), t_s, 1, w_eff, l, prm)
        new_p.append(rows_p + (wbuf_p,))
        new_s.append(rows_s + (wbuf_s,))
    sb_p, nsa_p, diff_p, moba_p, win_p = [jnp.stack([r[i] for r in new_p]) for i in range(5)]
    sb_s, nsa_s, diff_s, moba_s, win_s = [jnp.stack([r[i] for r in new_s]) for i in range(5)]
    return (y_p, y_s, sb_p, nsa_p, diff_p, moba_p, win_p, sb_s, nsa_s, diff_s, moba_s, win_s)
```

```python
import functools
import math

import numpy as np
import jax
import jax.numpy as jnp
from jax import lax
from jax.experimental import pallas as pl
from jax.experimental.pallas import tpu as pltpu

F32 = jnp.float32
BF16 = jnp.bfloat16

D_MODEL = 1024
HEADS = 4
HD = 64
GROUP_W = HEADS * HD
DIFF_QK = HD // 2
DEPTH_LAYERS = 2
ROPE_THETA = 500000.0
ROPE_FRACTION = 4
NSA_STRIDE = 16
NSA_CMP_LEN = 32
NSA_CMP_HID = 128
NSA_SEL_BLOCK = 64
NSA_TOPN = 16
NSA_WINDOW = 512
NSA_FORCE = 1.0e4
MOBA_BLOCK = 256
MOBA_TOPK = 3
N_GROUPS = 4
EXPERTS_PER_GROUP = 4
N_EXPERTS = 16
EXPERT_HID = 256
ALPHA = (2 * DEPTH_LAYERS) ** 0.25
LN_EPS = 1e-5
RMS_EPS = 1e-5
NEG_BIG = -1e30
LANES = 128
KV_TILE = 256
VMEM_LIMIT = 56 * 1024 * 1024

SEGS = (
    ("sbq", 256, (0, 0)),
    ("sbkv", 512, (0, 0, 0, 0)),
    ("nq", 256, (1, 1)),
    ("nkv", 256, (2, 2)),
    ("wkv", 128, (2,)),
    ("gate", 128, (4,)),
    ("dq", 256, (3, 3)),
    ("dkv", 512, (3, 3, 0, 0)),
    ("mq", 256, (1, 1)),
    ("mkv", 512, (1, 1, 0, 0)),
    ("slc4", 512, (1, 1, 0, 0)),
    ("win4", 512, (1, 1, 0, 0)),
    ("cmp", 128, (2,)),
)
PROJ_COLS = sum(s[1] for s in SEGS)


def _dot(a, b):
    return jnp.dot(a, b, preferred_element_type=F32)


def _dot_t(a, b):
    return lax.dot_general(a, b, (((1,), (1,)), ((), ())), preferred_element_type=F32)


def _split_dot(a, b_bf):
    hi = a.astype(BF16)
    lo = (a - hi.astype(F32)).astype(BF16)
    return _dot(hi, b_bf) + _dot(lo, b_bf)


def _lane_group_masks(width, group, n):
    lane = lax.broadcasted_iota(jnp.int32, (1, width), 1)
    return [jnp.where((lane // group) == g, 1.0, 0.0).astype(F32) for g in range(n)]


def _stack_groups(q, masks):
    return jnp.concatenate([q * m for m in masks], axis=0)


def _col(x, idx):
    lane = lax.broadcasted_iota(jnp.int32, x.shape, 1)
    return jnp.sum(jnp.where(lane == idx, x, 0.0), axis=-1, keepdims=True)


def _params(n_axes):
    return pltpu.CompilerParams(dimension_semantics=("arbitrary",) * n_axes, vmem_limit_bytes=VMEM_LIMIT)


def _proj_kernel(x_ref, w_ref, tab_ref, *out_refs):
    xb = x_ref[...].astype(BF16)
    off = 0
    for (_, width, kinds), o_ref in zip(SEGS, out_refs):
        for c0 in range(0, width, 256):
            cw = min(256, width - c0)
            y2 = _dot(xb, w_ref[:, off + c0:off + c0 + cw])
            for cc in range(cw // LANES):
                y = y2[:, cc * LANES:(cc + 1) * LANES]
                kind = kinds[(c0 // LANES) + cc]
                if kind in (1, 2, 3):
                    half = 4 if kind == 3 else 8
                    t0 = (kind - 1) * 3
                    y = (y * tab_ref[t0] + pltpu.roll(y, LANES - half, 1) * tab_ref[t0 + 1]
                         + pltpu.roll(y, half, 1) * tab_ref[t0 + 2])
                elif kind == 4:
                    y = 1.0 / (1.0 + jnp.exp(-y))
                o_ref[:, c0 + cc * LANES:c0 + (cc + 1) * LANES] = y
        off += width


def _project(x2d, wcat, tabs, tm):
    n = x2d.shape[0]
    n_tab = tabs.shape[1] // tm
    return pl.pallas_call(
        _proj_kernel,
        grid=(n // tm,),
        in_specs=[pl.BlockSpec((tm, D_MODEL), lambda r: (r, 0)),
                  pl.BlockSpec((D_MODEL, PROJ_COLS), lambda r: (0, 0)),
                  pl.BlockSpec((9, tm, LANES), lambda r: (0, r % n_tab, 0))],
        out_specs=[pl.BlockSpec((tm, s[1]), lambda r: (r, 0)) for s in SEGS],
        out_shape=[jax.ShapeDtypeStruct((n, s[1]), F32) for s in SEGS],
        compiler_params=_params(1),
        name="proj",
    )(x2d, wcat, tabs)


def _rope_tables(pos):
    pos = pos.astype(F32)
    lane = np.arange(LANES)
    tabs = []
    for dim, first_only in ((HD, False), (HD, True), (DIFF_QK, False)):
        rot = dim // ROPE_FRACTION
        half = rot // 2
        inv = jnp.power(jnp.float32(ROPE_THETA), -jnp.arange(half, dtype=F32) * (2.0 / rot))
        ang = pos[:, None] * inv[None, :]
        cos, sin = jnp.cos(ang), jnp.sin(ang)
        c = lane % dim
        in_rot = c < rot
        if first_only:
            in_rot = in_rot & (lane < HD)
        first = in_rot & (c < half)
        second = in_rot & (c >= half)
        fidx = c % half
        cos_l, sin_l = cos[:, fidx], sin[:, fidx]
        tabs.append(jnp.where(in_rot[None, :], cos_l, 1.0))
        tabs.append(jnp.where(first[None, :], -sin_l, 0.0))
        tabs.append(jnp.where(second[None, :], sin_l, 0.0))
    return jnp.stack(tabs).astype(F32)


def _proj_weight(w_in_l):
    o = np.cumsum((0, 256, 256, 256, 256, 256, 128, 12, 256, 256, 256, 256, 256, 256))
    sl = lambda a, b: w_in_l[:, a:b]
    nkv0, wkv0 = int(o[4]), int(o[5])
    slck, slcv = sl(nkv0 + 128, nkv0 + 192), sl(nkv0 + 192, nkv0 + 256)
    wink, winv = sl(wkv0, wkv0 + 64), sl(wkv0 + 64, wkv0 + 128)
    parts = [
        sl(o[0], o[1]), sl(o[1], o[3]), sl(o[3], o[4]), sl(o[4], o[5]), sl(o[5], o[6]),
        jnp.pad(sl(o[6], o[7]), ((0, 0), (0, LANES - 12))),
        sl(o[7], o[8]), sl(o[8], o[10]), sl(o[10], o[11]), sl(o[11], o[13]),
        slck, slck, slck, slck, slcv, slcv, slcv, slcv,
        wink, wink, wink, wink, winv, winv, winv, winv,
        sl(nkv0, nkv0 + 128),
    ]
    return jnp.concatenate(parts, axis=1).astype(BF16)


def _qpos(p_lo, tq, rows):
    r = lax.broadcasted_iota(jnp.int32, (rows, 1), 0)
    return p_lo + (r & (tq - 1))


def _kpos(j, tk, rows):
    return j * tk + lax.broadcasted_iota(jnp.int32, (rows, tk), 1)


def _load_kv(kv_ref, j, tk):
    kv = kv_ref[0, pl.ds(pl.multiple_of(j * tk, tk), tk), :]
    return kv[:, :GROUP_W].astype(BF16), kv[:, GROUP_W:].astype(BF16)


def _flash_step(qb, k, v, mask, scale, carry):
    m, l, acc = carry
    s = _dot_t(qb, k) * scale
    s = jnp.where(mask, s, NEG_BIG)
    m_new = jnp.maximum(m, jnp.max(s, axis=-1, keepdims=True))
    a = jnp.exp(m - m_new)
    p = jnp.where(mask, jnp.exp(s - m_new), 0.0)
    l = a * l + jnp.sum(p, axis=-1, keepdims=True)
    acc = a * acc + _dot(p.astype(BF16), v)
    return m_new, l, acc


def _flash_init(rows):
    return (jnp.full((rows, 1), NEG_BIG, F32), jnp.zeros((rows, 1), F32), jnp.zeros((rows, GROUP_W), F32))


def _unstack_heads(rows_val, tq, masks):
    out = rows_val[0:tq] * masks[0]
    for h in range(1, HEADS):
        out = out + rows_val[h * tq:(h + 1) * tq] * masks[h]
    return out


def _sb_kernel(q_ref, kv_ref, o_ref, *, tq, tk, pos0):
    i = pl.program_id(1)
    p_lo = pos0 + i * tq
    rows = HEADS * tq
    masks = _lane_group_masks(GROUP_W, HD, HEADS)
    qb = _stack_groups(q_ref[0], masks).astype(BF16)
    qpos = _qpos(p_lo, tq, rows)
    r_i = lax.broadcasted_iota(jnp.int32, (tk, tk), 0)
    c_i = lax.broadcasted_iota(jnp.int32, (tk, tk), 1)
    later_mat = jnp.where(r_i > c_i, 1.0, 0.0).astype(BF16)
    n_tiles = (p_lo + tq - 2) // tk + 1

    def body(jj, carry):
        acc, tail = carry
        j = n_tiles - 1 - jj
        k, v = _load_kv(kv_ref, j, tk)
        z = _dot_t(qb, k) * (HD ** -0.5)
        mask = _kpos(j, tk, rows) < qpos
        log_sig = jnp.minimum(z, 0.0) - jnp.log1p(jnp.exp(-jnp.abs(z)))
        log_fail = jnp.where(mask, log_sig - z, 0.0)
        later = _split_dot(log_fail, later_mat) + tail
        w = jnp.where(mask, jnp.exp(log_sig + later), 0.0)
        acc = acc + _dot(w.astype(BF16), v)
        tail = tail + jnp.sum(log_fail, axis=-1, keepdims=True)
        return acc, tail

    acc, _ = lax.fori_loop(0, n_tiles, body, (jnp.zeros((rows, GROUP_W), F32), jnp.zeros((rows, 1), F32)))
    o_ref[0] = _unstack_heads(acc, tq, masks)


def _sb_attention(q, kv, tq, pos0):
    b, t, _ = q.shape
    l = kv.shape[1]
    return pl.pallas_call(
        functools.partial(_sb_kernel, tq=tq, tk=KV_TILE, pos0=pos0),
        grid=(b, t // tq),
        in_specs=[pl.BlockSpec((1, tq, GROUP_W), lambda bb, i: (bb, i, 0)),
                  pl.BlockSpec((1, l, 2 * GROUP_W), lambda bb, i: (bb, 0, 0))],
        out_specs=pl.BlockSpec((1, tq, GROUP_W), lambda bb, i: (bb, i, 0)),
        out_shape=jax.ShapeDtypeStruct((b, t, GROUP_W), F32),
        compiler_params=_params(2),
        name="sb_attn",
    )(q, kv)


def _diff_kernel(q_ref, kv_ref, lam_ref, sub_ref, o_ref, *, tq, tk, pos0, lam_init):
    i = pl.program_id(1)
    p_lo = pos0 + i * tq
    n_maps = 2 * HEADS
    rows = n_maps * tq
    map_masks = _lane_group_masks(GROUP_W, DIFF_QK, n_maps)
    head_masks = _lane_group_masks(GROUP_W, HD, HEADS)
    qb = _stack_groups(q_ref[0], map_masks).astype(BF16)
    qpos = _qpos(p_lo, tq, rows)
    n_tiles = (p_lo + tq - 1) // tk + 1

    def body(j, carry):
        k, v = _load_kv(kv_ref, j, tk)
        mask = _kpos(j, tk, rows) <= qpos
        return _flash_step(qb, k, v, mask, DIFF_QK ** -0.5, carry)

    _, l, acc = lax.fori_loop(0, n_tiles, body, _flash_init(rows))
    o = acc / l
    lp = lam_ref[...]
    lam = (jnp.exp(jnp.sum(lp[0:1] * lp[1:2], axis=-1, keepdims=True))
           - jnp.exp(jnp.sum(lp[2:3] * lp[3:4], axis=-1, keepdims=True)) + lam_init)
    out = jnp.zeros((tq, GROUP_W), F32)
    for h in range(HEADS):
        oh = (o[(2 * h) * tq:(2 * h + 1) * tq] - lam * o[(2 * h + 1) * tq:(2 * h + 2) * tq]) * head_masks[h]
        ms = jnp.sum(oh * oh, axis=-1, keepdims=True) * (1.0 / HD)
        out = out + oh * lax.rsqrt(ms + RMS_EPS)
    o_ref[0] = out * sub_ref[...] * (1.0 - lam_init)


def _diff_attention(q, kv, lam_p, subln4, tq, pos0, lam_init):
    b, t, _ = q.shape
    l = kv.shape[1]
    return pl.pallas_call(
        functools.partial(_diff_kernel, tq=tq, tk=KV_TILE, pos0=pos0, lam_init=lam_init),
        grid=(b, t // tq),
        in_specs=[pl.BlockSpec((1, tq, GROUP_W), lambda bb, i: (bb, i, 0)),
                  pl.BlockSpec((1, l, 2 * GROUP_W), lambda bb, i: (bb, 0, 0)),
                  pl.BlockSpec((4, DIFF_QK), lambda bb, i: (0, 0)),
                  pl.BlockSpec((1, GROUP_W), lambda bb, i: (0, 0))],
        out_specs=pl.BlockSpec((1, tq, GROUP_W), lambda bb, i: (bb, i, 0)),
        out_shape=jax.ShapeDtypeStruct((b, t, GROUP_W), F32),
        compiler_params=_params(2),
        name="diff_attn",
    )(q, kv, lam_p, subln4)


def _moba_kernel(q_ref, kv_ref, o_ref, km_ref, *, tq, tk, pos0, n_blocks):
    i = pl.program_id(1)
    p_lo = pos0 + i * tq
    rows = HEADS * tq
    masks = _lane_group_masks(GROUP_W, HD, HEADS)

    @pl.when(i == 0)
    def _():
        km_ref[...] = jnp.zeros_like(km_ref)
        for m in range(n_blocks):
            blk = kv_ref[0, m * MOBA_BLOCK:(m + 1) * MOBA_BLOCK, 0:GROUP_W]
            km_ref[m:m + 1, :] = jnp.sum(blk, axis=0, keepdims=True) * (1.0 / MOBA_BLOCK)

    qb = _stack_groups(q_ref[0], masks).astype(BF16)
    qpos = _qpos(p_lo, tq, rows)
    own = p_lo // MOBA_BLOCK
    gate = _dot_t(qb, km_ref[...].astype(BF16))
    lane = lax.broadcasted_iota(jnp.int32, (rows, LANES), 1)
    rank = jnp.zeros((rows, LANES), F32)
    for m in range(n_blocks):
        gm = gate[:, m:m + 1]
        beats = (gm > gate) | ((gm == gate) & (m < lane))
        rank = rank + jnp.where(beats & (m < own), 1.0, 0.0)
    sel = jnp.where((rank < MOBA_TOPK) & (lane < own), 1.0, 0.0)

    def body(j, carry):
        k, v = _load_kv(kv_ref, j, tk)
        picked = (_col(sel, j) + jnp.where(j == own, 1.0, 0.0)) > 0.5
        mask = (_kpos(j, tk, rows) <= qpos) & picked
        return _flash_step(qb, k, v, mask, HD ** -0.5, carry)

    _, l, acc = lax.fori_loop(0, own + 1, body, _flash_init(rows))
    o_ref[0] = _unstack_heads(acc / l, tq, masks)


def _moba_attention(q, kv, tq, pos0):
    b, t, _ = q.shape
    l = kv.shape[1]
    assert MOBA_BLOCK % tq == 0 and pos0 % tq == 0 and l % MOBA_BLOCK == 0
    return pl.pallas_call(
        functools.partial(_moba_kernel, tq=tq, tk=KV_TILE, pos0=pos0, n_blocks=l // MOBA_BLOCK),
        grid=(b, t // tq),
        in_specs=[pl.BlockSpec((1, tq, GROUP_W), lambda bb, i: (bb, i, 0)),
                  pl.BlockSpec((1, l, 2 * GROUP_W), lambda bb, i: (bb, 0, 0))],
        out_specs=pl.BlockSpec((1, tq, GROUP_W), lambda bb, i: (bb, i, 0)),
        out_shape=jax.ShapeDtypeStruct((b, t, GROUP_W), F32),
        scratch_shapes=[pltpu.VMEM((LANES, GROUP_W), F32)],
        compiler_params=_params(2),
        name="moba_attn",
    )(q, kv)


def _gelu_tanh(x):
    return 0.5 * x * (1.0 + jnp.tanh(math.sqrt(2.0 / math.pi) * (x + 0.044715 * (x * x * x))))


def _nsa_kernel(q_ref, g_ref, nkv_ref, slc_ref, win_ref, w1_ref, b1_ref, pe_ref, w2k_ref, w2v_ref,
                o_ref, ck_ref, cv_ref, *, tq, tk, pos0, n_cmp, n_sel):
    i = pl.program_id(1)
    p_lo = pos0 + i * tq
    rows = HEADS * tq
    ncp = ck_ref.shape[0]
    masks = _lane_group_masks(GROUP_W, HD, HEADS)

    @pl.when(i == 0)
    def _():
        row = lax.broadcasted_iota(jnp.int32, (n_cmp, 1), 0)
        hid = jnp.zeros((n_cmp, 2 * NSA_CMP_HID), F32)
        for l in range(NSA_STRIDE):
            x = nkv_ref[0, pl.ds(l, n_cmp, stride=NSA_STRIDE), :]
            xs = jnp.where(row == n_cmp - 1, 0.0, pltpu.roll(x, n_cmp - 1, 0))
            hid = hid + _dot((x + pe_ref[l:l + 1, :]).astype(BF16), w1_ref[l])
            hid = hid + _dot((xs + pe_ref[l + NSA_STRIDE:l + NSA_STRIDE + 1, :]).astype(BF16),
                             w1_ref[l + NSA_STRIDE])
        hid = _gelu_tanh(hid + b1_ref[...]).astype(BF16)
        ck_ref[...] = jnp.zeros_like(ck_ref)
        cv_ref[...] = jnp.zeros_like(cv_ref)
        ck_ref[0:n_cmp, :] = _dot(hid, w2k_ref[...])
        cv_ref[0:n_cmp, :] = _dot(hid, w2v_ref[...])

    qb = _stack_groups(q_ref[0], masks).astype(BF16)
    qpos = _qpos(p_lo, tq, rows)
    scale = HD ** -0.5

    s = _dot_t(qb, ck_ref[...].astype(BF16)) * scale
    c_end = lax.broadcasted_iota(jnp.int32, (rows, ncp), 1) * NSA_STRIDE + (NSA_CMP_LEN - 1)
    c_mask = c_end <= qpos
    s = jnp.where(c_mask, s, NEG_BIG)
    p = jnp.where(c_mask, jnp.exp(s - jnp.max(s, axis=-1, keepdims=True)), 0.0)
    den = jnp.sum(p, axis=-1, keepdims=True)
    p = p / jnp.where(den > 0, den, 1.0)
    o_cmp = _dot(p.astype(BF16), cv_ref[...].astype(BF16))

    p_sum = p[0:tq]
    for h in range(1, HEADS):
        p_sum = p_sum + p[h * tq:(h + 1) * tq]
    gi = lax.broadcasted_iota(jnp.int32, (ncp, LANES), 0)
    gj = lax.broadcasted_iota(jnp.int32, (ncp, LANES), 1)
    group_mat = jnp.where((gi // (NSA_SEL_BLOCK // NSA_STRIDE)) == gj, 1.0, 0.0).astype(BF16)
    imp = _split_dot(p_sum, group_mat)
    qpos_t = qpos[0:tq]
    qblk = qpos_t // NSA_SEL_BLOCK
    lane = lax.broadcasted_iota(jnp.int32, (tq, LANES), 1)
    valid = (lane <= qblk) & (lane < n_sel)
    forced = (lane == 0) | (lane == qblk) | (lane == qblk - 1)
    score = jnp.where(valid, jnp.where(forced, NSA_FORCE, imp), -jnp.inf)
    rank = jnp.zeros((tq, LANES), F32)
    for m in range(n_sel):
        sm = score[:, m:m + 1]
        beats = (sm > score) | ((sm == score) & (m < lane))
        rank = rank + jnp.where(beats, 1.0, 0.0)
    sel = jnp.where((rank < min(NSA_TOPN, n_sel)) & valid, 1.0, 0.0).astype(BF16)

    n_tiles = (p_lo + tq - 1) // tk + 1
    ei = lax.broadcasted_iota(jnp.int32, (LANES, tk), 0)
    ej = lax.broadcasted_iota(jnp.int32, (LANES, tk), 1) // NSA_SEL_BLOCK

    def slc_body(j, carry):
        k, v = _load_kv(slc_ref, j, tk)
        expand = jnp.where(ei == ej + j * (tk // NSA_SEL_BLOCK), 1.0, 0.0).astype(BF16)
        picked = _dot(sel, expand)
        picked = jnp.concatenate([picked] * HEADS, axis=0) > 0.5
        mask = picked & (_kpos(j, tk, rows) <= qpos)
        return _flash_step(qb, k, v, mask, scale, carry)

    _, l_s, acc_s = lax.fori_loop(0, n_tiles, slc_body, _flash_init(rows))

    def win_body(j, carry):
        k, v = _load_kv(win_ref, j, tk)
        kp = _kpos(j, tk, rows)
        mask = (kp <= qpos) & (kp > qpos - NSA_WINDOW)
        return _flash_step(qb, k, v, mask, scale, carry)

    j_lo = jnp.maximum(p_lo - (NSA_WINDOW - 1), 0) // tk
    _, l_w, acc_w = lax.fori_loop(j_lo, n_tiles, win_body, _flash_init(rows))

    o_slc = acc_s / l_s
    o_win = acc_w / l_w
    g = g_ref[0]
    out = jnp.zeros((tq, GROUP_W), F32)
    for h in range(HEADS):
        r0, r1 = h * tq, (h + 1) * tq
        mix = (g[:, 3 * h:3 * h + 1] * o_cmp[r0:r1] + g[:, 3 * h + 1:3 * h + 2] * o_slc[r0:r1]
               + g[:, 3 * h + 2:3 * h + 3] * o_win[r0:r1])
        out = out + mix * masks[h]
    o_ref[0] = out


def _nsa_attention(q, gates, nkv, slc4, win4, cmp_w, tq, pos0):
    b, t, _ = q.shape
    l = nkv.shape[1]
    w1cat, b1cat, pecat, w2k4, w2v4 = cmp_w
    n_cmp = l // NSA_STRIDE
    n_sel = l // NSA_SEL_BLOCK
    ncp = -(-n_cmp // LANES) * LANES
    assert n_sel <= LANES and l % KV_TILE == 0
    full = lambda *shape: pl.BlockSpec(shape, lambda bb, i: (0,) * len(shape))
    return pl.pallas_call(
        functools.partial(_nsa_kernel, tq=tq, tk=KV_TILE, pos0=pos0, n_cmp=n_cmp, n_sel=n_sel),
        grid=(b, t // tq),
        in_specs=[pl.BlockSpec((1, tq, GROUP_W), lambda bb, i: (bb, i, 0)),
                  pl.BlockSpec((1, tq, LANES), lambda bb, i: (bb, i, 0)),
                  pl.BlockSpec((1, l, LANES), lambda bb, i: (bb, 0, 0)),
                  pl.BlockSpec((1, l, 2 * GROUP_W), lambda bb, i: (bb, 0, 0)),
                  pl.BlockSpec((1, l, 2 * GROUP_W), lambda bb, i: (bb, 0, 0)),
                  full(NSA_CMP_LEN, LANES, 2 * NSA_CMP_HID), full(1, 2 * NSA_CMP_HID),
                  full(NSA_CMP_LEN, LANES), full(2 * NSA_CMP_HID, GROUP_W), full(2 * NSA_CMP_HID, GROUP_W)],
        out_specs=pl.BlockSpec((1, tq, GROUP_W), lambda bb, i: (bb, i, 0)),
        out_shape=jax.ShapeDtypeStruct((b, t, GROUP_W), F32),
        scratch_shapes=[pltpu.VMEM((ncp, GROUP_W), F32), pltpu.VMEM((ncp, GROUP_W), F32)],
        compiler_params=_params(2),
        name="nsa_attn",
    )(q, gates, nkv, slc4, win4, w1cat, b1cat, pecat, w2k4, w2v4)


def _nsa_cmp_weights(pe, w1, b1, w2):
    z = jnp.zeros((NSA_CMP_LEN, HD, NSA_CMP_HID), F32)
    w1cat = jnp.concatenate([jnp.concatenate([w1[0], z], axis=2), jnp.concatenate([z, w1[1]], axis=2)], axis=1)
    b1cat = jnp.concatenate([b1[0], b1[1]])[None, :]
    pecat = jnp.concatenate([pe[0], pe[1]], axis=1)
    z2 = jnp.zeros((NSA_CMP_HID, GROUP_W), F32)
    w2k4 = jnp.concatenate([jnp.tile(w2[0], (1, HEADS)), z2], axis=0)
    w2v4 = jnp.concatenate([z2, jnp.tile(w2[1], (1, HEADS))], axis=0)
    return w1cat.astype(BF16), b1cat, pecat, w2k4.astype(BF16), w2v4.astype(BF16)


def _layer_norm(x, g, b):
    xc = x - jnp.mean(x, axis=-1, keepdims=True)
    var = jnp.mean(xc * xc, axis=-1, keepdims=True)
    return xc * lax.rsqrt(var + LN_EPS) * g + b


def _outproj_kernel(x_ref, a_ref, b_ref, c_ref, d_ref, w_ref, g_ref, be_ref, o_ref):
    mix = _dot(a_ref[...].astype(BF16), w_ref[0:256, :])
    mix = mix + _dot(b_ref[...].astype(BF16), w_ref[256:512, :])
    mix = mix + _dot(c_ref[...].astype(BF16), w_ref[512:768, :])
    mix = mix + _dot(d_ref[...].astype(BF16), w_ref[768:1024, :])
    o_ref[...] = _layer_norm(ALPHA * x_ref[...] + mix, g_ref[...], be_ref[...])


def _outproj_norm(x2d, outs, w_out_bf, gain, bias, tm):
    n = x2d.shape[0]
    row = lambda w: pl.BlockSpec((tm, w), lambda r: (r, 0))
    return pl.pallas_call(
        _outproj_kernel,
        grid=(n // tm,),
        in_specs=[row(D_MODEL), row(GROUP_W), row(GROUP_W), row(GROUP_W), row(GROUP_W),
                  pl.BlockSpec((D_MODEL, D_MODEL), lambda r: (0, 0)),
                  pl.BlockSpec((1, D_MODEL), lambda r: (0, 0)), pl.BlockSpec((1, D_MODEL), lambda r: (0, 0))],
        out_specs=row(D_MODEL),
        out_shape=jax.ShapeDtypeStruct((n, D_MODEL), F32),
        compiler_params=_params(1),
        name="outproj_ln",
    )(x2d, *outs, w_out_bf, gain, bias)


def _moe_kernel(h_ref, wr_ref, rb_ref, wgu_ref, wd_ref, g_ref, be_ref, o_ref, hb_ref, comb_ref, acc_ref):
    e = pl.program_id(1)

    @pl.when(e == 0)
    def _():
        hb = h_ref[...].astype(BF16)
        hb_ref[...] = hb
        logit = _dot(hb, wr_ref[...]) + rb_ref[...]
        lane = lax.broadcasted_iota(jnp.int32, logit.shape, 1)
        is_g = lane < N_GROUPS
        lg = jnp.where(is_g, logit, NEG_BIG)
        gmax = jnp.max(lg, axis=-1, keepdims=True)
        gsel = jnp.min(jnp.where(is_g & (lg == gmax), lane, LANES), axis=-1, keepdims=True)
        pg = 1.0 / jnp.sum(jnp.where(is_g, jnp.exp(lg - gmax), 0.0), axis=-1, keepdims=True)
        e0 = N_GROUPS + gsel * EXPERTS_PER_GROUP
        in_g = (lane >= e0) & (lane < e0 + EXPERTS_PER_GROUP)
        le = jnp.where(in_g, logit, NEG_BIG)
        v1 = jnp.max(le, axis=-1, keepdims=True)
        i1 = jnp.min(jnp.where(in_g & (le == v1), lane, LANES), axis=-1, keepdims=True)
        rest = in_g & (lane != i1)
        le2 = jnp.where(rest, logit, NEG_BIG)
        v2 = jnp.max(le2, axis=-1, keepdims=True)
        i2 = jnp.min(jnp.where(rest & (le2 == v2), lane, LANES), axis=-1, keepdims=True)
        ex = jnp.exp(v2 - v1)
        w1 = pg / (1.0 + ex)
        w2 = pg * ex / (1.0 + ex)
        comb_ref[...] = jnp.where(lane == i1, w1, 0.0) + jnp.where(lane == i2, w2, 0.0)
        acc_ref[...] = jnp.zeros_like(acc_ref)

    gu = _dot(hb_ref[...], wgu_ref[0])
    gt, up = gu[:, :EXPERT_HID], gu[:, EXPERT_HID:]
    hid = gt / (1.0 + jnp.exp(-gt)) * up
    hid = hid * _col(comb_ref[...], N_GROUPS + e)
    acc_ref[...] += _dot(hid.astype(BF16), wd_ref[0])

    @pl.when(e == N_EXPERTS - 1)
    def _():
        o_ref[...] = _layer_norm(ALPHA * h_ref[...] + acc_ref[...], g_ref[...], be_ref[...])


def _moe_norm(h2d, wr, rb, wgu, wd, gain, bias, tm):
    n = h2d.shape[0]
    assert n % tm == 0
    return pl.pallas_call(
        _moe_kernel,
        grid=(n // tm, N_EXPERTS),
        in_specs=[pl.BlockSpec((tm, D_MODEL), lambda r, e: (r, 0)),
                  pl.BlockSpec((D_MODEL, LANES), lambda r, e: (0, 0)),
                  pl.BlockSpec((1, LANES), lambda r, e: (0, 0)),
                  pl.BlockSpec((1, D_MODEL, 2 * EXPERT_HID), lambda r, e: (e, 0, 0)),
                  pl.BlockSpec((1, EXPERT_HID, D_MODEL), lambda r, e: (e, 0, 0)),
                  pl.BlockSpec((1, D_MODEL), lambda r, e: (0, 0)),
                  pl.BlockSpec((1, D_MODEL), lambda r, e: (0, 0))],
        out_specs=pl.BlockSpec((tm, D_MODEL), lambda r, e: (r, 0)),
        out_shape=jax.ShapeDtypeStruct((n, D_MODEL), F32),
        scratch_shapes=[pltpu.VMEM((tm, D_MODEL), BF16), pltpu.VMEM((tm, LANES), F32),
                        pltpu.VMEM((tm, D_MODEL), F32)],
        compiler_params=_params(2),
        name="moe_ln",
    )(h2d, wr, rb, wgu, wd, gain, bias)


def _layer_weights(l, w_in, w_out, pe, w1, b1, w2, lam_p, subln, ln_g, ln_b, rgw, rgb, rew, reb, wg, wu, wd):
    wr = jnp.concatenate([rgw[l], rew[l], jnp.zeros((D_MODEL, LANES - N_GROUPS - N_EXPERTS), F32)], axis=1)
    rb = jnp.concatenate([rgb[l], reb[l], jnp.zeros((LANES - N_GROUPS - N_EXPERTS,), F32)])[None, :]
    return dict(
        wcat=_proj_weight(w_in[l]),
        w_out=w_out[l].astype(BF16),
        cmp=_nsa_cmp_weights(pe[l], w1[l], b1[l], w2[l]),
        lam_p=lam_p[l],
        subln4=jnp.tile(subln[l], HEADS)[None, :],
        lam_init=0.8 - 0.6 * math.exp(-0.3 * l),
        g1=ln_g[l, 0][None, :], b1=ln_b[l, 0][None, :], g2=ln_g[l, 1][None, :], b2=ln_b[l, 1][None, :],
        wr=wr.astype(BF16), rb=rb,
        wgu=jnp.concatenate([wg[l], wu[l]], axis=2).astype(BF16),
        wd=wd[l].astype(BF16),
    )


def _mix_and_ffn(x, p, ctx, lw, tabs_tm, tq, tq_diff, pos0, tm_ffn):
    b, t, _ = x.shape
    r3 = lambda a: a.reshape(b, t, a.shape[-1])
    o_sb = _sb_attention(r3(p["sbq"]), ctx["sb"], tq, pos0)
    o_nsa = _nsa_attention(r3(p["nq"]), r3(p["gate"]), ctx["cmp"], ctx["slc4"], ctx["win4"], lw["cmp"], tq, pos0)
    o_diff = _diff_attention(r3(p["dq"]), ctx["diff"], lw["lam_p"], lw["subln4"], tq_diff, pos0, lw["lam_init"])
    o_moba = _moba_attention(r3(p["mq"]), ctx["moba"], tq, pos0)
    x2d = x.reshape(b * t, D_MODEL)
    outs = [o.reshape(b * t, GROUP_W) for o in (o_sb, o_nsa, o_diff, o_moba)]
    h = _outproj_norm(x2d, outs, lw["w_out"], lw["g1"], lw["b1"], tabs_tm)
    y = _moe_norm(h, lw["wr"], lw["rb"], lw["wgu"], lw["wd"], lw["g2"], lw["b2"], tm_ffn)
    return y.reshape(b, t, D_MODEL)


def _pad_ctx(a, l_pad):
    return jnp.pad(a, ((0, 0), (0, l_pad - a.shape[1]), (0, 0)))


def kernel(x_prompt, x_sample, cache_sb_kv, cache_nsa_kv, cache_diff_kv, cache_moba_kv, state_nsa_win, page_table, w_in, w_out, nsa_cmp_pe, nsa_cmp_w1, nsa_cmp_b1, nsa_cmp_w2, diff_lambda, diff_subln, ln_gain, ln_bias, router_group_w, router_group_b, router_expert_w, router_expert_b, expert_w_gate, expert_w_up, expert_w_down):
    bp, tp, _ = x_prompt.shape
    bs, ts, _ = x_sample.shape
    depth = w_in.shape[0]
    n_pages, page = page_table.shape[1], cache_sb_kv.shape[2]
    past = n_pages * page
    w_eff = state_nsa_win.shape[2]
    l_s = past + ts
    l_s_pad = -(-l_s // KV_TILE) * KV_TILE
    tm_p = 256
    tm_s = min(256, bs * ts)
    assert tp % KV_TILE == 0 and tm_p % ts == 0 and (bs * ts) % tm_s == 0 and w_eff <= past

    tabs_p = _rope_tables(jnp.arange(tp, dtype=jnp.int32))
    tabs_s = _rope_tables(past + (jnp.arange(tm_s, dtype=jnp.int32) % ts))

    y_p, y_s = x_prompt, x_sample
    new_p, new_s = [], []
    for l in range(depth):
        lw = _layer_weights(l, w_in, w_out, nsa_cmp_pe, nsa_cmp_w1, nsa_cmp_b1, nsa_cmp_w2, diff_lambda,
                            diff_subln, ln_gain, ln_bias, router_group_w, router_group_b, router_expert_w,
                            router_expert_b, expert_w_gate, expert_w_up, expert_w_down)
        pp = dict(zip([s[0] for s in SEGS], _project(y_p.reshape(bp * tp, D_MODEL), lw["wcat"], tabs_p, tm_p)))
        r3 = lambda a, b_, t_: a.reshape(b_, t_, a.shape[-1])
        ctx_p = dict(sb=r3(pp["sbkv"], bp, tp), cmp=r3(pp["cmp"], bp, tp), slc4=r3(pp["slc4"], bp, tp),
                     win4=r3(pp["win4"], bp, tp), diff=r3(pp["dkv"], bp, tp), moba=r3(pp["mkv"], bp, tp))
        y_p = _mix_and_ffn(y_p, pp, ctx_p, lw, tm_p, 256, 128, 0, math.gcd(1024, bp * tp))
        wkv_p = r3(pp["wkv"], bp, tp)
        new_p.append((pp["sbkv"].reshape(bp, tp, 2, HEADS, HD), pp["nkv"].reshape(bp, tp, 4, HD),
                      pp["dkv"].reshape(bp, tp, 2, HEADS, HD), pp["mkv"].reshape(bp, tp, 2, HEADS, HD),
                      wkv_p[:, tp - min(NSA_WINDOW, tp):].reshape(bp, min(NSA_WINDOW, tp), 2, HD)))

        ps = dict(zip([s[0] for s in SEGS], _project(y_s.reshape(bs * ts, D_MODEL), lw["wcat"], tabs_s, tm_s)))

        def paged(pool, width):
            rows = pool[l][page_table].reshape(bs, past, width)
            return rows

        def with_new(past_rows, new_rows):
            return _pad_ctx(jnp.concatenate([past_rows, new_rows], axis=1), l_s_pad)

        nkv_ctx = with_new(paged(cache_nsa_kv, 4 * HD), r3(ps["nkv"], bs, ts))
        slck, slcv = nkv_ctx[..., 2 * HD:3 * HD], nkv_ctx[..., 3 * HD:4 * HD]
        slc4_ctx = jnp.concatenate([slck] * HEADS + [slcv] * HEADS, axis=-1)
        win_rows = jnp.concatenate([state_nsa_win[l].reshape(bs, w_eff, 2 * HD), r3(ps["wkv"], bs, ts)], axis=1)
        wink, winv = win_rows[..., :HD], win_rows[..., HD:]
        win4_rows = jnp.concatenate([wink] * HEADS + [winv] * HEADS, axis=-1)
        win4_ctx = jnp.pad(win4_rows, ((0, 0), (past - w_eff, l_s_pad - l_s), (0, 0)))
        ctx_s = dict(sb=with_new(paged(cache_sb_kv, 2 * GROUP_W), r3(ps["sbkv"], bs, ts)),
                     cmp=nkv_ctx[..., :2 * HD], slc4=slc4_ctx, win4=win4_ctx,
                     diff=with_new(paged(cache_diff_kv, 2 * GROUP_W), r3(ps["dkv"], bs, ts)),
                     moba=with_new(paged(cache_moba_kv, 2 * GROUP_W), r3(ps["mkv"], bs, ts)))
        y_s = _mix_and_ffn(y_s, ps, ctx_s, lw, tm_s, ts, ts, past, math.gcd(1024, bs * ts))
        new_s.append((ps["sbkv"].reshape(bs, ts, 2, HEADS, HD), ps["nkv"].reshape(bs, ts, 4, HD),
                      ps["dkv"].reshape(bs, ts, 2, HEADS, HD), ps["mkv"].reshape(bs, ts, 2, HEADS, HD),
                      win_rows[:, ts:].reshape(bs, w_eff, 2, HD)))

    outs_p = [jnp.stack([r[k] for r in new_p]) for k in range(5)]
    outs_s = [jnp.stack([r[k] for r in new_s]) for k in range(5)]
    return (y_p, y_s, *outs_p, *outs_s)
```

```python
import functools
import math

import numpy as np
import jax
import jax.numpy as jnp
from jax import lax
from jax.experimental import pallas as pl
from jax.experimental.pallas import tpu as pltpu

F32 = jnp.float32
BF16 = jnp.bfloat16

D_MODEL = 1024
HEADS = 4
HD = 64
GROUP_W = HEADS * HD
DIFF_QK = HD // 2
DEPTH_LAYERS = 2
ROPE_THETA = 500000.0
ROPE_FRACTION = 4
NSA_STRIDE = 16
NSA_CMP_LEN = 32
NSA_CMP_HID = 128
NSA_SEL_BLOCK = 64
NSA_TOPN = 16
NSA_WINDOW = 512
NSA_FORCE = 1.0e4
MOBA_BLOCK = 256
MOBA_TOPK = 3
N_GROUPS = 4
EXPERTS_PER_GROUP = 4
N_EXPERTS = 16
EXPERT_HID = 256
ALPHA = (2 * DEPTH_LAYERS) ** 0.25
LN_EPS = 1e-5
RMS_EPS = 1e-5
NEG_BIG = -1e30
LANES = 128
KV_TILE = 256
VMEM_LIMIT = 56 * 1024 * 1024

Q_SEGS = (("sbq", 256, (0, 0)), ("nq", 256, (1, 1)), ("dq", 256, (3, 3)), ("mq", 256, (1, 1)),
          ("gate", 128, (4,)))
KV_SEGS = (("sbkv", 512, (0, 0, 0, 0)), ("nkv", 256, (2, 2)), ("wkv", 128, (2,)),
           ("dkv", 512, (3, 3, 0, 0)), ("mkv", 512, (1, 1, 0, 0)))
Q_COLS = sum(s[1] for s in Q_SEGS)
KV_COLS = sum(s[1] for s in KV_SEGS)


def _dot(a, b):
    return jnp.dot(a, b, preferred_element_type=F32)


def _dot_t(a, b):
    return lax.dot_general(a, b, (((1,), (1,)), ((), ())), preferred_element_type=F32)


def _split_dot(a, b_bf):
    hi = a.astype(BF16)
    lo = (a - hi.astype(F32)).astype(BF16)
    return _dot(hi, b_bf) + _dot(lo, b_bf)


def _lane_group_masks(width, group, n):
    lane = lax.broadcasted_iota(jnp.int32, (1, width), 1)
    return [jnp.where((lane // group) == g, 1.0, 0.0).astype(F32) for g in range(n)]


def _stack_groups(q, masks):
    return jnp.concatenate([q * m for m in masks], axis=0)


def _col(x, idx):
    lane = lax.broadcasted_iota(jnp.int32, x.shape, 1)
    return jnp.sum(jnp.where(lane == idx, x, 0.0), axis=-1, keepdims=True)


def _params(n_axes):
    return pltpu.CompilerParams(dimension_semantics=("arbitrary",) * n_axes, vmem_limit_bytes=VMEM_LIMIT)


def _rope_rows(y, tab_ref, kind):
    half = 4 if kind == 3 else 8
    t0 = (kind - 1) * 3
    return (y * tab_ref[t0] + pltpu.roll(y, LANES - half, 1) * tab_ref[t0 + 1]
            + pltpu.roll(y, half, 1) * tab_ref[t0 + 2])


def _rope_cols(y, tab_ref, kind):
    half = 4 if kind == 3 else 8
    t0 = (kind - 1) * 3
    return (y * tab_ref[t0] + pltpu.roll(y, LANES - half, 0) * tab_ref[t0 + 1]
            + pltpu.roll(y, half, 0) * tab_ref[t0 + 2])


def _proj_kernel(x_ref, w_ref, wt_ref, tab_ref, tabt_ref, *out_refs, row_segs, col_segs):
    xb = x_ref[...].astype(BF16)
    off = 0
    n_row = len(row_segs)
    for (_, width, kinds), o_ref in zip(row_segs, out_refs[:n_row]):
        for c0 in range(0, width, 256):
            cw = min(256, width - c0)
            y2 = _dot(xb, w_ref[:, off + c0:off + c0 + cw])
            for cc in range(cw // LANES):
                y = y2[:, cc * LANES:(cc + 1) * LANES]
                kind = kinds[(c0 // LANES) + cc]
                if kind in (1, 2, 3):
                    y = _rope_rows(y, tab_ref, kind)
                elif kind == 4:
                    y = 1.0 / (1.0 + jnp.exp(-y))
                o_ref[:, c0 + cc * LANES:c0 + (cc + 1) * LANES] = y
        off += width
    off = 0
    for (_, width, kinds), o_ref in zip(col_segs, out_refs[n_row:]):
        for c0 in range(0, width, 256):
            cw = min(256, width - c0)
            y2 = _dot_t(wt_ref[off + c0:off + c0 + cw, :], xb)
            for cc in range(cw // LANES):
                y = y2[cc * LANES:(cc + 1) * LANES, :]
                kind = kinds[(c0 // LANES) + cc]
                if kind in (1, 2, 3):
                    y = _rope_cols(y, tabt_ref, kind)
                o_ref[0, c0 + cc * LANES:c0 + (cc + 1) * LANES, :] = y
        off += width


def _project(x2d, w_rows, w_cols, tabs, tabs_t, tm, row_segs, col_segs, seq):
    n = x2d.shape[0]
    n_tab = tabs.shape[1] // tm
    per_seq = max(seq // tm, 1)
    in_specs = [pl.BlockSpec((tm, D_MODEL), lambda r: (r, 0)),
                pl.BlockSpec(w_rows.shape, lambda r: (0, 0)),
                pl.BlockSpec(w_cols.shape, lambda r: (0, 0)),
                pl.BlockSpec((9, tm, LANES), lambda r: (0, r % n_tab, 0)),
                pl.BlockSpec((9, LANES, tm), lambda r: (0, 0, r % n_tab))]
    out_specs = [pl.BlockSpec((tm, s[1]), lambda r: (r, 0)) for s in row_segs]
    out_specs += [pl.BlockSpec((1, s[1], tm), lambda r: (r // per_seq, 0, r % per_seq)) for s in col_segs]
    out_shape = [jax.ShapeDtypeStruct((n, s[1]), F32) for s in row_segs]
    out_shape += [jax.ShapeDtypeStruct((n // seq, s[1], seq), F32) for s in col_segs]
    outs = pl.pallas_call(
        functools.partial(_proj_kernel, row_segs=row_segs, col_segs=col_segs),
        grid=(n // tm,),
        in_specs=in_specs, out_specs=out_specs, out_shape=out_shape,
        compiler_params=_params(1),
        name="proj",
    )(x2d, w_rows, w_cols, tabs, tabs_t)
    return dict(zip([s[0] for s in row_segs + col_segs], outs))


def _rope_tables(pos):
    pos = pos.astype(F32)
    lane = np.arange(LANES)
    tabs = []
    for dim, first_only in ((HD, False), (HD, True), (DIFF_QK, False)):
        rot = dim // ROPE_FRACTION
        half = rot // 2
        inv = jnp.power(jnp.float32(ROPE_THETA), -jnp.arange(half, dtype=F32) * (2.0 / rot))
        ang = pos[:, None] * inv[None, :]
        cos, sin = jnp.cos(ang), jnp.sin(ang)
        c = lane % dim
        in_rot = c < rot
        if first_only:
            in_rot = in_rot & (lane < HD)
        first = in_rot & (c < half)
        second = in_rot & (c >= half)
        fidx = c % half
        cos_l, sin_l = cos[:, fidx], sin[:, fidx]
        tabs.append(jnp.where(in_rot[None, :], cos_l, 1.0))
        tabs.append(jnp.where(first[None, :], -sin_l, 0.0))
        tabs.append(jnp.where(second[None, :], sin_l, 0.0))
    return jnp.stack(tabs).astype(F32)


def _proj_weights(w_in_l):
    o = [int(v) for v in np.cumsum((0, 256, 256, 256, 256, 256, 128, 12, 256, 256, 256, 256, 256, 256))]
    sl = lambda a, b: w_in_l[:, a:b]
    gate = jnp.pad(sl(o[6], o[7]), ((0, 0), (0, LANES - 12)))
    wq = jnp.concatenate([sl(o[0], o[1]), sl(o[3], o[4]), sl(o[7], o[8]), sl(o[10], o[11]), gate], axis=1)
    wkv = jnp.concatenate([sl(o[1], o[3]), sl(o[4], o[5]), sl(o[5], o[6]), sl(o[8], o[10]), sl(o[11], o[13])],
                          axis=1)
    return wq.astype(BF16), wkv.astype(BF16)


def _qpos(p_lo, tq, rows):
    r = lax.broadcasted_iota(jnp.int32, (rows, 1), 0)
    return p_lo + (r & (tq - 1))


def _kpos(kpos0, tk, rows):
    return kpos0 + lax.broadcasted_iota(jnp.int32, (rows, tk), 1)


def _flash_step(qb, kt, vt, mask, scale, carry):
    m, l, acc = carry
    s = _dot(qb, kt) * scale
    s = jnp.where(mask, s, NEG_BIG)
    m_new = jnp.maximum(m, jnp.max(s, axis=-1, keepdims=True))
    a = jnp.exp(m - m_new)
    p = jnp.where(mask, jnp.exp(s - m_new), 0.0)
    l = a * l + jnp.sum(p, axis=-1, keepdims=True)
    acc = a * acc + _dot_t(p.astype(BF16), vt)
    return m_new, l, acc


def _flash_init(rows):
    return (jnp.full((rows, 1), NEG_BIG, F32), jnp.zeros((rows, 1), F32), jnp.zeros((rows, GROUP_W), F32))


def _unstack_heads(rows_val, tq, masks):
    out = rows_val[0:tq] * masks[0]
    for h in range(1, HEADS):
        out = out + rows_val[h * tq:(h + 1) * tq] * masks[h]
    return out


def _later_matrix(tk):
    r_i = lax.broadcasted_iota(jnp.int32, (tk, tk), 0)
    c_i = lax.broadcasted_iota(jnp.int32, (tk, tk), 1)
    return jnp.where(r_i > c_i, 1.0, 0.0).astype(BF16)


def _sb_step(qb, kt, vt, kpos0, qpos, later_mat, carry):
    acc, tail = carry
    rows, tk = qb.shape[0], kt.shape[1]
    z = _dot(qb, kt) * (HD ** -0.5)
    mask = _kpos(kpos0, tk, rows) < qpos
    log_sig = jnp.minimum(z, 0.0) - jnp.log1p(jnp.exp(-jnp.abs(z)))
    log_fail = jnp.where(mask, log_sig - z, 0.0)
    later = _split_dot(log_fail, later_mat) + tail
    w = jnp.where(mask, jnp.exp(log_sig + later), 0.0)
    acc = acc + _dot_t(w.astype(BF16), vt)
    tail = tail + jnp.sum(log_fail, axis=-1, keepdims=True)
    return acc, tail


def _tile_from_ref(ref, idx, j, tk):
    cols = pl.ds(pl.multiple_of(j * tk, tk), tk)
    return (ref[idx + (slice(0, GROUP_W), cols)].astype(BF16),
            ref[idx + (slice(GROUP_W, 2 * GROUP_W), cols)].astype(BF16))


def _page_tile(ref):
    idx = (0,) * (len(ref.shape) - 2)
    return (ref[idx + (slice(0, GROUP_W), slice(None))].astype(BF16),
            ref[idx + (slice(GROUP_W, 2 * GROUP_W), slice(None))].astype(BF16))


def _sb_kernel(q_ref, kv_ref, o_ref, *, tq, tk, pos0):
    i = pl.program_id(1)
    p_lo = pos0 + i * tq
    rows = HEADS * tq
    masks = _lane_group_masks(GROUP_W, HD, HEADS)
    qb = _stack_groups(q_ref[0], masks).astype(BF16)
    qpos = _qpos(p_lo, tq, rows)
    later_mat = _later_matrix(tk)
    n_tiles = (p_lo + tq - 2) // tk + 1

    def body(jj, carry):
        j = n_tiles - 1 - jj
        kt, vt = _tile_from_ref(kv_ref, (0,), j, tk)
        return _sb_step(qb, kt, vt, j * tk, qpos, later_mat, carry)

    acc, _ = lax.fori_loop(0, n_tiles, body, (jnp.zeros((rows, GROUP_W), F32), jnp.zeros((rows, 1), F32)))
    o_ref[0] = _unstack_heads(acc, tq, masks)


def _sb_sample_kernel(pt_ref, q_ref, *refs, n_pages, page, past):
    page_refs, new_ref, o_ref = refs[:n_pages], refs[n_pages], refs[n_pages + 1]
    tq = q_ref.shape[1]
    rows = HEADS * tq
    masks = _lane_group_masks(GROUP_W, HD, HEADS)
    qb = _stack_groups(q_ref[0], masks).astype(BF16)
    qpos = _qpos(past, tq, rows)
    later_mat = _later_matrix(page)
    carry = (jnp.zeros((rows, GROUP_W), F32), jnp.zeros((rows, 1), F32))
    kt, vt = _page_tile(new_ref)
    carry = _sb_step(qb, kt, vt, past, qpos, later_mat, carry)
    for p in reversed(range(n_pages)):
        kt, vt = _page_tile(page_refs[p])
        carry = _sb_step(qb, kt, vt, p * page, qpos, later_mat, carry)
    o_ref[0] = _unstack_heads(carry[0], tq, masks)


def _prompt_call(kern, q, kv_t, extra, extra_specs, tq, name, scratch=()):
    b, t, _ = q.shape
    rows_kv, l = kv_t.shape[1], kv_t.shape[2]
    return pl.pallas_call(
        kern,
        grid=(b, t // tq),
        in_specs=[pl.BlockSpec((1, tq, GROUP_W), lambda bb, i: (bb, i, 0)),
                  pl.BlockSpec((1, rows_kv, l), lambda bb, i: (bb, 0, 0))] + list(extra_specs),
        out_specs=pl.BlockSpec((1, tq, GROUP_W), lambda bb, i: (bb, i, 0)),
        out_shape=jax.ShapeDtypeStruct((b, t, GROUP_W), F32),
        scratch_shapes=list(scratch),
        compiler_params=_params(2),
        name=name,
    )(q, kv_t, *extra)


def _sample_call(kern, page_table, q, pool_t, layer, new_t, extra, extra_specs, name, scratch=()):
    b, tq, _ = q.shape
    n_pages = page_table.shape[1]
    rows_kv, page = pool_t.shape[2], pool_t.shape[3]

    def page_spec(p):
        return pl.BlockSpec((1, 1, rows_kv, page), lambda bb, pt: (layer, pt[bb * n_pages + p], 0, 0))

    in_specs = ([pl.BlockSpec((1, tq, GROUP_W), lambda bb, pt: (bb, 0, 0))]
                + [page_spec(p) for p in range(n_pages)]
                + [pl.BlockSpec((1, rows_kv, page), lambda bb, pt: (bb, 0, 0))] + list(extra_specs))
    return pl.pallas_call(
        kern,
        grid_spec=pltpu.PrefetchScalarGridSpec(
            num_scalar_prefetch=1, grid=(b,), in_specs=in_specs,
            out_specs=pl.BlockSpec((1, tq, GROUP_W), lambda bb, pt: (bb, 0, 0)),
            scratch_shapes=list(scratch)),
        out_shape=jax.ShapeDtypeStruct((b, tq, GROUP_W), F32),
        compiler_params=_params(1),
        name=name,
    )(page_table.reshape(-1), q, *([pool_t] * n_pages), new_t, *extra)


def _diff_finish(carry, tq, lam_ref, sub_ref, lam_init):
    _, l, acc = carry
    head_masks = _lane_group_masks(GROUP_W, HD, HEADS)
    o = acc / l
    lp = lam_ref[...]
    lam = (jnp.exp(jnp.sum(lp[0:1] * lp[1:2], axis=-1, keepdims=True))
           - jnp.exp(jnp.sum(lp[2:3] * lp[3:4], axis=-1, keepdims=True)) + lam_init)
    out = jnp.zeros((tq, GROUP_W), F32)
    for h in range(HEADS):
        oh = (o[(2 * h) * tq:(2 * h + 1) * tq] - lam * o[(2 * h + 1) * tq:(2 * h + 2) * tq]) * head_masks[h]
        ms = jnp.sum(oh * oh, axis=-1, keepdims=True) * (1.0 / HD)
        out = out + oh * lax.rsqrt(ms + RMS_EPS)
    return out * sub_ref[...] * (1.0 - lam_init)


def _diff_kernel(q_ref, kv_ref, lam_ref, sub_ref, o_ref, *, tq, tk, pos0, lam_init):
    i = pl.program_id(1)
    p_lo = pos0 + i * tq
    rows = 2 * HEADS * tq
    qb = _stack_groups(q_ref[0], _lane_group_masks(GROUP_W, DIFF_QK, 2 * HEADS)).astype(BF16)
    qpos = _qpos(p_lo, tq, rows)
    n_tiles = (p_lo + tq - 1) // tk + 1

    def body(j, carry):
        kt, vt = _tile_from_ref(kv_ref, (0,), j, tk)
        mask = _kpos(j * tk, tk, rows) <= qpos
        return _flash_step(qb, kt, vt, mask, DIFF_QK ** -0.5, carry)

    carry = lax.fori_loop(0, n_tiles, body, _flash_init(rows))
    o_ref[0] = _diff_finish(carry, tq, lam_ref, sub_ref, lam_init)


def _diff_sample_kernel(pt_ref, q_ref, *refs, n_pages, page, past, lam_init):
    page_refs, new_ref = refs[:n_pages], refs[n_pages]
    lam_ref, sub_ref, o_ref = refs[n_pages + 1:n_pages + 4]
    tq = q_ref.shape[1]
    rows = 2 * HEADS * tq
    qb = _stack_groups(q_ref[0], _lane_group_masks(GROUP_W, DIFF_QK, 2 * HEADS)).astype(BF16)
    qpos = _qpos(past, tq, rows)
    carry = _flash_init(rows)
    for p in range(n_pages + 1):
        kt, vt = _page_tile(page_refs[p] if p < n_pages else new_ref)
        mask = _kpos(p * page, page, rows) <= qpos
        carry = _flash_step(qb, kt, vt, mask, DIFF_QK ** -0.5, carry)
    o_ref[0] = _diff_finish(carry, tq, lam_ref, sub_ref, lam_init)


def _moba_select(qb, km_t, own, n_blocks):
    gate = _dot(qb, km_t.astype(BF16))
    rows = qb.shape[0]
    lane = lax.broadcasted_iota(jnp.int32, (rows, LANES), 1)
    rank = jnp.zeros((rows, LANES), F32)
    for m in range(n_blocks):
        gm = gate[:, m:m + 1]
        beats = (gm > gate) | ((gm == gate) & (m < lane))
        rank = rank + jnp.where(beats & (m < own), 1.0, 0.0)
    return jnp.where((rank < MOBA_TOPK) & (lane < own), 1.0, 0.0)


def _block_mean_col(kt_f32, m):
    lane = lax.broadcasted_iota(jnp.int32, (GROUP_W, LANES), 1)
    return jnp.where(lane == m, jnp.sum(kt_f32, axis=-1, keepdims=True), 0.0)


def _moba_kernel(q_ref, kv_ref, o_ref, km_ref, *, tq, tk, pos0, n_blocks):
    i = pl.program_id(1)
    p_lo = pos0 + i * tq
    rows = HEADS * tq
    masks = _lane_group_masks(GROUP_W, HD, HEADS)

    @pl.when(i == 0)
    def _():
        km = jnp.zeros((GROUP_W, LANES), F32)
        for m in range(n_blocks):
            km = km + _block_mean_col(kv_ref[0, 0:GROUP_W, m * MOBA_BLOCK:(m + 1) * MOBA_BLOCK], m)
        km_ref[...] = km * (1.0 / MOBA_BLOCK)

    qb = _stack_groups(q_ref[0], masks).astype(BF16)
    qpos = _qpos(p_lo, tq, rows)
    own = p_lo // MOBA_BLOCK
    sel = _moba_select(qb, km_ref[...], own, n_blocks)

    def body(j, carry):
        kt, vt = _tile_from_ref(kv_ref, (0,), j, tk)
        picked = (_col(sel, j) + jnp.where(j == own, 1.0, 0.0)) > 0.5
        mask = (_kpos(j * tk, tk, rows) <= qpos) & picked
        return _flash_step(qb, kt, vt, mask, HD ** -0.5, carry)

    _, l, acc = lax.fori_loop(0, own + 1, body, _flash_init(rows))
    o_ref[0] = _unstack_heads(acc / l, tq, masks)


def _moba_sample_kernel(pt_ref, q_ref, *refs, n_pages, page, past):
    page_refs, new_ref, o_ref = refs[:n_pages], refs[n_pages], refs[n_pages + 1]
    tq = q_ref.shape[1]
    rows = HEADS * tq
    per_blk = MOBA_BLOCK // page
    own = past // MOBA_BLOCK
    masks = _lane_group_masks(GROUP_W, HD, HEADS)
    km = jnp.zeros((GROUP_W, LANES), F32)
    for p in range(n_pages):
        km = km + _block_mean_col(page_refs[p][0, 0, 0:GROUP_W, :], p // per_blk)
    km = km * (1.0 / MOBA_BLOCK)
    qb = _stack_groups(q_ref[0], masks).astype(BF16)
    qpos = _qpos(past, tq, rows)
    sel = _moba_select(qb, km, own, own)
    carry = _flash_init(rows)
    for p in range(n_pages + 1):
        kt, vt = _page_tile(page_refs[p] if p < n_pages else new_ref)
        causal = _kpos(p * page, page, rows) <= qpos
        mask = causal if p == n_pages else causal & (sel[:, p // per_blk:p // per_blk + 1] > 0.5)
        carry = _flash_step(qb, kt, vt, mask, HD ** -0.5, carry)
    _, l, acc = carry
    o_ref[0] = _unstack_heads(acc / l, tq, masks)


def _gelu_tanh(x):
    return 0.5 * x * (1.0 + jnp.tanh(math.sqrt(2.0 / math.pi) * (x + 0.044715 * (x * x * x))))


def _nsa_compress(rows_ref, w1_ref, b1_ref, pe_ref, w2k_ref, w2v_ref, ck_ref, cv_ref, n_cmp):
    row = lax.broadcasted_iota(jnp.int32, (n_cmp, 1), 0)
    hid = jnp.zeros((n_cmp, 2 * NSA_CMP_HID), F32)
    for l in range(NSA_STRIDE):
        x = rows_ref[pl.ds(l, n_cmp, stride=NSA_STRIDE), :]
        xs = jnp.where(row == n_cmp - 1, 0.0, pltpu.roll(x, n_cmp - 1, 0))
        hid = hid + _dot((x + pe_ref[l:l + 1, :]).astype(BF16), w1_ref[l])
        hid = hid + _dot((xs + pe_ref[l + NSA_STRIDE:l + NSA_STRIDE + 1, :]).astype(BF16),
                         w1_ref[l + NSA_STRIDE])
    hid = _gelu_tanh(hid + b1_ref[...]).astype(BF16)
    ck_ref[...] = jnp.zeros_like(ck_ref)
    cv_ref[...] = jnp.zeros_like(cv_ref)
    ck_ref[0:n_cmp, :] = _dot(hid, w2k_ref[...])
    cv_ref[0:n_cmp, :] = _dot(hid, w2v_ref[...])


def _nsa_cmp_and_select(qb, qpos, ck_ref, cv_ref, tq, n_sel):
    rows = qb.shape[0]
    ncp = ck_ref.shape[0]
    s = _dot_t(qb, ck_ref[...].astype(BF16)) * (HD ** -0.5)
    c_end = lax.broadcasted_iota(jnp.int32, (rows, ncp), 1) * NSA_STRIDE + (NSA_CMP_LEN - 1)
    c_mask = c_end <= qpos
    s = jnp.where(c_mask, s, NEG_BIG)
    p = jnp.where(c_mask, jnp.exp(s - jnp.max(s, axis=-1, keepdims=True)), 0.0)
    den = jnp.sum(p, axis=-1, keepdims=True)
    p = p / jnp.where(den > 0, den, 1.0)
    o_cmp = _dot(p.astype(BF16), cv_ref[...].astype(BF16))
    p_sum = p[0:tq]
    for h in range(1, HEADS):
        p_sum = p_sum + p[h * tq:(h + 1) * tq]
    gi = lax.broadcasted_iota(jnp.int32, (ncp, LANES), 0)
    gj = lax.broadcasted_iota(jnp.int32, (ncp, LANES), 1)
    group_mat = jnp.where((gi // (NSA_SEL_BLOCK // NSA_STRIDE)) == gj, 1.0, 0.0).astype(BF16)
    imp = _split_dot(p_sum, group_mat)
    qblk = qpos[0:tq] // NSA_SEL_BLOCK
    lane = lax.broadcasted_iota(jnp.int32, (tq, LANES), 1)
    valid = (lane <= qblk) & (lane < n_sel)
    forced = (lane == 0) | (lane == qblk) | (lane == qblk - 1)
    score = jnp.where(valid, jnp.where(forced, NSA_FORCE, imp), -jnp.inf)
    rank = jnp.zeros((tq, LANES), F32)
    for m in range(n_sel):
        sm = score[:, m:m + 1]
        beats = (sm > score) | ((sm == score) & (m < lane))
        rank = rank + jnp.where(beats, 1.0, 0.0)
    sel = jnp.where((rank < min(NSA_TOPN, n_sel)) & valid, 1.0, 0.0)
    return o_cmp, sel


def _shared_tile(k64, v64):
    return (jnp.concatenate([k64] * HEADS, axis=0).astype(BF16),
            jnp.concatenate([v64] * HEADS, axis=0).astype(BF16))


def _slc_mask(sel_bf, blk0, kpos0, tk, qpos, rows):
    ei = lax.broadcasted_iota(jnp.int32, (LANES, tk), 0)
    ej = lax.broadcasted_iota(jnp.int32, (LANES, tk), 1) // NSA_SEL_BLOCK
    expand = jnp.where(ei == ej + blk0, 1.0, 0.0).astype(BF16)
    picked = _dot(sel_bf, expand)
    picked = jnp.concatenate([picked] * HEADS, axis=0) > 0.5
    return picked & (_kpos(kpos0, tk, rows) <= qpos)


def _win_mask(kpos0, tk, qpos, rows):
    kp = _kpos(kpos0, tk, rows)
    return (kp <= qpos) & (kp > qpos - NSA_WINDOW)


def _nsa_combine(g, o_cmp, slc, win, tq, masks):
    o_slc = slc[2] / slc[1]
    o_win = win[2] / win[1]
    out = jnp.zeros((tq, GROUP_W), F32)
    for h in range(HEADS):
        r0, r1 = h * tq, (h + 1) * tq
        mix = (g[:, 3 * h:3 * h + 1] * o_cmp[r0:r1] + g[:, 3 * h + 1:3 * h + 2] * o_slc[r0:r1]
               + g[:, 3 * h + 2:3 * h + 3] * o_win[r0:r1])
        out = out + mix * masks[h]
    return out


def _nsa_kernel(q_ref, nkv_ref, g_ref, wkv_ref, w1_ref, b1_ref, pe_ref, w2k_ref, w2v_ref,
                o_ref, rows_ref, ck_ref, cv_ref, *, tq, tk, pos0, n_cmp, n_sel):
    i = pl.program_id(1)
    p_lo = pos0 + i * tq
    rows = HEADS * tq
    masks = _lane_group_masks(GROUP_W, HD, HEADS)

    @pl.when(i == 0)
    def _():
        for c in range(nkv_ref.shape[2] // LANES):
            rows_ref[c * LANES:(c + 1) * LANES, :] = nkv_ref[0, 0:LANES, c * LANES:(c + 1) * LANES].T
        _nsa_compress(rows_ref, w1_ref, b1_ref, pe_ref, w2k_ref, w2v_ref, ck_ref, cv_ref, n_cmp)

    qb = _stack_groups(q_ref[0], masks).astype(BF16)
    qpos = _qpos(p_lo, tq, rows)
    scale = HD ** -0.5
    o_cmp, sel = _nsa_cmp_and_select(qb, qpos, ck_ref, cv_ref, tq, n_sel)
    sel_bf = sel.astype(BF16)
    n_tiles = (p_lo + tq - 1) // tk + 1

    def slc_body(j, carry):
        cols = pl.ds(pl.multiple_of(j * tk, tk), tk)
        kt, vt = _shared_tile(nkv_ref[0, 2 * HD:3 * HD, cols], nkv_ref[0, 3 * HD:4 * HD, cols])
        mask = _slc_mask(sel_bf, j * (tk // NSA_SEL_BLOCK), j * tk, tk, qpos, rows)
        return _flash_step(qb, kt, vt, mask, scale, carry)

    slc = lax.fori_loop(0, n_tiles, slc_body, _flash_init(rows))

    def win_body(j, carry):
        cols = pl.ds(pl.multiple_of(j * tk, tk), tk)
        kt, vt = _shared_tile(wkv_ref[0, 0:HD, cols], wkv_ref[0, HD:2 * HD, cols])
        return _flash_step(qb, kt, vt, _win_mask(j * tk, tk, qpos, rows), scale, carry)

    j_lo = jnp.maximum(p_lo - (NSA_WINDOW - 1), 0) // tk
    win = lax.fori_loop(j_lo, n_tiles, win_body, _flash_init(rows))
    o_ref[0] = _nsa_combine(g_ref[0], o_cmp, slc, win, tq, masks)


def _nsa_sample_kernel(pt_ref, q_ref, *refs, n_pages, page, past, n_cmp, n_sel, w_eff):
    page_refs, new_ref = refs[:n_pages], refs[n_pages]
    (g_ref, wst_ref, wnew_ref, w1_ref, b1_ref, pe_ref, w2k_ref, w2v_ref,
     o_ref, rows_ref, ck_ref, cv_ref) = refs[n_pages + 1:]
    tq = q_ref.shape[1]
    rows = HEADS * tq
    masks = _lane_group_masks(GROUP_W, HD, HEADS)
    rows_ref[...] = jnp.zeros_like(rows_ref)
    for p in range(n_pages):
        rows_ref[p * page:(p + 1) * page, :] = page_refs[p][0, 0, 0:LANES, :].T
    rows_ref[n_pages * page:(n_pages + 1) * page, :] = new_ref[0, 0:LANES, :].T
    _nsa_compress(rows_ref, w1_ref, b1_ref, pe_ref, w2k_ref, w2v_ref, ck_ref, cv_ref, n_cmp)

    qb = _stack_groups(q_ref[0], masks).astype(BF16)
    qpos = _qpos(past, tq, rows)
    scale = HD ** -0.5
    o_cmp, sel = _nsa_cmp_and_select(qb, qpos, ck_ref, cv_ref, tq, n_sel)
    sel_bf = sel.astype(BF16)
    per_tile = page // NSA_SEL_BLOCK
    slc = _flash_init(rows)
    for p in range(n_pages + 1):
        if p < n_pages:
            k64, v64 = page_refs[p][0, 0, 2 * HD:3 * HD, :], page_refs[p][0, 0, 3 * HD:4 * HD, :]
        else:
            k64, v64 = new_ref[0, 2 * HD:3 * HD, :], new_ref[0, 3 * HD:4 * HD, :]
        kt, vt = _shared_tile(k64, v64)
        slc = _flash_step(qb, kt, vt, _slc_mask(sel_bf, p * per_tile, p * page, page, qpos, rows), scale, slc)
    win = _flash_init(rows)
    for c in range(w_eff // LANES + 1):
        if c < w_eff // LANES:
            k64, v64 = wst_ref[0, 0, 0:HD, c * LANES:(c + 1) * LANES], wst_ref[0, 0, HD:2 * HD, c * LANES:(c + 1) * LANES]
        else:
            k64, v64 = wnew_ref[0, 0:HD, :], wnew_ref[0, HD:2 * HD, :]
        kt, vt = _shared_tile(k64, v64)
        win = _flash_step(qb, kt, vt, _win_mask(past - w_eff + c * LANES, LANES, qpos, rows), scale, win)
    o_ref[0] = _nsa_combine(g_ref[0], o_cmp, slc, win, tq, masks)


def _nsa_cmp_weights(pe, w1, b1, w2):
    z = jnp.zeros((NSA_CMP_LEN, HD, NSA_CMP_HID), F32)
    w1cat = jnp.concatenate([jnp.concatenate([w1[0], z], axis=2), jnp.concatenate([z, w1[1]], axis=2)], axis=1)
    b1cat = jnp.concatenate([b1[0], b1[1]])[None, :]
    pecat = jnp.concatenate([pe[0], pe[1]], axis=1)
    z2 = jnp.zeros((NSA_CMP_HID, GROUP_W), F32)
    w2k4 = jnp.concatenate([jnp.tile(w2[0], (1, HEADS)), z2], axis=0)
    w2v4 = jnp.concatenate([z2, jnp.tile(w2[1], (1, HEADS))], axis=0)
    return w1cat.astype(BF16), b1cat, pecat, w2k4.astype(BF16), w2v4.astype(BF16)


def _cmp_specs(index_map):
    full = lambda *shape: pl.BlockSpec(shape, lambda *a: (0,) * len(shape))
    return [full(NSA_CMP_LEN, LANES, 2 * NSA_CMP_HID), full(1, 2 * NSA_CMP_HID),
            full(NSA_CMP_LEN, LANES), full(2 * NSA_CMP_HID, GROUP_W), full(2 * NSA_CMP_HID, GROUP_W)]


def _layer_norm(x, g, b):
    xc = x - jnp.mean(x, axis=-1, keepdims=True)
    var = jnp.mean(xc * xc, axis=-1, keepdims=True)
    return xc * lax.rsqrt(var + LN_EPS) * g + b


def _outproj_kernel(x_ref, a_ref, b_ref, c_ref, d_ref, w_ref, g_ref, be_ref, o_ref):
    mix = _dot(a_ref[...].astype(BF16), w_ref[0:256, :])
    mix = mix + _dot(b_ref[...].astype(BF16), w_ref[256:512, :])
    mix = mix + _dot(c_ref[...].astype(BF16), w_ref[512:768, :])
    mix = mix + _dot(d_ref[...].astype(BF16), w_ref[768:1024, :])
    o_ref[...] = _layer_norm(ALPHA * x_ref[...] + mix, g_ref[...], be_ref[...])


def _outproj_norm(x2d, outs, w_out_bf, gain, bias, tm):
    n = x2d.shape[0]
    row = lambda w: pl.BlockSpec((tm, w), lambda r: (r, 0))
    return pl.pallas_call(
        _outproj_kernel,
        grid=(n // tm,),
        in_specs=[row(D_MODEL), row(GROUP_W), row(GROUP_W), row(GROUP_W), row(GROUP_W),
                  pl.BlockSpec((D_MODEL, D_MODEL), lambda r: (0, 0)),
                  pl.BlockSpec((1, D_MODEL), lambda r: (0, 0)), pl.BlockSpec((1, D_MODEL), lambda r: (0, 0))],
        out_specs=row(D_MODEL),
        out_shape=jax.ShapeDtypeStruct((n, D_MODEL), F32),
        compiler_params=_params(1),
        name="outproj_ln",
    )(x2d, *outs, w_out_bf, gain, bias)


def _moe_kernel(h_ref, wr_ref, rb_ref, wgu_ref, wd_ref, g_ref, be_ref, o_ref, hb_ref, comb_ref, acc_ref):
    e = pl.program_id(1)

    @pl.when(e == 0)
    def _():
        hb = h_ref[...].astype(BF16)
        hb_ref[...] = hb
        logit = _dot(hb, wr_ref[...]) + rb_ref[...]
        lane = lax.broadcasted_iota(jnp.int32, logit.shape, 1)
        is_g = lane < N_GROUPS
        lg = jnp.where(is_g, logit, NEG_BIG)
        gmax = jnp.max(lg, axis=-1, keepdims=True)
        gsel = jnp.min(jnp.where(is_g & (lg == gmax), lane, LANES), axis=-1, keepdims=True)
        pg = 1.0 / jnp.sum(jnp.where(is_g, jnp.exp(lg - gmax), 0.0), axis=-1, keepdims=True)
        e0 = N_GROUPS + gsel * EXPERTS_PER_GROUP
        in_g = (lane >= e0) & (lane < e0 + EXPERTS_PER_GROUP)
        le = jnp.where(in_g, logit, NEG_BIG)
        v1 = jnp.max(le, axis=-1, keepdims=True)
        i1 = jnp.min(jnp.where(in_g & (le == v1), lane, LANES), axis=-1, keepdims=True)
        rest = in_g & (lane != i1)
        le2 = jnp.where(rest, logit, NEG_BIG)
        v2 = jnp.max(le2, axis=-1, keepdims=True)
        i2 = jnp.min(jnp.where(rest & (le2 == v2), lane, LANES), axis=-1, keepdims=True)
        ex = jnp.exp(v2 - v1)
        w1 = pg / (1.0 + ex)
        w2 = pg * ex / (1.0 + ex)
        comb_ref[...] = jnp.where(lane == i1, w1, 0.0) + jnp.where(lane == i2, w2, 0.0)
        acc_ref[...] = jnp.zeros_like(acc_ref)

    gu = _dot(hb_ref[...], wgu_ref[0])
    gt, up = gu[:, :EXPERT_HID], gu[:, EXPERT_HID:]
    hid = gt / (1.0 + jnp.exp(-gt)) * up
    hid = hid * _col(comb_ref[...], N_GROUPS + e)
    acc_ref[...] += _dot(hid.astype(BF16), wd_ref[0])

    @pl.when(e == N_EXPERTS - 1)
    def _():
        o_ref[...] = _layer_norm(ALPHA * h_ref[...] + acc_ref[...], g_ref[...], be_ref[...])


def _moe_norm(h2d, wr, rb, wgu, wd, gain, bias, tm):
    n = h2d.shape[0]
    assert n % tm == 0
    return pl.pallas_call(
        _moe_kernel,
        grid=(n // tm, N_EXPERTS),
        in_specs=[pl.BlockSpec((tm, D_MODEL), lambda r, e: (r, 0)),
                  pl.BlockSpec((D_MODEL, LANES), lambda r, e: (0, 0)),
                  pl.BlockSpec((1, LANES), lambda r, e: (0, 0)),
                  pl.BlockSpec((1, D_MODEL, 2 * EXPERT_HID), lambda r, e: (e, 0, 0)),
                  pl.BlockSpec((1, EXPERT_HID, D_MODEL), lambda r, e: (e, 0, 0)),
                  pl.BlockSpec((1, D_MODEL), lambda r, e: (0, 0)),
                  pl.BlockSpec((1, D_MODEL), lambda r, e: (0, 0))],
        out_specs=pl.BlockSpec((tm, D_MODEL), lambda r, e: (r, 0)),
        out_shape=jax.ShapeDtypeStruct((n, D_MODEL), F32),
        scratch_shapes=[pltpu.VMEM((tm, D_MODEL), BF16), pltpu.VMEM((tm, LANES), F32),
                        pltpu.VMEM((tm, D_MODEL), F32)],
        compiler_params=_params(2),
        name="moe_ln",
    )(h2d, wr, rb, wgu, wd, gain, bias)


def _layer_weights(l, w_in, w_out, pe, w1, b1, w2, lam_p, subln, ln_g, ln_b, rgw, rgb, rew, reb, wg, wu, wd):
    wr = jnp.concatenate([rgw[l], rew[l], jnp.zeros((D_MODEL, LANES - N_GROUPS - N_EXPERTS), F32)], axis=1)
    rb = jnp.concatenate([rgb[l], reb[l], jnp.zeros((LANES - N_GROUPS - N_EXPERTS,), F32)])[None, :]
    wq, wkv = _proj_weights(w_in[l])
    return dict(
        wq=wq, wkv=wkv, wkv_t=wkv.T, wall=jnp.concatenate([wq, wkv], axis=1),
        w_out=w_out[l].astype(BF16),
        cmp=_nsa_cmp_weights(pe[l], w1[l], b1[l], w2[l]),
        lam_p=lam_p[l],
        subln4=jnp.tile(subln[l], HEADS)[None, :],
        lam_init=0.8 - 0.6 * math.exp(-0.3 * l),
        g1=ln_g[l, 0][None, :], b1=ln_b[l, 0][None, :], g2=ln_g[l, 1][None, :], b2=ln_b[l, 1][None, :],
        wr=wr.astype(BF16), rb=rb,
        wgu=jnp.concatenate([wg[l], wu[l]], axis=2).astype(BF16),
        wd=wd[l].astype(BF16),
    )


def _ffn(x, outs, lw, tm, tm_ffn):
    b, t, _ = x.shape
    x2d = x.reshape(b * t, D_MODEL)
    h = _outproj_norm(x2d, [o.reshape(b * t, GROUP_W) for o in outs], lw["w_out"], lw["g1"], lw["b1"], tm)
    y = _moe_norm(h, lw["wr"], lw["rb"], lw["wgu"], lw["wd"], lw["g2"], lw["b2"], tm_ffn)
    return y.reshape(b, t, D_MODEL)


def _new_tile(rows, b, t, page):
    w = rows.shape[-1]
    return jnp.pad(jnp.transpose(rows.reshape(b, t, w), (0, 2, 1)), ((0, 0), (0, 0), (0, page - t)))


def kernel(x_prompt, x_sample, cache_sb_kv, cache_nsa_kv, cache_diff_kv, cache_moba_kv, state_nsa_win, page_table, w_in, w_out, nsa_cmp_pe, nsa_cmp_w1, nsa_cmp_b1, nsa_cmp_w2, diff_lambda, diff_subln, ln_gain, ln_bias, router_group_w, router_group_b, router_expert_w, router_expert_b, expert_w_gate, expert_w_up, expert_w_down):
    bp, tp, _ = x_prompt.shape
    bs, ts, _ = x_sample.shape
    depth = w_in.shape[0]
    n_phys, page = cache_sb_kv.shape[1], cache_sb_kv.shape[2]
    n_pages = page_table.shape[1]
    past = n_pages * page
    w_eff = state_nsa_win.shape[2]
    l_s = past + ts
    l_s_pad = -(-l_s // MOBA_BLOCK) * MOBA_BLOCK
    tm_p = 256
    tm_s = min(256, bs * ts)
    assert tp % KV_TILE == 0 and tm_p % ts == 0 and (bs * ts) % tm_s == 0 and ts <= page
    assert page == LANES and past % MOBA_BLOCK == 0 and w_eff % LANES == 0 and w_eff == min(NSA_WINDOW, past)

    pool_sb = jnp.transpose(cache_sb_kv, (0, 1, 3, 4, 5, 2)).reshape(depth, n_phys, 2 * GROUP_W, page)
    pool_diff = jnp.transpose(cache_diff_kv, (0, 1, 3, 4, 5, 2)).reshape(depth, n_phys, 2 * GROUP_W, page)
    pool_moba = jnp.transpose(cache_moba_kv, (0, 1, 3, 4, 5, 2)).reshape(depth, n_phys, 2 * GROUP_W, page)
    pool_nsa = jnp.transpose(cache_nsa_kv, (0, 1, 3, 4, 2)).reshape(depth, n_phys, 4 * HD, page)
    win_state = jnp.transpose(state_nsa_win, (0, 1, 3, 4, 2)).reshape(depth, bs, 2 * HD, w_eff)

    tabs_p = _rope_tables(jnp.arange(tp, dtype=jnp.int32))
    tabs_s = _rope_tables(past + (jnp.arange(tm_s, dtype=jnp.int32) % ts))
    tabs_p_t, tabs_s_t = jnp.transpose(tabs_p, (0, 2, 1)), jnp.transpose(tabs_s, (0, 2, 1))
    n_cmp_p, n_sel_p = tp // NSA_STRIDE, tp // NSA_SEL_BLOCK
    n_cmp_s, n_sel_s = l_s_pad // NSA_STRIDE, l_s_pad // NSA_SEL_BLOCK
    ncp = lambda n: -(-n // LANES) * LANES

    y_p, y_s = x_prompt, x_sample
    new_p, new_s = [], []
    for l in range(depth):
        lw = _layer_weights(l, w_in, w_out, nsa_cmp_pe, nsa_cmp_w1, nsa_cmp_b1, nsa_cmp_w2, diff_lambda,
                            diff_subln, ln_gain, ln_bias, router_group_w, router_group_b, router_expert_w,
                            router_expert_b, expert_w_gate, expert_w_up, expert_w_down)
        lam_specs = [pl.BlockSpec((4, DIFF_QK), lambda *a: (0, 0)), pl.BlockSpec((1, GROUP_W), lambda *a: (0, 0))]

        pp = _project(y_p.reshape(bp * tp, D_MODEL), lw["wq"], lw["wkv_t"], tabs_p, tabs_p_t, tm_p,
                      Q_SEGS, KV_SEGS, tp)
        q3 = lambda a: a.reshape(bp, tp, a.shape[-1])
        o_sb = _prompt_call(functools.partial(_sb_kernel, tq=256, tk=KV_TILE, pos0=0),
                            q3(pp["sbq"]), pp["sbkv"], (), (), 256, "sb_attn")
        o_nsa = _prompt_call(
            functools.partial(_nsa_kernel, tq=256, tk=KV_TILE, pos0=0, n_cmp=n_cmp_p, n_sel=n_sel_p),
            q3(pp["nq"]), pp["nkv"], (q3(pp["gate"]), pp["wkv"]) + lw["cmp"],
            [pl.BlockSpec((1, 256, LANES), lambda bb, i: (bb, i, 0)),
             pl.BlockSpec((1, 2 * HD, tp), lambda bb, i: (bb, 0, 0))] + _cmp_specs(None), 256, "nsa_attn",
            scratch=[pltpu.VMEM((tp, LANES), F32), pltpu.VMEM((ncp(n_cmp_p), GROUP_W), F32),
                     pltpu.VMEM((ncp(n_cmp_p), GROUP_W), F32)])
        o_diff = _prompt_call(
            functools.partial(_diff_kernel, tq=128, tk=KV_TILE, pos0=0, lam_init=lw["lam_init"]),
            q3(pp["dq"]), pp["dkv"], (lw["lam_p"], lw["subln4"]), lam_specs, 128, "diff_attn")
        o_moba = _prompt_call(
            functools.partial(_moba_kernel, tq=256, tk=KV_TILE, pos0=0, n_blocks=tp // MOBA_BLOCK),
            q3(pp["mq"]), pp["mkv"], (), (), 256, "moba_attn", scratch=[pltpu.VMEM((GROUP_W, LANES), F32)])
        y_p = _ffn(y_p, (o_sb, o_nsa, o_diff, o_moba), lw, tm_p, math.gcd(1024, bp * tp))
        new_p.append((pp["sbkv"], pp["nkv"], pp["dkv"], pp["mkv"], pp["wkv"][:, :, tp - min(NSA_WINDOW, tp):]))

        ps = _project(y_s.reshape(bs * ts, D_MODEL), lw["wall"], lw["wkv_t"], tabs_s, tabs_s_t, tm_s,
                      Q_SEGS + KV_SEGS, (), ts)
        s3 = lambda a: a.reshape(bs, ts, a.shape[-1])
        new_tile = lambda name: _new_tile(ps[name], bs, ts, page)
        common = dict(n_pages=n_pages, page=page, past=past)
        o_sb = _sample_call(functools.partial(_sb_sample_kernel, **common), page_table, s3(ps["sbq"]),
                            pool_sb, l, new_tile("sbkv"), (), (), "sb_attn_s")
        o_nsa = _sample_call(
            functools.partial(_nsa_sample_kernel, n_cmp=n_cmp_s, n_sel=n_sel_s, w_eff=w_eff, **common),
            page_table, s3(ps["nq"]), pool_nsa, l, new_tile("nkv"),
            (s3(ps["gate"]), win_state, new_tile("wkv")) + lw["cmp"],
            [pl.BlockSpec((1, ts, LANES), lambda bb, pt: (bb, 0, 0)),
             pl.BlockSpec((1, 1, 2 * HD, w_eff), lambda bb, pt: (l, bb, 0, 0)),
             pl.BlockSpec((1, 2 * HD, page), lambda bb, pt: (bb, 0, 0))] + _cmp_specs(None), "nsa_attn_s",
            scratch=[pltpu.VMEM((l_s_pad, LANES), F32), pltpu.VMEM((ncp(n_cmp_s), GROUP_W), F32),
                     pltpu.VMEM((ncp(n_cmp_s), GROUP_W), F32)])
        o_diff = _sample_call(functools.partial(_diff_sample_kernel, lam_init=lw["lam_init"], **common),
                              page_table, s3(ps["dq"]), pool_diff, l, new_tile("dkv"),
                              (lw["lam_p"], lw["subln4"]), lam_specs, "diff_attn_s")
        o_moba = _sample_call(functools.partial(_moba_sample_kernel, **common), page_table, s3(ps["mq"]),
                              pool_moba, l, new_tile("mkv"), (), (), "moba_attn_s")
        y_s = _ffn(y_s, (o_sb, o_nsa, o_diff, o_moba), lw, tm_s, math.gcd(1024, bs * ts))
        win_new = jnp.concatenate([win_state[l][:, :, ts:], new_tile("wkv")[:, :, :ts]], axis=2)
        new_s.append((ps["sbkv"], ps["nkv"], ps["dkv"], ps["mkv"], win_new))

    def heads_out(a, b, t):
        return jnp.transpose(a.reshape(depth, b, 2, HEADS, HD, t), (0, 1, 5, 2, 3, 4))

    def slots_out(a, b, t, n):
        return jnp.transpose(a.reshape(depth, b, n, HD, t), (0, 1, 4, 2, 3))

    st_p = [jnp.stack([r[k] for r in new_p]) for k in range(5)]
    st_s = [jnp.stack([r[k] for r in new_s]) for k in range(5)]
    outs_p = (heads_out(st_p[0], bp, tp), slots_out(st_p[1], bp, tp, 4), heads_out(st_p[2], bp, tp),
              heads_out(st_p[3], bp, tp), slots_out(st_p[4], bp, min(NSA_WINDOW, tp), 2))
    outs_s = (st_s[0].reshape(depth, bs, ts, 2, HEADS, HD), st_s[1].reshape(depth, bs, ts, 4, HD),
              st_s[2].reshape(depth, bs, ts, 2, HEADS, HD), st_s[3].reshape(depth, bs, ts, 2, HEADS, HD),
              slots_out(st_s[4], bs, w_eff, 2))
    return (y_p, y_s, *outs_p, *outs_s)
```

```python
import functools
import math

import numpy as np
import jax
import jax.numpy as jnp
from jax import lax
from jax.experimental import pallas as pl
from jax.experimental.pallas import tpu as pltpu

F32 = jnp.float32
BF16 = jnp.bfloat16

D_MODEL = 1024
HEADS = 4
HD = 64
GROUP_W = HEADS * HD
DIFF_QK = HD // 2
DEPTH_LAYERS = 2
ROPE_THETA = 500000.0
ROPE_FRACTION = 4
NSA_STRIDE = 16
NSA_CMP_LEN = 32
NSA_CMP_HID = 128
NSA_SEL_BLOCK = 64
NSA_TOPN = 16
NSA_WINDOW = 512
NSA_FORCE = 1.0e4
MOBA_BLOCK = 256
MOBA_TOPK = 3
N_GROUPS = 4
EXPERTS_PER_GROUP = 4
N_EXPERTS = 16
EXPERT_HID = 256
ALPHA = (2 * DEPTH_LAYERS) ** 0.25
LN_EPS = 1e-5
RMS_EPS = 1e-5
NEG_BIG = -1e30
LANES = 128
KV_TILE = 256
VMEM_LIMIT = 56 * 1024 * 1024

Q_SEGS = (("sbq", 256, (0, 0)), ("nq", 256, (1, 1)), ("dq", 256, (3, 3)), ("mq", 256, (1, 1)),
          ("gate", 128, (4,)))
KV_SEGS = (("sbkv", 512, (0, 0, 0, 0)), ("nkv", 256, (2, 2)), ("wkv", 128, (2,)),
           ("dkv", 512, (3, 3, 0, 0)), ("mkv", 512, (1, 1, 0, 0)))
Q_COLS = sum(s[1] for s in Q_SEGS)
KV_COLS = sum(s[1] for s in KV_SEGS)


def _dot(a, b):
    return jnp.dot(a, b, preferred_element_type=F32)


def _dot_t(a, b):
    return lax.dot_general(a, b, (((1,), (1,)), ((), ())), preferred_element_type=F32)


def _split_dot(a, b_bf):
    hi = a.astype(BF16)
    lo = (a - hi.astype(F32)).astype(BF16)
    return _dot(hi, b_bf) + _dot(lo, b_bf)


def _lane_group_masks(width, group, n):
    lane = lax.broadcasted_iota(jnp.int32, (1, width), 1)
    return [jnp.where((lane // group) == g, 1.0, 0.0).astype(F32) for g in range(n)]


def _stack_groups(q, masks):
    return jnp.concatenate([q * m for m in masks], axis=0)


def _col(x, idx):
    lane = lax.broadcasted_iota(jnp.int32, x.shape, 1)
    return jnp.sum(jnp.where(lane == idx, x, 0.0), axis=-1, keepdims=True)


def _params(n_axes):
    return pltpu.CompilerParams(dimension_semantics=("arbitrary",) * n_axes, vmem_limit_bytes=VMEM_LIMIT)


def _rope_rows(y, tab_ref, kind):
    half = 4 if kind == 3 else 8
    t0 = (kind - 1) * 3
    return (y * tab_ref[t0] + pltpu.roll(y, LANES - half, 1) * tab_ref[t0 + 1]
            + pltpu.roll(y, half, 1) * tab_ref[t0 + 2])


def _rope_cols(y, tab_ref, kind):
    half = 4 if kind == 3 else 8
    t0 = (kind - 1) * 3
    return (y * tab_ref[t0] + pltpu.roll(y, LANES - half, 0) * tab_ref[t0 + 1]
            + pltpu.roll(y, half, 0) * tab_ref[t0 + 2])


def _proj_kernel(x_ref, w_ref, wt_ref, tab_ref, tabt_ref, *out_refs, row_segs, col_segs):
    xb = x_ref[...].astype(BF16)
    off = 0
    n_row = len(row_segs)
    for (_, width, kinds), o_ref in zip(row_segs, out_refs[:n_row]):
        for c0 in range(0, width, 256):
            cw = min(256, width - c0)
            y2 = _dot(xb, w_ref[:, off + c0:off + c0 + cw])
            for cc in range(cw // LANES):
                y = y2[:, cc * LANES:(cc + 1) * LANES]
                kind = kinds[(c0 // LANES) + cc]
                if kind in (1, 2, 3):
                    y = _rope_rows(y, tab_ref, kind)
                elif kind == 4:
                    y = 1.0 / (1.0 + jnp.exp(-y))
                o_ref[:, c0 + cc * LANES:c0 + (cc + 1) * LANES] = y
        off += width
    off = 0
    for (_, width, kinds), o_ref in zip(col_segs, out_refs[n_row:]):
        for c0 in range(0, width, 256):
            cw = min(256, width - c0)
            y2 = _dot_t(wt_ref[off + c0:off + c0 + cw, :], xb)
            for cc in range(cw // LANES):
                y = y2[cc * LANES:(cc + 1) * LANES, :]
                kind = kinds[(c0 // LANES) + cc]
                if kind in (1, 2, 3):
                    y = _rope_cols(y, tabt_ref, kind)
                o_ref[0, c0 + cc * LANES:c0 + (cc + 1) * LANES, :] = y
        off += width


def _project(x2d, w_rows, w_cols, tabs, tabs_t, tm, row_segs, col_segs, seq):
    n = x2d.shape[0]
    n_tab = tabs.shape[1] // tm
    per_seq = max(seq // tm, 1)
    in_specs = [pl.BlockSpec((tm, D_MODEL), lambda r: (r, 0)),
                pl.BlockSpec(w_rows.shape, lambda r: (0, 0)),
                pl.BlockSpec(w_cols.shape, lambda r: (0, 0)),
                pl.BlockSpec((9, tm, LANES), lambda r: (0, r % n_tab, 0)),
                pl.BlockSpec((9, LANES, tm), lambda r: (0, 0, r % n_tab))]
    out_specs = [pl.BlockSpec((tm, s[1]), lambda r: (r, 0)) for s in row_segs]
    out_specs += [pl.BlockSpec((1, s[1], tm), lambda r: (r // per_seq, 0, r % per_seq)) for s in col_segs]
    out_shape = [jax.ShapeDtypeStruct((n, s[1]), F32) for s in row_segs]
    out_shape += [jax.ShapeDtypeStruct((n // seq, s[1], seq), F32) for s in col_segs]
    outs = pl.pallas_call(
        functools.partial(_proj_kernel, row_segs=row_segs, col_segs=col_segs),
        grid=(n // tm,),
        in_specs=in_specs, out_specs=out_specs, out_shape=out_shape,
        compiler_params=_params(1),
        name="proj",
    )(x2d, w_rows, w_cols, tabs, tabs_t)
    return dict(zip([s[0] for s in row_segs + col_segs], outs))


def _rope_tables(pos):
    pos = pos.astype(F32)
    lane = np.arange(LANES)
    tabs = []
    for dim, first_only in ((HD, False), (HD, True), (DIFF_QK, False)):
        rot = dim // ROPE_FRACTION
        half = rot // 2
        inv = jnp.power(jnp.float32(ROPE_THETA), -jnp.arange(half, dtype=F32) * (2.0 / rot))
        ang = pos[:, None] * inv[None, :]
        cos, sin = jnp.cos(ang), jnp.sin(ang)
        c = lane % dim
        in_rot = c < rot
        if first_only:
            in_rot = in_rot & (lane < HD)
        first = in_rot & (c < half)
        second = in_rot & (c >= half)
        fidx = c % half
        cos_l, sin_l = cos[:, fidx], sin[:, fidx]
        tabs.append(jnp.where(in_rot[None, :], cos_l, 1.0))
        tabs.append(jnp.where(first[None, :], -sin_l, 0.0))
        tabs.append(jnp.where(second[None, :], sin_l, 0.0))
    return jnp.stack(tabs).astype(F32)


def _proj_weights(w_in_l):
    o = [int(v) for v in np.cumsum((0, 256, 256, 256, 256, 256, 128, 12, 256, 256, 256, 256, 256, 256))]
    sl = lambda a, b: w_in_l[:, a:b]
    gate = jnp.pad(sl(o[6], o[7]), ((0, 0), (0, LANES - 12)))
    wq = jnp.concatenate([sl(o[0], o[1]), sl(o[3], o[4]), sl(o[7], o[8]), sl(o[10], o[11]), gate], axis=1)
    wkv = jnp.concatenate([sl(o[1], o[3]), sl(o[4], o[5]), sl(o[5], o[6]), sl(o[8], o[10]), sl(o[11], o[13])],
                          axis=1)
    return wq.astype(BF16), wkv.astype(BF16)


def _qpos(p_lo, tq, rows):
    r = lax.broadcasted_iota(jnp.int32, (rows, 1), 0)
    return p_lo + (r & (tq - 1))


def _kpos(kpos0, tk, rows):
    return kpos0 + lax.broadcasted_iota(jnp.int32, (rows, tk), 1)


def _flash_step(qb, kt, vt, mask, scale, carry):
    m, l, acc = carry
    s = _dot(qb, kt) * scale
    if mask is not None:
        s = jnp.where(mask, s, NEG_BIG)
    m_new = jnp.maximum(m, jnp.max(s, axis=-1, keepdims=True))
    a = jnp.exp(m - m_new)
    p = jnp.exp(s - m_new)
    l = a * l + jnp.sum(p, axis=-1, keepdims=True)
    acc = a * acc + _dot_t(p.astype(BF16), vt)
    return m_new, l, acc


def _flash_init(rows):
    return (jnp.full((rows, 1), NEG_BIG, F32), jnp.zeros((rows, 1), F32), jnp.zeros((rows, GROUP_W), F32))


def _unstack_heads(rows_val, tq, masks):
    out = rows_val[0:tq] * masks[0]
    for h in range(1, HEADS):
        out = out + rows_val[h * tq:(h + 1) * tq] * masks[h]
    return out


def _later_matrix(tk):
    r_i = lax.broadcasted_iota(jnp.int32, (tk, tk), 0)
    c_i = lax.broadcasted_iota(jnp.int32, (tk, tk), 1)
    return jnp.where(r_i > c_i, 1.0, 0.0).astype(BF16)


def _tile_from_ref(ref, idx, j, tk):
    cols = pl.ds(pl.multiple_of(j * tk, tk), tk)
    return (ref[idx + (slice(0, GROUP_W), cols)].astype(BF16),
            ref[idx + (slice(GROUP_W, 2 * GROUP_W), cols)].astype(BF16))


def _page_tile(ref):
    idx = (0,) * (len(ref.shape) - 2)
    return (ref[idx + (slice(0, GROUP_W), slice(None))].astype(BF16),
            ref[idx + (slice(GROUP_W, 2 * GROUP_W), slice(None))].astype(BF16))


def _gather_context(page_refs, new_ref, kt_ref, vt_ref, page):
    for p, ref in enumerate(list(page_refs) + [new_ref]):
        kt, vt = _page_tile(ref)
        kt_ref[:, p * page:(p + 1) * page] = kt
        vt_ref[:, p * page:(p + 1) * page] = vt


def _softmax_context(s, mask, vt):
    s = jnp.where(mask, s, NEG_BIG)
    m = jnp.max(s, axis=-1, keepdims=True)
    p = jnp.exp(s - m)
    return m, jnp.sum(p, axis=-1, keepdims=True), _dot_t(p.astype(BF16), vt)


def _sb_tile(qb_ref, acc_ref, tail_ref, kt, vt, later_mat, kpos0, p_lo, tq, causal, chunk):
    rows, tk = qb_ref.shape[0], kt.shape[1]
    for r0 in range(0, rows, chunk):
        rs = slice(r0, r0 + chunk)
        z = _dot(qb_ref[rs, :], kt)
        log_sig = jnp.minimum(z, 0.0) - jnp.log(1.0 + jnp.exp(-jnp.abs(z)))
        log_fail = log_sig - z
        if causal:
            qpos = p_lo + ((r0 + lax.broadcasted_iota(jnp.int32, (chunk, 1), 0)) & (tq - 1))
            mask = _kpos(kpos0, tk, chunk) < qpos
            log_fail = jnp.where(mask, log_fail, 0.0)
        later = _split_dot(log_fail, later_mat) + tail_ref[rs, :]
        w = jnp.exp(log_sig + later)
        if causal:
            w = jnp.where(mask, w, 0.0)
        acc_ref[rs, :] += _dot_t(w.astype(BF16), vt)
        tail_ref[rs, :] += jnp.sum(log_fail, axis=-1, keepdims=True)


def _sb_kernel(q_ref, kv_ref, o_ref, qb_ref, acc_ref, tail_ref, *, tq, tk, pos0, chunk):
    i = pl.program_id(1)
    p_lo = pos0 + i * tq
    masks = _lane_group_masks(GROUP_W, HD, HEADS)
    qb_ref[...] = _stack_groups(q_ref[0] * (HD ** -0.5), masks).astype(BF16)
    acc_ref[...] = jnp.zeros_like(acc_ref)
    tail_ref[...] = jnp.zeros_like(tail_ref)
    later_mat = _later_matrix(tk)
    j_own = (p_lo + tq - 1) // tk
    kt, vt = _tile_from_ref(kv_ref, (0,), j_own, tk)
    _sb_tile(qb_ref, acc_ref, tail_ref, kt, vt, later_mat, j_own * tk, p_lo, tq, True, chunk)

    def body(jj, _):
        j = j_own - 1 - jj
        kt, vt = _tile_from_ref(kv_ref, (0,), j, tk)
        _sb_tile(qb_ref, acc_ref, tail_ref, kt, vt, later_mat, j * tk, p_lo, tq, False, chunk)
        return 0

    lax.fori_loop(0, j_own, body, 0)
    o_ref[0] = _unstack_heads(acc_ref[...], tq, masks)


def _sb_sample_kernel(pt_ref, q_ref, *refs, n_pages, page, past):
    page_refs, new_ref, o_ref = refs[:n_pages], refs[n_pages], refs[n_pages + 1]
    tq = q_ref.shape[1]
    rows = HEADS * tq
    masks = _lane_group_masks(GROUP_W, HD, HEADS)
    kt_ref, vt_ref = refs[n_pages + 2:]
    _gather_context(page_refs, new_ref, kt_ref, vt_ref, page)
    qb = _stack_groups(q_ref[0] * (HD ** -0.5), masks).astype(BF16)
    qpos = _qpos(past, tq, rows)
    later_mat = _later_matrix(page)
    ctx = kt_ref.shape[1]
    z = _dot(qb, kt_ref[...])
    mask = _kpos(0, ctx, rows) < qpos
    log_sig = jnp.minimum(z, 0.0) - jnp.log(1.0 + jnp.exp(-jnp.abs(z)))
    log_fail = jnp.where(mask, log_sig - z, 0.0)
    tail = jnp.zeros((rows, 1), F32)
    later = [None] * (n_pages + 1)
    for p in reversed(range(n_pages + 1)):
        blk = log_fail[:, p * page:(p + 1) * page]
        later[p] = _split_dot(blk, later_mat) + tail
        tail = tail + jnp.sum(blk, axis=-1, keepdims=True)
    w = jnp.where(mask, jnp.exp(log_sig + jnp.concatenate(later, axis=1)), 0.0)
    o_ref[0] = _unstack_heads(_dot_t(w.astype(BF16), vt_ref[...]), tq, masks)


def _prompt_call(kern, q, kv_t, extra, extra_specs, tq, name, scratch=()):
    b, t, _ = q.shape
    rows_kv, l = kv_t.shape[1], kv_t.shape[2]
    return pl.pallas_call(
        kern,
        grid=(b, t // tq),
        in_specs=[pl.BlockSpec((1, tq, GROUP_W), lambda bb, i: (bb, i, 0)),
                  pl.BlockSpec((1, rows_kv, l), lambda bb, i: (bb, 0, 0))] + list(extra_specs),
        out_specs=pl.BlockSpec((1, tq, GROUP_W), lambda bb, i: (bb, i, 0)),
        out_shape=jax.ShapeDtypeStruct((b, t, GROUP_W), F32),
        scratch_shapes=list(scratch),
        compiler_params=_params(2),
        name=name,
    )(q, kv_t, *extra)


def _sample_call(kern, page_table, q, pool_t, layer, new_t, extra, extra_specs, name, scratch=()):
    b, tq, _ = q.shape
    n_pages = page_table.shape[1]
    rows_kv, page = pool_t.shape[2], pool_t.shape[3]

    def page_spec(p):
        return pl.BlockSpec((1, 1, rows_kv, page), lambda bb, pt: (layer, pt[bb * n_pages + p], 0, 0))

    in_specs = ([pl.BlockSpec((1, tq, GROUP_W), lambda bb, pt: (bb, 0, 0))]
                + [page_spec(p) for p in range(n_pages)]
                + [pl.BlockSpec((1, rows_kv, page), lambda bb, pt: (bb, 0, 0))] + list(extra_specs))
    return pl.pallas_call(
        kern,
        grid_spec=pltpu.PrefetchScalarGridSpec(
            num_scalar_prefetch=1, grid=(b,), in_specs=in_specs,
            out_specs=pl.BlockSpec((1, tq, GROUP_W), lambda bb, pt: (bb, 0, 0)),
            scratch_shapes=list(scratch)),
        out_shape=jax.ShapeDtypeStruct((b, tq, GROUP_W), F32),
        compiler_params=_params(1),
        name=name,
    )(page_table.reshape(-1), q, *([pool_t] * n_pages), new_t, *extra)


def _diff_finish(carry, tq, lam_ref, sub_ref, lam_init):
    _, l, acc = carry
    head_masks = _lane_group_masks(GROUP_W, HD, HEADS)
    o = acc / l
    lp = lam_ref[...]
    lam = (jnp.exp(jnp.sum(lp[0:1] * lp[1:2], axis=-1, keepdims=True))
           - jnp.exp(jnp.sum(lp[2:3] * lp[3:4], axis=-1, keepdims=True)) + lam_init)
    out = jnp.zeros((tq, GROUP_W), F32)
    for h in range(HEADS):
        oh = (o[(2 * h) * tq:(2 * h + 1) * tq] - lam * o[(2 * h + 1) * tq:(2 * h + 2) * tq]) * head_masks[h]
        ms = jnp.sum(oh * oh, axis=-1, keepdims=True) * (1.0 / HD)
        out = out + oh * lax.rsqrt(ms + RMS_EPS)
    return out * sub_ref[...] * (1.0 - lam_init)


def _diff_kernel(q_ref, kv_ref, lam_ref, sub_ref, o_ref, *, tq, tk, pos0, lam_init):
    i = pl.program_id(1)
    p_lo = pos0 + i * tq
    rows = 2 * HEADS * tq
    qb = _stack_groups(q_ref[0], _lane_group_masks(GROUP_W, DIFF_QK, 2 * HEADS)).astype(BF16)
    qpos = _qpos(p_lo, tq, rows)
    j_own = (p_lo + tq - 1) // tk
    kt, vt = _tile_from_ref(kv_ref, (0,), j_own, tk)
    carry = _flash_step(qb, kt, vt, _kpos(j_own * tk, tk, rows) <= qpos, DIFF_QK ** -0.5, _flash_init(rows))

    def body(j, carry):
        kt, vt = _tile_from_ref(kv_ref, (0,), j, tk)
        return _flash_step(qb, kt, vt, None, DIFF_QK ** -0.5, carry)

    carry = lax.fori_loop(0, j_own, body, carry)
    o_ref[0] = _diff_finish(carry, tq, lam_ref, sub_ref, lam_init)


def _diff_sample_kernel(pt_ref, q_ref, *refs, n_pages, page, past, lam_init):
    page_refs, new_ref = refs[:n_pages], refs[n_pages]
    lam_ref, sub_ref, o_ref = refs[n_pages + 1:n_pages + 4]
    tq = q_ref.shape[1]
    rows = 2 * HEADS * tq
    qb = _stack_groups(q_ref[0], _lane_group_masks(GROUP_W, DIFF_QK, 2 * HEADS)).astype(BF16)
    qpos = _qpos(past, tq, rows)
    kt_ref, vt_ref = refs[n_pages + 4:]
    _gather_context(page_refs, new_ref, kt_ref, vt_ref, page)
    s = _dot(qb, kt_ref[...]) * (DIFF_QK ** -0.5)
    mask = _kpos(0, kt_ref.shape[1], rows) <= qpos
    carry = _softmax_context(s, mask, vt_ref[...])
    o_ref[0] = _diff_finish(carry, tq, lam_ref, sub_ref, lam_init)


def _moba_select(qb, km_t, own, n_blocks):
    gate = _dot(qb, km_t.astype(BF16))
    rows = qb.shape[0]
    lane = lax.broadcasted_iota(jnp.int32, (rows, LANES), 1)
    rank = jnp.zeros((rows, LANES), F32)
    for m in range(n_blocks):
        gm = gate[:, m:m + 1]
        beats = (gm > gate) | ((gm == gate) & (m < lane))
        rank = rank + jnp.where(beats & (m < own), 1.0, 0.0)
    return jnp.where((rank < MOBA_TOPK) & (lane < own), 1.0, 0.0)


def _block_mean_col(kt_f32, m):
    lane = lax.broadcasted_iota(jnp.int32, (GROUP_W, LANES), 1)
    return jnp.where(lane == m, jnp.sum(kt_f32, axis=-1, keepdims=True), 0.0)


def _moba_kernel(q_ref, kv_ref, o_ref, km_ref, *, tq, tk, pos0, n_blocks):
    i = pl.program_id(1)
    p_lo = pos0 + i * tq
    rows = HEADS * tq
    masks = _lane_group_masks(GROUP_W, HD, HEADS)

    @pl.when(i == 0)
    def _():
        km = jnp.zeros((GROUP_W, LANES), F32)
        for m in range(n_blocks):
            km = km + _block_mean_col(kv_ref[0, 0:GROUP_W, m * MOBA_BLOCK:(m + 1) * MOBA_BLOCK], m)
        km_ref[...] = km * (1.0 / MOBA_BLOCK)

    qb = _stack_groups(q_ref[0], masks).astype(BF16)
    qpos = _qpos(p_lo, tq, rows)
    own = p_lo // MOBA_BLOCK
    sel = _moba_select(qb, km_ref[...], own, n_blocks)

    kt, vt = _tile_from_ref(kv_ref, (0,), own, tk)
    carry = _flash_step(qb, kt, vt, _kpos(own * tk, tk, rows) <= qpos, HD ** -0.5, _flash_init(rows))

    def body(j, carry):
        kt, vt = _tile_from_ref(kv_ref, (0,), j, tk)
        return _flash_step(qb, kt, vt, _col(sel, j) > 0.5, HD ** -0.5, carry)

    _, l, acc = lax.fori_loop(0, own, body, carry)
    o_ref[0] = _unstack_heads(acc / l, tq, masks)


def _moba_sample_kernel(pt_ref, q_ref, *refs, n_pages, page, past):
    page_refs, new_ref, o_ref = refs[:n_pages], refs[n_pages], refs[n_pages + 1]
    tq = q_ref.shape[1]
    rows = HEADS * tq
    per_blk = MOBA_BLOCK // page
    own = past // MOBA_BLOCK
    masks = _lane_group_masks(GROUP_W, HD, HEADS)
    km = jnp.zeros((GROUP_W, LANES), F32)
    for p in range(n_pages):
        km = km + _block_mean_col(page_refs[p][0, 0, 0:GROUP_W, :], p // per_blk)
    km = km * (1.0 / MOBA_BLOCK)
    qb = _stack_groups(q_ref[0], masks).astype(BF16)
    qpos = _qpos(past, tq, rows)
    sel = _moba_select(qb, km, own, own)
    kt_ref, vt_ref = refs[n_pages + 2:]
    _gather_context(page_refs, new_ref, kt_ref, vt_ref, page)
    s = _dot(qb, kt_ref[...]) * (HD ** -0.5)
    ctx = kt_ref.shape[1]
    picked = [jnp.broadcast_to(sel[:, m:m + 1], (rows, MOBA_BLOCK)) for m in range(own)]
    picked.append(jnp.where(_kpos(past, ctx - past, rows) <= qpos, 1.0, 0.0))
    _, l, acc = _softmax_context(s, jnp.concatenate(picked, axis=1) > 0.5, vt_ref[...])
    o_ref[0] = _unstack_heads(acc / l, tq, masks)


def _gelu_tanh(x):
    return 0.5 * x * (1.0 + jnp.tanh(math.sqrt(2.0 / math.pi) * (x + 0.044715 * (x * x * x))))


def _nsa_compress(rows_ref, w1_ref, b1_ref, pe_ref, w2k_ref, w2v_ref, ck_ref, cv_ref, n_cmp):
    row = lax.broadcasted_iota(jnp.int32, (n_cmp, 1), 0)
    hid = jnp.zeros((n_cmp, 2 * NSA_CMP_HID), F32)
    for l in range(NSA_STRIDE):
        x = rows_ref[pl.ds(l, n_cmp, stride=NSA_STRIDE), :]
        xs = jnp.where(row == n_cmp - 1, 0.0, pltpu.roll(x, n_cmp - 1, 0))
        hid = hid + _dot((x + pe_ref[l:l + 1, :]).astype(BF16), w1_ref[l])
        hid = hid + _dot((xs + pe_ref[l + NSA_STRIDE:l + NSA_STRIDE + 1, :]).astype(BF16),
                         w1_ref[l + NSA_STRIDE])
    hid = _gelu_tanh(hid + b1_ref[...]).astype(BF16)
    ck_ref[...] = jnp.zeros_like(ck_ref)
    cv_ref[...] = jnp.zeros_like(cv_ref)
    ck_ref[0:n_cmp, :] = _dot(hid, w2k_ref[...])
    cv_ref[0:n_cmp, :] = _dot(hid, w2v_ref[...])


def _nsa_cmp_and_select(qb, qpos, ck_ref, cv_ref, tq, n_sel):
    rows = qb.shape[0]
    ncp = ck_ref.shape[0]
    s = _dot_t(qb, ck_ref[...].astype(BF16)) * (HD ** -0.5)
    c_end = lax.broadcasted_iota(jnp.int32, (rows, ncp), 1) * NSA_STRIDE + (NSA_CMP_LEN - 1)
    c_mask = c_end <= qpos
    s = jnp.where(c_mask, s, NEG_BIG)
    p = jnp.where(c_mask, jnp.exp(s - jnp.max(s, axis=-1, keepdims=True)), 0.0)
    den = jnp.sum(p, axis=-1, keepdims=True)
    p = p / jnp.where(den > 0, den, 1.0)
    o_cmp = _dot(p.astype(BF16), cv_ref[...].astype(BF16))
    p_sum = p[0:tq]
    for h in range(1, HEADS):
        p_sum = p_sum + p[h * tq:(h + 1) * tq]
    gi = lax.broadcasted_iota(jnp.int32, (ncp, LANES), 0)
    gj = lax.broadcasted_iota(jnp.int32, (ncp, LANES), 1)
    group_mat = jnp.where((gi // (NSA_SEL_BLOCK // NSA_STRIDE)) == gj, 1.0, 0.0).astype(BF16)
    imp = _split_dot(p_sum, group_mat)
    qblk = qpos[0:tq] // NSA_SEL_BLOCK
    lane = lax.broadcasted_iota(jnp.int32, (tq, LANES), 1)
    valid = (lane <= qblk) & (lane < n_sel)
    forced = (lane == 0) | (lane == qblk) | (lane == qblk - 1)
    score = jnp.where(valid, jnp.where(forced, NSA_FORCE, imp), -jnp.inf)
    rank = jnp.zeros((tq, LANES), F32)
    for m in range(n_sel):
        sm = score[:, m:m + 1]
        beats = (sm > score) | ((sm == score) & (m < lane))
        rank = rank + jnp.where(beats, 1.0, 0.0)
    sel = jnp.where((rank < min(NSA_TOPN, n_sel)) & valid, 1.0, 0.0)
    return o_cmp, sel


def _shared_tile(k64, v64):
    return (jnp.concatenate([k64] * HEADS, axis=0).astype(BF16),
            jnp.concatenate([v64] * HEADS, axis=0).astype(BF16))


def _slc_mask(sel_bf, blk0, kpos0, tk, qpos, rows, causal):
    ei = lax.broadcasted_iota(jnp.int32, (LANES, tk), 0)
    ej = lax.broadcasted_iota(jnp.int32, (LANES, tk), 1) // NSA_SEL_BLOCK
    expand = jnp.where(ei == ej + blk0, 1.0, 0.0).astype(BF16)
    picked = _dot(sel_bf, expand)
    picked = jnp.concatenate([picked] * HEADS, axis=0) > 0.5
    return picked & (_kpos(kpos0, tk, rows) <= qpos) if causal else picked


def _win_mask(kpos0, tk, qpos, rows):
    kp = _kpos(kpos0, tk, rows)
    return (kp <= qpos) & (kp > qpos - NSA_WINDOW)


def _nsa_combine(g, o_cmp, slc, win, tq, masks):
    o_slc = slc[2] / slc[1]
    o_win = win[2] / win[1]
    out = jnp.zeros((tq, GROUP_W), F32)
    for h in range(HEADS):
        r0, r1 = h * tq, (h + 1) * tq
        mix = (g[:, 3 * h:3 * h + 1] * o_cmp[r0:r1] + g[:, 3 * h + 1:3 * h + 2] * o_slc[r0:r1]
               + g[:, 3 * h + 2:3 * h + 3] * o_win[r0:r1])
        out = out + mix * masks[h]
    return out


def _nsa_kernel(q_ref, nkv_ref, g_ref, wkv_ref, w1_ref, b1_ref, pe_ref, w2k_ref, w2v_ref,
                o_ref, rows_ref, ck_ref, cv_ref, *, tq, tk, pos0, n_cmp, n_sel):
    i = pl.program_id(1)
    p_lo = pos0 + i * tq
    rows = HEADS * tq
    masks = _lane_group_masks(GROUP_W, HD, HEADS)

    @pl.when(i == 0)
    def _():
        for c in range(nkv_ref.shape[2] // LANES):
            rows_ref[c * LANES:(c + 1) * LANES, :] = nkv_ref[0, 0:LANES, c * LANES:(c + 1) * LANES].T
        _nsa_compress(rows_ref, w1_ref, b1_ref, pe_ref, w2k_ref, w2v_ref, ck_ref, cv_ref, n_cmp)

    qb = _stack_groups(q_ref[0], masks).astype(BF16)
    qpos = _qpos(p_lo, tq, rows)
    scale = HD ** -0.5
    o_cmp, sel = _nsa_cmp_and_select(qb, qpos, ck_ref, cv_ref, tq, n_sel)
    sel_bf = sel.astype(BF16)
    j_own = (p_lo + tq - 1) // tk

    def slc_step(j, carry, causal):
        cols = pl.ds(pl.multiple_of(j * tk, tk), tk)
        kt, vt = _shared_tile(nkv_ref[0, 2 * HD:3 * HD, cols], nkv_ref[0, 3 * HD:4 * HD, cols])
        mask = _slc_mask(sel_bf, j * (tk // NSA_SEL_BLOCK), j * tk, tk, qpos, rows, causal)
        return _flash_step(qb, kt, vt, mask, scale, carry)

    slc = slc_step(j_own, _flash_init(rows), True)
    slc = lax.fori_loop(0, j_own, lambda j, c: slc_step(j, c, False), slc)

    def win_step(j, carry, mask_fn):
        cols = pl.ds(pl.multiple_of(j * tk, tk), tk)
        kt, vt = _shared_tile(wkv_ref[0, 0:HD, cols], wkv_ref[0, HD:2 * HD, cols])
        return _flash_step(qb, kt, vt, mask_fn(_kpos(j * tk, tk, rows)), scale, carry)

    win = win_step(j_own, _flash_init(rows), lambda kp: kp <= qpos)
    j_lo = jnp.maximum(p_lo - (NSA_WINDOW - 1), 0) // tk
    win = lax.fori_loop(j_lo, j_own, lambda j, c: win_step(j, c, lambda kp: kp > qpos - NSA_WINDOW), win)
    o_ref[0] = _nsa_combine(g_ref[0], o_cmp, slc, win, tq, masks)


def _nsa_sample_kernel(pt_ref, q_ref, *refs, n_pages, page, past, n_cmp, n_sel, w_eff):
    page_refs, new_ref = refs[:n_pages], refs[n_pages]
    (g_ref, wst_ref, wnew_ref, w1_ref, b1_ref, pe_ref, w2k_ref, w2v_ref,
     o_ref, rows_ref, ck_ref, cv_ref, sk_ref, sv_ref, wk_ref, wv_ref) = refs[n_pages + 1:]
    tq = q_ref.shape[1]
    rows = HEADS * tq
    masks = _lane_group_masks(GROUP_W, HD, HEADS)
    rows_ref[...] = jnp.zeros_like(rows_ref)
    for p in range(n_pages):
        rows_ref[p * page:(p + 1) * page, :] = page_refs[p][0, 0, 0:LANES, :].T
    rows_ref[n_pages * page:(n_pages + 1) * page, :] = new_ref[0, 0:LANES, :].T
    _nsa_compress(rows_ref, w1_ref, b1_ref, pe_ref, w2k_ref, w2v_ref, ck_ref, cv_ref, n_cmp)

    qb = _stack_groups(q_ref[0], masks).astype(BF16)
    qpos = _qpos(past, tq, rows)
    scale = HD ** -0.5
    o_cmp, sel = _nsa_cmp_and_select(qb, qpos, ck_ref, cv_ref, tq, n_sel)
    q = q_ref[0]
    qs = jnp.concatenate([q[:, h * HD:(h + 1) * HD] for h in range(HEADS)], axis=0).astype(BF16)
    ctx = sk_ref.shape[1]
    for p in range(n_pages + 1):
        src = (page_refs[p], (0, 0)) if p < n_pages else (new_ref, (0,))
        sk_ref[:, p * page:(p + 1) * page] = src[0][src[1] + (slice(2 * HD, 3 * HD), slice(None))].astype(BF16)
        sv_ref[:, p * page:(p + 1) * page] = src[0][src[1] + (slice(3 * HD, 4 * HD), slice(None))].astype(BF16)
    ei = lax.broadcasted_iota(jnp.int32, (LANES, ctx), 0)
    ej = lax.broadcasted_iota(jnp.int32, (LANES, ctx), 1) // NSA_SEL_BLOCK
    picked = _dot(sel.astype(BF16), jnp.where(ei == ej, 1.0, 0.0).astype(BF16))
    mask = (jnp.concatenate([picked] * HEADS, axis=0) > 0.5) & (_kpos(0, ctx, rows) <= qpos)
    slc = _softmax_context(_dot(qs, sk_ref[...]) * scale, mask, sv_ref[...])

    wk_ref[:, 0:w_eff] = wst_ref[0, 0, 0:HD, :].astype(BF16)
    wv_ref[:, 0:w_eff] = wst_ref[0, 0, HD:2 * HD, :].astype(BF16)
    wk_ref[:, w_eff:] = wnew_ref[0, 0:HD, :].astype(BF16)
    wv_ref[:, w_eff:] = wnew_ref[0, HD:2 * HD, :].astype(BF16)
    wmask = _win_mask(past - w_eff, wk_ref.shape[1], qpos, rows)
    win = _softmax_context(_dot(qs, wk_ref[...]) * scale, wmask, wv_ref[...])
    widen = lambda c: (c[0], c[1], jnp.concatenate([c[2]] * HEADS, axis=1))
    o_ref[0] = _nsa_combine(g_ref[0], o_cmp, widen(slc), widen(win), tq, masks)


def _nsa_cmp_weights(pe, w1, b1, w2):
    z = jnp.zeros((NSA_CMP_LEN, HD, NSA_CMP_HID), F32)
    w1cat = jnp.concatenate([jnp.concatenate([w1[0], z], axis=2), jnp.concatenate([z, w1[1]], axis=2)], axis=1)
    b1cat = jnp.concatenate([b1[0], b1[1]])[None, :]
    pecat = jnp.concatenate([pe[0], pe[1]], axis=1)
    z2 = jnp.zeros((NSA_CMP_HID, GROUP_W), F32)
    w2k4 = jnp.concatenate([jnp.tile(w2[0], (1, HEADS)), z2], axis=0)
    w2v4 = jnp.concatenate([z2, jnp.tile(w2[1], (1, HEADS))], axis=0)
    return w1cat.astype(BF16), b1cat, pecat, w2k4.astype(BF16), w2v4.astype(BF16)


def _cmp_specs(index_map):
    full = lambda *shape: pl.BlockSpec(shape, lambda *a: (0,) * len(shape))
    return [full(NSA_CMP_LEN, LANES, 2 * NSA_CMP_HID), full(1, 2 * NSA_CMP_HID),
            full(NSA_CMP_LEN, LANES), full(2 * NSA_CMP_HID, GROUP_W), full(2 * NSA_CMP_HID, GROUP_W)]


def _layer_norm(x, g, b):
    xc = x - jnp.mean(x, axis=-1, keepdims=True)
    var = jnp.mean(xc * xc, axis=-1, keepdims=True)
    return xc * lax.rsqrt(var + LN_EPS) * g + b


def _outproj_kernel(x_ref, a_ref, b_ref, c_ref, d_ref, w_ref, g_ref, be_ref, o_ref):
    mix = _dot(a_ref[...].astype(BF16), w_ref[0:256, :])
    mix = mix + _dot(b_ref[...].astype(BF16), w_ref[256:512, :])
    mix = mix + _dot(c_ref[...].astype(BF16), w_ref[512:768, :])
    mix = mix + _dot(d_ref[...].astype(BF16), w_ref[768:1024, :])
    o_ref[...] = _layer_norm(ALPHA * x_ref[...] + mix, g_ref[...], be_ref[...])


def _outproj_norm(x2d, outs, w_out_bf, gain, bias, tm):
    n = x2d.shape[0]
    row = lambda w: pl.BlockSpec((tm, w), lambda r: (r, 0))
    return pl.pallas_call(
        _outproj_kernel,
        grid=(n // tm,),
        in_specs=[row(D_MODEL), row(GROUP_W), row(GROUP_W), row(GROUP_W), row(GROUP_W),
                  pl.BlockSpec((D_MODEL, D_MODEL), lambda r: (0, 0)),
                  pl.BlockSpec((1, D_MODEL), lambda r: (0, 0)), pl.BlockSpec((1, D_MODEL), lambda r: (0, 0))],
        out_specs=row(D_MODEL),
        out_shape=jax.ShapeDtypeStruct((n, D_MODEL), F32),
        compiler_params=_params(1),
        name="outproj_ln",
    )(x2d, *outs, w_out_bf, gain, bias)


def _moe_kernel(h_ref, wr_ref, rb_ref, wgu_ref, wd_ref, g_ref, be_ref, o_ref, hb_ref, comb_ref, acc_ref):
    e = pl.program_id(1)

    @pl.when(e == 0)
    def _():
        hb = h_ref[...].astype(BF16)
        hb_ref[...] = hb
        logit = _dot(hb, wr_ref[...]) + rb_ref[...]
        lane = lax.broadcasted_iota(jnp.int32, logit.shape, 1)
        is_g = lane < N_GROUPS
        lg = jnp.where(is_g, logit, NEG_BIG)
        gmax = jnp.max(lg, axis=-1, keepdims=True)
        gsel = jnp.min(jnp.where(is_g & (lg == gmax), lane, LANES), axis=-1, keepdims=True)
        pg = 1.0 / jnp.sum(jnp.where(is_g, jnp.exp(lg - gmax), 0.0), axis=-1, keepdims=True)
        e0 = N_GROUPS + gsel * EXPERTS_PER_GROUP
        in_g = (lane >= e0) & (lane < e0 + EXPERTS_PER_GROUP)
        le = jnp.where(in_g, logit, NEG_BIG)
        v1 = jnp.max(le, axis=-1, keepdims=True)
        i1 = jnp.min(jnp.where(in_g & (le == v1), lane, LANES), axis=-1, keepdims=True)
        rest = in_g & (lane != i1)
        le2 = jnp.where(rest, logit, NEG_BIG)
        v2 = jnp.max(le2, axis=-1, keepdims=True)
        i2 = jnp.min(jnp.where(rest & (le2 == v2), lane, LANES), axis=-1, keepdims=True)
        ex = jnp.exp(v2 - v1)
        w1 = pg / (1.0 + ex)
        w2 = pg * ex / (1.0 + ex)
        comb_ref[...] = jnp.where(lane == i1, w1, 0.0) + jnp.where(lane == i2, w2, 0.0)
        acc_ref[...] = jnp.zeros_like(acc_ref)

    gu = _dot(hb_ref[...], wgu_ref[0])
    gt, up = gu[:, :EXPERT_HID], gu[:, EXPERT_HID:]
    hid = gt / (1.0 + jnp.exp(-gt)) * up
    hid = hid * _col(comb_ref[...], N_GROUPS + e)
    acc_ref[...] += _dot(hid.astype(BF16), wd_ref[0])

    @pl.when(e == N_EXPERTS - 1)
    def _():
        o_ref[...] = _layer_norm(ALPHA * h_ref[...] + acc_ref[...], g_ref[...], be_ref[...])


def _moe_norm(h2d, wr, rb, wgu, wd, gain, bias, tm):
    n = h2d.shape[0]
    assert n % tm == 0
    return pl.pallas_call(
        _moe_kernel,
        grid=(n // tm, N_EXPERTS),
        in_specs=[pl.BlockSpec((tm, D_MODEL), lambda r, e: (r, 0)),
                  pl.BlockSpec((D_MODEL, LANES), lambda r, e: (0, 0)),
                  pl.BlockSpec((1, LANES), lambda r, e: (0, 0)),
                  pl.BlockSpec((1, D_MODEL, 2 * EXPERT_HID), lambda r, e: (e, 0, 0)),
                  pl.BlockSpec((1, EXPERT_HID, D_MODEL), lambda r, e: (e, 0, 0)),
                  pl.BlockSpec((1, D_MODEL), lambda r, e: (0, 0)),
                  pl.BlockSpec((1, D_MODEL), lambda r, e: (0, 0))],
        out_specs=pl.BlockSpec((tm, D_MODEL), lambda r, e: (r, 0)),
        out_shape=jax.ShapeDtypeStruct((n, D_MODEL), F32),
        scratch_shapes=[pltpu.VMEM((tm, D_MODEL), BF16), pltpu.VMEM((tm, LANES), F32),
                        pltpu.VMEM((tm, D_MODEL), F32)],
        compiler_params=_params(2),
        name="moe_ln",
    )(h2d, wr, rb, wgu, wd, gain, bias)


def _layer_weights(l, w_in, w_out, pe, w1, b1, w2, lam_p, subln, ln_g, ln_b, rgw, rgb, rew, reb, wg, wu, wd):
    wr = jnp.concatenate([rgw[l], rew[l], jnp.zeros((D_MODEL, LANES - N_GROUPS - N_EXPERTS), F32)], axis=1)
    rb = jnp.concatenate([rgb[l], reb[l], jnp.zeros((LANES - N_GROUPS - N_EXPERTS,), F32)])[None, :]
    wq, wkv = _proj_weights(w_in[l])
    return dict(
        wq=wq, wkv=wkv, wkv_t=wkv.T, wall=jnp.concatenate([wq, wkv], axis=1),
        w_out=w_out[l].astype(BF16),
        cmp=_nsa_cmp_weights(pe[l], w1[l], b1[l], w2[l]),
        lam_p=lam_p[l],
        subln4=jnp.tile(subln[l], HEADS)[None, :],
        lam_init=0.8 - 0.6 * math.exp(-0.3 * l),
        g1=ln_g[l, 0][None, :], b1=ln_b[l, 0][None, :], g2=ln_g[l, 1][None, :], b2=ln_b[l, 1][None, :],
        wr=wr.astype(BF16), rb=rb,
        wgu=jnp.concatenate([wg[l], wu[l]], axis=2).astype(BF16),
        wd=wd[l].astype(BF16),
    )


def _ffn(x, outs, lw, tm, tm_ffn):
    b, t, _ = x.shape
    x2d = x.reshape(b * t, D_MODEL)
    h = _outproj_norm(x2d, [o.reshape(b * t, GROUP_W) for o in outs], lw["w_out"], lw["g1"], lw["b1"], tm)
    y = _moe_norm(h, lw["wr"], lw["rb"], lw["wgu"], lw["wd"], lw["g2"], lw["b2"], tm_ffn)
    return y.reshape(b, t, D_MODEL)


def _new_tile(rows, b, t, page):
    w = rows.shape[-1]
    return jnp.pad(jnp.transpose(rows.reshape(b, t, w), (0, 2, 1)), ((0, 0), (0, 0), (0, page - t)))


def kernel(x_prompt, x_sample, cache_sb_kv, cache_nsa_kv, cache_diff_kv, cache_moba_kv, state_nsa_win, page_table, w_in, w_out, nsa_cmp_pe, nsa_cmp_w1, nsa_cmp_b1, nsa_cmp_w2, diff_lambda, diff_subln, ln_gain, ln_bias, router_group_w, router_group_b, router_expert_w, router_expert_b, expert_w_gate, expert_w_up, expert_w_down):
    bp, tp, _ = x_prompt.shape
    bs, ts, _ = x_sample.shape
    depth = w_in.shape[0]
    n_phys, page = cache_sb_kv.shape[1], cache_sb_kv.shape[2]
    n_pages = page_table.shape[1]
    past = n_pages * page
    w_eff = state_nsa_win.shape[2]
    l_s = past + ts
    l_s_pad = -(-l_s // MOBA_BLOCK) * MOBA_BLOCK
    tm_p = 256
    tm_s = min(256, bs * ts)
    assert tp % KV_TILE == 0 and tm_p % ts == 0 and (bs * ts) % tm_s == 0 and ts <= page
    assert page == LANES and past % MOBA_BLOCK == 0 and w_eff % LANES == 0 and w_eff == min(NSA_WINDOW, past)

    pool_sb = jnp.transpose(cache_sb_kv, (0, 1, 3, 4, 5, 2)).reshape(depth, n_phys, 2 * GROUP_W, page)
    pool_diff = jnp.transpose(cache_diff_kv, (0, 1, 3, 4, 5, 2)).reshape(depth, n_phys, 2 * GROUP_W, page)
    pool_moba = jnp.transpose(cache_moba_kv, (0, 1, 3, 4, 5, 2)).reshape(depth, n_phys, 2 * GROUP_W, page)
    pool_nsa = jnp.transpose(cache_nsa_kv, (0, 1, 3, 4, 2)).reshape(depth, n_phys, 4 * HD, page)
    win_state = jnp.transpose(state_nsa_win, (0, 1, 3, 4, 2)).reshape(depth, bs, 2 * HD, w_eff)

    tabs_p = _rope_tables(jnp.arange(tp, dtype=jnp.int32))
    tabs_s = _rope_tables(past + (jnp.arange(tm_s, dtype=jnp.int32) % ts))
    tabs_p_t, tabs_s_t = jnp.transpose(tabs_p, (0, 2, 1)), jnp.transpose(tabs_s, (0, 2, 1))
    n_cmp_p, n_sel_p = tp // NSA_STRIDE, tp // NSA_SEL_BLOCK
    n_cmp_s, n_sel_s = l_s_pad // NSA_STRIDE, l_s_pad // NSA_SEL_BLOCK
    ncp = lambda n: -(-n // LANES) * LANES

    y_p, y_s = x_prompt, x_sample
    new_p, new_s = [], []
    for l in range(depth):
        lw = _layer_weights(l, w_in, w_out, nsa_cmp_pe, nsa_cmp_w1, nsa_cmp_b1, nsa_cmp_w2, diff_lambda,
                            diff_subln, ln_gain, ln_bias, router_group_w, router_group_b, router_expert_w,
                            router_expert_b, expert_w_gate, expert_w_up, expert_w_down)
        lam_specs = [pl.BlockSpec((4, DIFF_QK), lambda *a: (0, 0)), pl.BlockSpec((1, GROUP_W), lambda *a: (0, 0))]

        pp = _project(y_p.reshape(bp * tp, D_MODEL), lw["wq"], lw["wkv_t"], tabs_p, tabs_p_t, tm_p,
                      Q_SEGS, KV_SEGS, tp)
        q3 = lambda a: a.reshape(bp, tp, a.shape[-1])
        o_sb = _prompt_call(functools.partial(_sb_kernel, tq=256, tk=KV_TILE, pos0=0, chunk=HEADS * 256),
                            q3(pp["sbq"]), pp["sbkv"], (), (), 256, "sb_attn",
                            scratch=[pltpu.VMEM((HEADS * 256, GROUP_W), BF16),
                                     pltpu.VMEM((HEADS * 256, GROUP_W), F32),
                                     pltpu.VMEM((HEADS * 256, 1), F32)])
        o_nsa = _prompt_call(
            functools.partial(_nsa_kernel, tq=256, tk=KV_TILE, pos0=0, n_cmp=n_cmp_p, n_sel=n_sel_p),
            q3(pp["nq"]), pp["nkv"], (q3(pp["gate"]), pp["wkv"]) + lw["cmp"],
            [pl.BlockSpec((1, 256, LANES), lambda bb, i: (bb, i, 0)),
             pl.BlockSpec((1, 2 * HD, tp), lambda bb, i: (bb, 0, 0))] + _cmp_specs(None), 256, "nsa_attn",
            scratch=[pltpu.VMEM((tp, LANES), F32), pltpu.VMEM((ncp(n_cmp_p), GROUP_W), F32),
                     pltpu.VMEM((ncp(n_cmp_p), GROUP_W), F32)])
        o_diff = _prompt_call(
            functools.partial(_diff_kernel, tq=128, tk=KV_TILE, pos0=0, lam_init=lw["lam_init"]),
            q3(pp["dq"]), pp["dkv"], (lw["lam_p"], lw["subln4"]), lam_specs, 128, "diff_attn")
        o_moba = _prompt_call(
            functools.partial(_moba_kernel, tq=256, tk=KV_TILE, pos0=0, n_blocks=tp // MOBA_BLOCK),
            q3(pp["mq"]), pp["mkv"], (), (), 256, "moba_attn", scratch=[pltpu.VMEM((GROUP_W, LANES), F32)])
        y_p = _ffn(y_p, (o_sb, o_nsa, o_diff, o_moba), lw, tm_p, math.gcd(1024, bp * tp))
        new_p.append((pp["sbkv"], pp["nkv"], pp["dkv"], pp["mkv"], pp["wkv"][:, :, tp - min(NSA_WINDOW, tp):]))

        ps = _project(y_s.reshape(bs * ts, D_MODEL), lw["wall"], lw["wkv_t"], tabs_s, tabs_s_t, tm_s,
                      Q_SEGS + KV_SEGS, (), ts)
        s3 = lambda a: a.reshape(bs, ts, a.shape[-1])
        new_tile = lambda name: _new_tile(ps[name], bs, ts, page)
        common = dict(n_pages=n_pages, page=page, past=past)
        ctx_scratch = lambda ch: [pltpu.VMEM((ch, past + page), BF16), pltpu.VMEM((ch, past + page), BF16)]
        o_sb = _sample_call(functools.partial(_sb_sample_kernel, **common), page_table, s3(ps["sbq"]),
                            pool_sb, l, new_tile("sbkv"), (), (), "sb_attn_s", scratch=ctx_scratch(GROUP_W))
        o_nsa = _sample_call(
            functools.partial(_nsa_sample_kernel, n_cmp=n_cmp_s, n_sel=n_sel_s, w_eff=w_eff, **common),
            page_table, s3(ps["nq"]), pool_nsa, l, new_tile("nkv"),
            (s3(ps["gate"]), win_state, new_tile("wkv")) + lw["cmp"],
            [pl.BlockSpec((1, ts, LANES), lambda bb, pt: (bb, 0, 0)),
             pl.BlockSpec((1, 1, 2 * HD, w_eff), lambda bb, pt: (l, bb, 0, 0)),
             pl.BlockSpec((1, 2 * HD, page), lambda bb, pt: (bb, 0, 0))] + _cmp_specs(None), "nsa_attn_s",
            scratch=[pltpu.VMEM((l_s_pad, LANES), F32), pltpu.VMEM((ncp(n_cmp_s), GROUP_W), F32),
                     pltpu.VMEM((ncp(n_cmp_s), GROUP_W), F32)] + ctx_scratch(HD)
            + [pltpu.VMEM((HD, w_eff + page), BF16), pltpu.VMEM((HD, w_eff + page), BF16)])
        o_diff = _sample_call(functools.partial(_diff_sample_kernel, lam_init=lw["lam_init"], **common),
                              page_table, s3(ps["dq"]), pool_diff, l, new_tile("dkv"),
                              (lw["lam_p"], lw["subln4"]), lam_specs, "diff_attn_s",
                              scratch=ctx_scratch(GROUP_W))
        o_moba = _sample_call(functools.partial(_moba_sample_kernel, **common), page_table, s3(ps["mq"]),
                              pool_moba, l, new_tile("mkv"), (), (), "moba_attn_s",
                              scratch=ctx_scratch(GROUP_W))
        y_s = _ffn(y_s, (o_sb, o_nsa, o_diff, o_moba), lw, tm_s, math.gcd(1024, bs * ts))
        win_new = jnp.concatenate([win_state[l][:, :, ts:], new_tile("wkv")[:, :, :ts]], axis=2)
        new_s.append((ps["sbkv"], ps["nkv"], ps["dkv"], ps["mkv"], win_new))

    def heads_out(a, b, t):
        return jnp.transpose(a.reshape(depth, b, 2, HEADS, HD, t), (0, 1, 5, 2, 3, 4))

    def slots_out(a, b, t, n):
        return jnp.transpose(a.reshape(depth, b, n, HD, t), (0, 1, 4, 2, 3))

    st_p = [jnp.stack([r[k] for r in new_p]) for k in range(5)]
    st_s = [jnp.stack([r[k] for r in new_s]) for k in range(5)]
    outs_p = (heads_out(st_p[0], bp, tp), slots_out(st_p[1], bp, tp, 4), heads_out(st_p[2], bp, tp),
              heads_out(st_p[3], bp, tp), slots_out(st_p[4], bp, min(NSA_WINDOW, tp), 2))
    outs_s = (st_s[0].reshape(depth, bs, ts, 2, HEADS, HD), st_s[1].reshape(depth, bs, ts, 4, HD),
              st_s[2].reshape(depth, bs, ts, 2, HEADS, HD), st_s[3].reshape(depth, bs, ts, 2, HEADS, HD),
              slots_out(st_s[4], bs, w_eff, 2))
    return (y_p, y_s, *outs_p, *outs_s)
```

```python
import functools
import math

import numpy as np
import jax
import jax.numpy as jnp
from jax import lax
from jax.experimental import pallas as pl
from jax.experimental.pallas import tpu as pltpu

F32 = jnp.float32
BF16 = jnp.bfloat16

D_MODEL = 1024
HEADS = 4
HD = 64
GROUP_W = HEADS * HD
DIFF_QK = HD // 2
DEPTH_LAYERS = 2
ROPE_THETA = 500000.0
ROPE_FRACTION = 4
NSA_STRIDE = 16
NSA_CMP_LEN = 32
NSA_CMP_HID = 128
NSA_SEL_BLOCK = 64
NSA_TOPN = 16
NSA_WINDOW = 512
NSA_FORCE = 1.0e4
MOBA_BLOCK = 256
MOBA_TOPK = 3
N_GROUPS = 4
EXPERTS_PER_GROUP = 4
N_EXPERTS = 16
EXPERT_HID = 256
EXPERTS_PER_STEP = 2
ALPHA = (2 * DEPTH_LAYERS) ** 0.25
LN_EPS = 1e-5
RMS_EPS = 1e-5
NEG_BIG = -1e30
LANES = 128
KV_TILE = 256
SAMPLES_PER_STEP = 2
VMEM_LIMIT = 56 * 1024 * 1024

Q_SEGS = (("sbq", 256, (0, 0)), ("nq", 256, (1, 1)), ("dq", 256, (3, 3)), ("mq", 256, (1, 1)),
          ("gate", 128, (4,)))
KV_SEGS = (("sbkv", 512, (0, 0, 0, 0)), ("nkv", 256, (2, 2)), ("wkv", 128, (2,)),
           ("dkv", 512, (3, 3, 0, 0)), ("mkv", 512, (1, 1, 0, 0)))
Q_COLS = sum(s[1] for s in Q_SEGS)
KV_COLS = sum(s[1] for s in KV_SEGS)


def _dot(a, b):
    return jnp.dot(a, b, preferred_element_type=F32)


def _dot_t(a, b):
    return lax.dot_general(a, b, (((1,), (1,)), ((), ())), preferred_element_type=F32)


def _split_dot(a, b_bf):
    hi = a.astype(BF16)
    lo = (a - hi.astype(F32)).astype(BF16)
    return _dot(hi, b_bf) + _dot(lo, b_bf)


def _lane_group_masks(width, group, n):
    lane = lax.broadcasted_iota(jnp.int32, (1, width), 1)
    return [jnp.where((lane // group) == g, 1.0, 0.0).astype(F32) for g in range(n)]


def _stack_groups(q, masks):
    return jnp.concatenate([q * m for m in masks], axis=0)


def _col(x, idx):
    lane = lax.broadcasted_iota(jnp.int32, x.shape, 1)
    return jnp.sum(jnp.where(lane == idx, x, 0.0), axis=-1, keepdims=True)


def _params(n_axes):
    return pltpu.CompilerParams(dimension_semantics=("arbitrary",) * n_axes, vmem_limit_bytes=VMEM_LIMIT)


def _rope_rows(y, tab_ref, kind):
    half = 4 if kind == 3 else 8
    t0 = (kind - 1) * 3
    return (y * tab_ref[t0] + pltpu.roll(y, LANES - half, 1) * tab_ref[t0 + 1]
            + pltpu.roll(y, half, 1) * tab_ref[t0 + 2])


def _rope_cols(y, tab_ref, kind):
    half = 4 if kind == 3 else 8
    t0 = (kind - 1) * 3
    return (y * tab_ref[t0] + pltpu.roll(y, LANES - half, 0) * tab_ref[t0 + 1]
            + pltpu.roll(y, half, 0) * tab_ref[t0 + 2])


def _proj_kernel(x_ref, w_ref, wt_ref, tab_ref, tabt_ref, *out_refs, row_segs, col_segs, n_carried):
    out_refs = out_refs[n_carried:]
    xb = x_ref[...].astype(BF16)
    off = 0
    n_row = len(row_segs)
    for (_, width, kinds), o_ref in zip(row_segs, out_refs[:n_row]):
        for c0 in range(0, width, 256):
            cw = min(256, width - c0)
            y2 = _dot(xb, w_ref[:, off + c0:off + c0 + cw])
            for cc in range(cw // LANES):
                y = y2[:, cc * LANES:(cc + 1) * LANES]
                kind = kinds[(c0 // LANES) + cc]
                if kind in (1, 2, 3):
                    y = _rope_rows(y, tab_ref, kind)
                elif kind == 4:
                    y = 1.0 / (1.0 + jnp.exp(-y))
                o_ref[:, c0 + cc * LANES:c0 + (cc + 1) * LANES] = y
        off += width
    off = 0
    for (_, width, kinds), o_ref in zip(col_segs, out_refs[n_row:]):
        for c0 in range(0, width, 256):
            cw = min(256, width - c0)
            y2 = _dot_t(wt_ref[off + c0:off + c0 + cw, :], xb)
            for cc in range(cw // LANES):
                y = y2[cc * LANES:(cc + 1) * LANES, :]
                kind = kinds[(c0 // LANES) + cc]
                if kind in (1, 2, 3):
                    y = _rope_cols(y, tabt_ref, kind)
                o_ref[0, c0 + cc * LANES:c0 + (cc + 1) * LANES, :] = y
        off += width


def _project(x2d, w_rows, w_cols, tabs, tabs_t, tm, row_segs, col_segs, seq, depth=1, layer=0, carried=None):
    n = x2d.shape[0]
    carried = list(carried or [])
    n_tab = tabs.shape[1] // tm
    per_seq = max(seq // tm, 1)
    in_specs = [pl.BlockSpec((tm, D_MODEL), lambda r: (r, 0)),
                pl.BlockSpec(w_rows.shape, lambda r: (0, 0)),
                pl.BlockSpec(w_cols.shape, lambda r: (0, 0)),
                pl.BlockSpec((9, tm, LANES), lambda r: (0, r % n_tab, 0)),
                pl.BlockSpec((9, LANES, tm), lambda r: (0, 0, r % n_tab))]
    out_specs = [pl.BlockSpec((tm, s[1]), lambda r: (r, 0)) for s in row_segs]
    in_specs += [pl.BlockSpec(memory_space=pl.ANY)] * len(carried)
    out_specs += [pl.BlockSpec((None, 1, s[1], tm), lambda r: (layer, r // per_seq, 0, r % per_seq))
                  for s in col_segs]
    out_shape = [jax.ShapeDtypeStruct((n, s[1]), F32) for s in row_segs]
    out_shape += [jax.ShapeDtypeStruct((depth, n // seq, s[1], seq), F32) for s in col_segs]
    outs = pl.pallas_call(
        functools.partial(_proj_kernel, row_segs=row_segs, col_segs=col_segs, n_carried=len(carried)),
        grid=(n // tm,),
        in_specs=in_specs, out_specs=out_specs, out_shape=out_shape,
        input_output_aliases={5 + k: len(row_segs) + k for k in range(len(carried))},
        compiler_params=_params(1),
        name="proj",
    )(x2d, w_rows, w_cols, tabs, tabs_t, *carried)
    return dict(zip([s[0] for s in row_segs + col_segs], outs))


def _rope_tables(pos):
    pos = pos.astype(F32)
    lane = np.arange(LANES)
    tabs = []
    for dim, first_only in ((HD, False), (HD, True), (DIFF_QK, False)):
        rot = dim // ROPE_FRACTION
        half = rot // 2
        inv = jnp.power(jnp.float32(ROPE_THETA), -jnp.arange(half, dtype=F32) * (2.0 / rot))
        ang = pos[:, None] * inv[None, :]
        cos, sin = jnp.cos(ang), jnp.sin(ang)
        c = lane % dim
        in_rot = c < rot
        if first_only:
            in_rot = in_rot & (lane < HD)
        first = in_rot & (c < half)
        second = in_rot & (c >= half)
        fidx = c % half
        cos_l, sin_l = cos[:, fidx], sin[:, fidx]
        tabs.append(jnp.where(in_rot[None, :], cos_l, 1.0))
        tabs.append(jnp.where(first[None, :], -sin_l, 0.0))
        tabs.append(jnp.where(second[None, :], sin_l, 0.0))
    return jnp.stack(tabs).astype(F32)


def _proj_weights(w_in_l):
    o = [int(v) for v in np.cumsum((0, 256, 256, 256, 256, 256, 128, 12, 256, 256, 256, 256, 256, 256))]
    sl = lambda a, b: w_in_l[:, a:b]
    gate = jnp.pad(sl(o[6], o[7]), ((0, 0), (0, LANES - 12)))
    wq = jnp.concatenate([sl(o[0], o[1]), sl(o[3], o[4]), sl(o[7], o[8]), sl(o[10], o[11]), gate], axis=1)
    wkv = jnp.concatenate([sl(o[1], o[3]), sl(o[4], o[5]), sl(o[5], o[6]), sl(o[8], o[10]), sl(o[11], o[13])],
                          axis=1)
    return wq.astype(BF16), wkv.astype(BF16)


def _qpos(p_lo, tq, rows):
    r = lax.broadcasted_iota(jnp.int32, (rows, 1), 0)
    return p_lo + (r & (tq - 1))


def _kpos(kpos0, tk, rows):
    return kpos0 + lax.broadcasted_iota(jnp.int32, (rows, tk), 1)


def _flash_step(qb, kt, vt, mask, scale, carry):
    m, l, acc = carry
    s = _dot(qb, kt) * scale
    if mask is not None:
        s = jnp.where(mask, s, NEG_BIG)
    m_new = jnp.maximum(m, jnp.max(s, axis=-1, keepdims=True))
    a = jnp.exp(m - m_new)
    p = jnp.exp(s - m_new)
    l = a * l + jnp.sum(p, axis=-1, keepdims=True)
    acc = a * acc + _dot_t(p.astype(BF16), vt)
    return m_new, l, acc


def _flash_init(rows):
    return (jnp.full((rows, 1), NEG_BIG, F32), jnp.zeros((rows, 1), F32), jnp.zeros((rows, GROUP_W), F32))


def _unstack_heads(rows_val, tq, masks):
    out = rows_val[0:tq] * masks[0]
    for h in range(1, HEADS):
        out = out + rows_val[h * tq:(h + 1) * tq] * masks[h]
    return out


def _later_matrix(tk):
    r_i = lax.broadcasted_iota(jnp.int32, (tk, tk), 0)
    c_i = lax.broadcasted_iota(jnp.int32, (tk, tk), 1)
    return jnp.where(r_i > c_i, 1.0, 0.0).astype(BF16)


def _tile_from_ref(ref, idx, j, tk):
    cols = pl.ds(pl.multiple_of(j * tk, tk), tk)
    return (ref[idx + (slice(0, GROUP_W), cols)].astype(BF16),
            ref[idx + (slice(GROUP_W, 2 * GROUP_W), cols)].astype(BF16))


def _page_tile(ref):
    idx = (0,) * (len(ref.shape) - 2)
    return (ref[idx + (slice(0, GROUP_W), slice(None))].astype(BF16),
            ref[idx + (slice(GROUP_W, 2 * GROUP_W), slice(None))].astype(BF16))


def _gather_context(page_refs, new_ref, kt_ref, vt_ref, page):
    for p, ref in enumerate(list(page_refs) + [new_ref]):
        kt, vt = _page_tile(ref)
        kt_ref[:, p * page:(p + 1) * page] = kt
        vt_ref[:, p * page:(p + 1) * page] = vt


def _softmax_context(s, mask, vt):
    s = jnp.where(mask, s, NEG_BIG)
    m = jnp.max(s, axis=-1, keepdims=True)
    p = jnp.exp(s - m)
    return m, jnp.sum(p, axis=-1, keepdims=True), _dot_t(p.astype(BF16), vt)


def _sb_tile(qb_ref, acc_ref, tail_ref, kt, vt, later_mat, kpos0, p_lo, tq, causal, chunk):
    rows, tk = qb_ref.shape[0], kt.shape[1]
    for r0 in range(0, rows, chunk):
        rs = slice(r0, r0 + chunk)
        z = _dot(qb_ref[rs, :], kt)
        log_sig = jnp.minimum(z, 0.0) - jnp.log(1.0 + jnp.exp(-jnp.abs(z)))
        log_fail = log_sig - z
        if causal:
            qpos = p_lo + ((r0 + lax.broadcasted_iota(jnp.int32, (chunk, 1), 0)) & (tq - 1))
            mask = _kpos(kpos0, tk, chunk) < qpos
            log_fail = jnp.where(mask, log_fail, 0.0)
        later = _split_dot(log_fail, later_mat) + tail_ref[rs, :]
        w = jnp.exp(log_sig + later)
        if causal:
            w = jnp.where(mask, w, 0.0)
        acc_ref[rs, :] += _dot_t(w.astype(BF16), vt)
        tail_ref[rs, :] += jnp.sum(log_fail, axis=-1, keepdims=True)


def _sb_kernel(q_ref, kv_ref, o_ref, qb_ref, acc_ref, tail_ref, *, tq, tk, pos0, chunk):
    i = pl.program_id(1)
    p_lo = pos0 + i * tq
    masks = _lane_group_masks(GROUP_W, HD, HEADS)
    qb_ref[...] = _stack_groups(q_ref[0] * (HD ** -0.5), masks).astype(BF16)
    acc_ref[...] = jnp.zeros_like(acc_ref)
    tail_ref[...] = jnp.zeros_like(tail_ref)
    later_mat = _later_matrix(tk)
    j_own = (p_lo + tq - 1) // tk
    kt, vt = _tile_from_ref(kv_ref, (0,), j_own, tk)
    _sb_tile(qb_ref, acc_ref, tail_ref, kt, vt, later_mat, j_own * tk, p_lo, tq, True, chunk)

    def body(jj, _):
        j = j_own - 1 - jj
        kt, vt = _tile_from_ref(kv_ref, (0,), j, tk)
        _sb_tile(qb_ref, acc_ref, tail_ref, kt, vt, later_mat, j * tk, p_lo, tq, False, chunk)
        return 0

    lax.fori_loop(0, j_own, body, 0)
    o_ref[0] = _unstack_heads(acc_ref[...], tq, masks)


def _sb_sample_kernel(pt_ref, q_ref, *refs, n_pages, page, past):
    page_refs, new_ref, o_ref = refs[:n_pages], refs[n_pages], refs[n_pages + 1]
    tq = q_ref.shape[1]
    rows = HEADS * tq
    masks = _lane_group_masks(GROUP_W, HD, HEADS)
    kt_ref, vt_ref = refs[n_pages + 2:]
    _gather_context(page_refs, new_ref, kt_ref, vt_ref, page)
    qb = _stack_groups(q_ref[0] * (HD ** -0.5), masks).astype(BF16)
    qpos = _qpos(past, tq, rows)
    later_mat = _later_matrix(page)
    ctx = kt_ref.shape[1]
    z = _dot(qb, kt_ref[...])
    mask = _kpos(0, ctx, rows) < qpos
    log_sig = jnp.minimum(z, 0.0) - jnp.log(1.0 + jnp.exp(-jnp.abs(z)))
    log_fail = jnp.where(mask, log_sig - z, 0.0)
    tail = jnp.zeros((rows, 1), F32)
    later = [None] * (n_pages + 1)
    for p in reversed(range(n_pages + 1)):
        blk = log_fail[:, p * page:(p + 1) * page]
        later[p] = _split_dot(blk, later_mat) + tail
        tail = tail + jnp.sum(blk, axis=-1, keepdims=True)
    w = jnp.where(mask, jnp.exp(log_sig + jnp.concatenate(later, axis=1)), 0.0)
    o_ref[0] = _unstack_heads(_dot_t(w.astype(BF16), vt_ref[...]), tq, masks)


def _prompt_call(kern, q, kv_t, layer, extra, extra_specs, tq, name, scratch=()):
    b, t, _ = q.shape
    rows_kv, l = kv_t.shape[2], kv_t.shape[3]
    return pl.pallas_call(
        kern,
        grid=(b, t // tq),
        in_specs=[pl.BlockSpec((1, tq, GROUP_W), lambda bb, i: (bb, i, 0)),
                  pl.BlockSpec((None, 1, rows_kv, l), lambda bb, i: (layer, bb, 0, 0))] + list(extra_specs),
        out_specs=pl.BlockSpec((1, tq, GROUP_W), lambda bb, i: (bb, i, 0)),
        out_shape=jax.ShapeDtypeStruct((b, t, GROUP_W), F32),
        scratch_shapes=list(scratch),
        compiler_params=_params(2),
        name=name,
    )(q, kv_t, *extra)


def _sample_call(body, page_table, q, pool_t, layer, new_t, extra, extra_specs, extra_axes, name, scratch=()):
    b, tq, _ = q.shape
    spb = SAMPLES_PER_STEP
    n_pages = page_table.shape[1]
    rows_kv, page = pool_t.shape[2], pool_t.shape[3]
    assert b % spb == 0

    def page_spec(s_, p):
        return pl.BlockSpec((1, 1, rows_kv, page), lambda bb, pt: (layer, pt[(bb * spb + s_) * n_pages + p], 0, 0))

    axes = [0] + [None] * (spb * n_pages) + [0] + list(extra_axes) + [0]
    n_scratch = len(scratch)

    def kern(pt_ref, *refs):
        io, scr = refs[:len(refs) - n_scratch], refs[len(refs) - n_scratch:]
        for s_ in range(spb):
            views = []
            for k, (r, ax) in enumerate(zip(io, axes)):
                if 1 <= k <= spb * n_pages:
                    if (k - 1) // n_pages == s_:
                        views.append(r)
                elif ax is None:
                    views.append(r)
                else:
                    views.append(r.at[(slice(None),) * ax + (pl.ds(s_, 1),)])
            body(pt_ref, *views, *[r.at[s_] for r in scr])

    in_specs = ([pl.BlockSpec((spb, tq, GROUP_W), lambda bb, pt: (bb, 0, 0))]
                + [page_spec(s_, p) for s_ in range(spb) for p in range(n_pages)]
                + [pl.BlockSpec((spb, rows_kv, page), lambda bb, pt: (bb, 0, 0))] + list(extra_specs))
    return pl.pallas_call(
        kern,
        grid_spec=pltpu.PrefetchScalarGridSpec(
            num_scalar_prefetch=1, grid=(b // spb,), in_specs=in_specs,
            out_specs=pl.BlockSpec((spb, tq, GROUP_W), lambda bb, pt: (bb, 0, 0)),
            scratch_shapes=[pltpu.VMEM((spb,) + shape, dt) for shape, dt in scratch]),
        out_shape=jax.ShapeDtypeStruct((b, tq, GROUP_W), F32),
        compiler_params=_params(1),
        name=name,
    )(page_table.reshape(-1), q, *([pool_t] * (spb * n_pages)), new_t, *extra)


def _diff_finish(carry, tq, lam_ref, sub_ref, lam_init):
    _, l, acc = carry
    head_masks = _lane_group_masks(GROUP_W, HD, HEADS)
    o = acc / l
    lp = lam_ref[...]
    lam = (jnp.exp(jnp.sum(lp[0:1] * lp[1:2], axis=-1, keepdims=True))
           - jnp.exp(jnp.sum(lp[2:3] * lp[3:4], axis=-1, keepdims=True)) + lam_init)
    out = jnp.zeros((tq, GROUP_W), F32)
    for h in range(HEADS):
        oh = (o[(2 * h) * tq:(2 * h + 1) * tq] - lam * o[(2 * h + 1) * tq:(2 * h + 2) * tq]) * head_masks[h]
        ms = jnp.sum(oh * oh, axis=-1, keepdims=True) * (1.0 / HD)
        out = out + oh * lax.rsqrt(ms + RMS_EPS)
    return out * sub_ref[...] * (1.0 - lam_init)


def _diff_kernel(q_ref, kv_ref, lam_ref, sub_ref, o_ref, *, tq, tk, pos0, lam_init):
    i = pl.program_id(1)
    p_lo = pos0 + i * tq
    rows = 2 * HEADS * tq
    qb = _stack_groups(q_ref[0], _lane_group_masks(GROUP_W, DIFF_QK, 2 * HEADS)).astype(BF16)
    qpos = _qpos(p_lo, tq, rows)
    j_own = (p_lo + tq - 1) // tk
    kt, vt = _tile_from_ref(kv_ref, (0,), j_own, tk)
    carry = _flash_step(qb, kt, vt, _kpos(j_own * tk, tk, rows) <= qpos, DIFF_QK ** -0.5, _flash_init(rows))

    def body(j, carry):
        kt, vt = _tile_from_ref(kv_ref, (0,), j, tk)
        return _flash_step(qb, kt, vt, None, DIFF_QK ** -0.5, carry)

    carry = lax.fori_loop(0, j_own, body, carry)
    o_ref[0] = _diff_finish(carry, tq, lam_ref, sub_ref, lam_init)


def _diff_sample_kernel(pt_ref, q_ref, *refs, n_pages, page, past, lam_init):
    page_refs, new_ref = refs[:n_pages], refs[n_pages]
    lam_ref, sub_ref, o_ref = refs[n_pages + 1:n_pages + 4]
    tq = q_ref.shape[1]
    rows = 2 * HEADS * tq
    qb = _stack_groups(q_ref[0], _lane_group_masks(GROUP_W, DIFF_QK, 2 * HEADS)).astype(BF16)
    qpos = _qpos(past, tq, rows)
    kt_ref, vt_ref = refs[n_pages + 4:]
    _gather_context(page_refs, new_ref, kt_ref, vt_ref, page)
    s = _dot(qb, kt_ref[...]) * (DIFF_QK ** -0.5)
    mask = _kpos(0, kt_ref.shape[1], rows) <= qpos
    carry = _softmax_context(s, mask, vt_ref[...])
    o_ref[0] = _diff_finish(carry, tq, lam_ref, sub_ref, lam_init)


def _topk_mask_t(score_t, k):
    row = lax.broadcasted_iota(jnp.int32, score_t.shape, 0)
    rank = jnp.zeros(score_t.shape, F32)
    for m in range(score_t.shape[0]):
        sm = score_t[m:m + 1, :]
        tie = jnp.where(row > m, 1.0, 0.0)
        rank = rank + jnp.where(sm > score_t, 1.0, jnp.where(sm == score_t, tie, 0.0))
    return jnp.where(rank < k, 1.0, 0.0)


def _rows_from_t(mask_t):
    n, cols = mask_t.shape
    return jnp.concatenate([mask_t, jnp.zeros((LANES - n, cols), F32)], axis=0).T


def _moba_select(qb, km_t, own, n_blocks):
    gate = _dot(qb, km_t.astype(BF16))
    rows = qb.shape[0]
    lane = lax.broadcasted_iota(jnp.int32, (rows, LANES), 1)
    rank = jnp.zeros((rows, LANES), F32)
    for m in range(n_blocks):
        gm = gate[:, m:m + 1]
        beats = (gm > gate) | ((gm == gate) & (m < lane))
        rank = rank + jnp.where(beats & (m < own), 1.0, 0.0)
    return jnp.where((rank < MOBA_TOPK) & (lane < own), 1.0, 0.0)


def _block_mean_col(kt_f32, m):
    lane = lax.broadcasted_iota(jnp.int32, (GROUP_W, LANES), 1)
    return jnp.where(lane == m, jnp.sum(kt_f32, axis=-1, keepdims=True), 0.0)


def _moba_select_t(qb, km_rows, own):
    gate_t = _dot_t(km_rows.astype(BF16), qb)
    past = lax.broadcasted_iota(jnp.int32, gate_t.shape, 0) < own
    pick_t = _topk_mask_t(jnp.where(past, gate_t, -jnp.inf), MOBA_TOPK)
    return _rows_from_t(jnp.where(past, pick_t, 0.0))


def _moba_kernel(q_ref, kv_ref, o_ref, km_ref, *, tq, tk, pos0, n_blocks):
    i = pl.program_id(1)
    p_lo = pos0 + i * tq
    rows = HEADS * tq
    masks = _lane_group_masks(GROUP_W, HD, HEADS)

    @pl.when(i == 0)
    def _():
        km = jnp.zeros((GROUP_W, LANES), F32)
        for m in range(n_blocks):
            km = km + _block_mean_col(kv_ref[0, 0:GROUP_W, m * MOBA_BLOCK:(m + 1) * MOBA_BLOCK], m)
        km_ref[...] = (km * (1.0 / MOBA_BLOCK)).T

    qb = _stack_groups(q_ref[0], masks).astype(BF16)
    qpos = _qpos(p_lo, tq, rows)
    own = p_lo // MOBA_BLOCK
    sel = _moba_select_t(qb, km_ref[0:16, :], own)

    kt, vt = _tile_from_ref(kv_ref, (0,), own, tk)
    carry = _flash_step(qb, kt, vt, _kpos(own * tk, tk, rows) <= qpos, HD ** -0.5, _flash_init(rows))

    def body(j, carry):
        kt, vt = _tile_from_ref(kv_ref, (0,), j, tk)
        return _flash_step(qb, kt, vt, _col(sel, j) > 0.5, HD ** -0.5, carry)

    _, l, acc = lax.fori_loop(0, own, body, carry)
    o_ref[0] = _unstack_heads(acc / l, tq, masks)


def _moba_sample_kernel(pt_ref, q_ref, *refs, n_pages, page, past):
    page_refs, new_ref, o_ref = refs[:n_pages], refs[n_pages], refs[n_pages + 1]
    tq = q_ref.shape[1]
    rows = HEADS * tq
    per_blk = MOBA_BLOCK // page
    own = past // MOBA_BLOCK
    masks = _lane_group_masks(GROUP_W, HD, HEADS)
    km = jnp.zeros((GROUP_W, LANES), F32)
    for p in range(n_pages):
        km = km + _block_mean_col(page_refs[p][0, 0, 0:GROUP_W, :], p // per_blk)
    km = km * (1.0 / MOBA_BLOCK)
    qb = _stack_groups(q_ref[0], masks).astype(BF16)
    qpos = _qpos(past, tq, rows)
    sel = _moba_select(qb, km, own, own)
    kt_ref, vt_ref = refs[n_pages + 2:]
    _gather_context(page_refs, new_ref, kt_ref, vt_ref, page)
    s = _dot(qb, kt_ref[...]) * (HD ** -0.5)
    ctx = kt_ref.shape[1]
    picked = [jnp.broadcast_to(sel[:, m:m + 1], (rows, MOBA_BLOCK)) for m in range(own)]
    picked.append(jnp.where(_kpos(past, ctx - past, rows) <= qpos, 1.0, 0.0))
    _, l, acc = _softmax_context(s, jnp.concatenate(picked, axis=1) > 0.5, vt_ref[...])
    o_ref[0] = _unstack_heads(acc / l, tq, masks)


def _gelu_tanh(x):
    return 0.5 * x * (1.0 + jnp.tanh(math.sqrt(2.0 / math.pi) * (x + 0.044715 * (x * x * x))))


def _phase_rows(tile_t):
    r = lax.broadcasted_iota(jnp.int32, (LANES, LANES), 0)
    c = lax.broadcasted_iota(jnp.int32, (LANES, LANES), 1)
    perm = jnp.where(c == NSA_STRIDE * (r % 8) + r // 8, 1.0, 0.0).astype(BF16)
    hi = tile_t.astype(BF16)
    lo = (tile_t - hi.astype(F32)).astype(BF16)
    return _dot_t(perm, hi) + _dot_t(perm, lo)


def _nsa_compress(rows_ref, w1_ref, b1_ref, pe_ref, w2k_ref, w2v_ref, ck_ref, cv_ref, n_cmp):
    row = lax.broadcasted_iota(jnp.int32, (n_cmp, 1), 0)
    first, second = [], []
    for l in range(NSA_STRIDE):
        x = jnp.concatenate([rows_ref[t * LANES + l * 8:t * LANES + l * 8 + 8, :] for t in range(n_cmp // 8)],
                            axis=0)
        xs = jnp.where(row == n_cmp - 1, 0.0, pltpu.roll(x, n_cmp - 1, 0))
        first.append((x + pe_ref[l:l + 1, :]).astype(BF16))
        second.append((xs + pe_ref[l + NSA_STRIDE:l + NSA_STRIDE + 1, :]).astype(BF16))
    hid = _dot(jnp.concatenate(first + second, axis=1), w1_ref[...].reshape(NSA_CMP_LEN * LANES, 2 * NSA_CMP_HID))
    hid = _gelu_tanh(hid + b1_ref[...]).astype(BF16)
    ck_ref[...] = jnp.zeros_like(ck_ref)
    cv_ref[...] = jnp.zeros_like(cv_ref)
    ck_ref[0:n_cmp, :] = _dot(hid, w2k_ref[...])
    cv_ref[0:n_cmp, :] = _dot(hid, w2v_ref[...])


def _nsa_select_t(p_sum, p_first, n_sel):
    tq, ncp = p_sum.shape
    n_rows = -(-n_sel // 8) * 8
    gi = lax.broadcasted_iota(jnp.int32, (n_rows, ncp), 0)
    gj = lax.broadcasted_iota(jnp.int32, (n_rows, ncp), 1) // (NSA_SEL_BLOCK // NSA_STRIDE)
    group_t = jnp.where(gi == gj, 1.0, 0.0).astype(BF16)
    hi = p_sum.astype(BF16)
    lo = (p_sum - hi.astype(F32)).astype(BF16)
    imp_t = _dot_t(group_t, hi) + _dot_t(group_t, lo)
    row = lax.broadcasted_iota(jnp.int32, (n_rows, tq), 0)
    qblk = (p_first + lax.broadcasted_iota(jnp.int32, (n_rows, tq), 1)) // NSA_SEL_BLOCK
    valid = (row <= qblk) & (row < n_sel)
    forced = (row == 0) | (row == qblk) | (row == qblk - 1)
    score_t = jnp.where(valid, jnp.where(forced, NSA_FORCE, imp_t), -jnp.inf)
    pick_t = _topk_mask_t(score_t, min(NSA_TOPN, n_sel))
    return _rows_from_t(jnp.where(valid, pick_t, 0.0))


def _nsa_cmp_and_select(qb, qpos, ck_ref, cv_ref, tq, n_sel):
    rows = qb.shape[0]
    ncp = ck_ref.shape[0]
    s = _dot_t(qb, ck_ref[...].astype(BF16)) * (HD ** -0.5)
    c_end = lax.broadcasted_iota(jnp.int32, (rows, ncp), 1) * NSA_STRIDE + (NSA_CMP_LEN - 1)
    c_mask = c_end <= qpos
    s = jnp.where(c_mask, s, NEG_BIG)
    p = jnp.where(c_mask, jnp.exp(s - jnp.max(s, axis=-1, keepdims=True)), 0.0)
    den = jnp.sum(p, axis=-1, keepdims=True)
    p = p / jnp.where(den > 0, den, 1.0)
    o_cmp = _dot(p.astype(BF16), cv_ref[...].astype(BF16))
    p_sum = p[0:tq]
    for h in range(1, HEADS):
        p_sum = p_sum + p[h * tq:(h + 1) * tq]
    if tq % LANES == 0:
        return o_cmp, _nsa_select_t(p_sum, qpos[0:1, :], n_sel)
    gi = lax.broadcasted_iota(jnp.int32, (ncp, LANES), 0)
    gj = lax.broadcasted_iota(jnp.int32, (ncp, LANES), 1)
    group_mat = jnp.where((gi // (NSA_SEL_BLOCK // NSA_STRIDE)) == gj, 1.0, 0.0).astype(BF16)
    imp = _split_dot(p_sum, group_mat)
    qblk = qpos[0:tq] // NSA_SEL_BLOCK
    lane = lax.broadcasted_iota(jnp.int32, (tq, LANES), 1)
    valid = (lane <= qblk) & (lane < n_sel)
    forced = (lane == 0) | (lane == qblk) | (lane == qblk - 1)
    score = jnp.where(valid, jnp.where(forced, NSA_FORCE, imp), -jnp.inf)
    rank = jnp.zeros((tq, LANES), F32)
    for m in range(n_sel):
        sm = score[:, m:m + 1]
        beats = (sm > score) | ((sm == score) & (m < lane))
        rank = rank + jnp.where(beats, 1.0, 0.0)
    sel = jnp.where((rank < min(NSA_TOPN, n_sel)) & valid, 1.0, 0.0)
    return o_cmp, sel


def _shared_tile(k64, v64):
    return (jnp.concatenate([k64] * HEADS, axis=0).astype(BF16),
            jnp.concatenate([v64] * HEADS, axis=0).astype(BF16))


def _slc_mask(sel_bf, blk0, kpos0, tk, qpos, rows, causal):
    ei = lax.broadcasted_iota(jnp.int32, (LANES, tk), 0)
    ej = lax.broadcasted_iota(jnp.int32, (LANES, tk), 1) // NSA_SEL_BLOCK
    expand = jnp.where(ei == ej + blk0, 1.0, 0.0).astype(BF16)
    picked = _dot(sel_bf, expand)
    picked = jnp.concatenate([picked] * HEADS, axis=0) > 0.5
    return picked & (_kpos(kpos0, tk, rows) <= qpos) if causal else picked


def _win_mask(kpos0, tk, qpos, rows):
    kp = _kpos(kpos0, tk, rows)
    return (kp <= qpos) & (kp > qpos - NSA_WINDOW)


def _nsa_combine(g, o_cmp, slc, win, tq, masks):
    o_slc = slc[2] / slc[1]
    o_win = win[2] / win[1]
    out = jnp.zeros((tq, GROUP_W), F32)
    for h in range(HEADS):
        r0, r1 = h * tq, (h + 1) * tq
        mix = (g[:, 3 * h:3 * h + 1] * o_cmp[r0:r1] + g[:, 3 * h + 1:3 * h + 2] * o_slc[r0:r1]
               + g[:, 3 * h + 2:3 * h + 3] * o_win[r0:r1])
        out = out + mix * masks[h]
    return out


def _nsa_kernel(q_ref, nkv_ref, g_ref, wkv_ref, w1_ref, b1_ref, pe_ref, w2k_ref, w2v_ref,
                o_ref, rows_ref, ck_ref, cv_ref, *, tq, tk, pos0, n_cmp, n_sel):
    i = pl.program_id(1)
    p_lo = pos0 + i * tq
    rows = HEADS * tq
    masks = _lane_group_masks(GROUP_W, HD, HEADS)

    @pl.when(i == 0)
    def _():
        for c in range(nkv_ref.shape[2] // LANES):
            rows_ref[c * LANES:(c + 1) * LANES, :] = _phase_rows(nkv_ref[0, 0:LANES, c * LANES:(c + 1) * LANES])
        _nsa_compress(rows_ref, w1_ref, b1_ref, pe_ref, w2k_ref, w2v_ref, ck_ref, cv_ref, n_cmp)

    qb = _stack_groups(q_ref[0], masks).astype(BF16)
    qpos = _qpos(p_lo, tq, rows)
    scale = HD ** -0.5
    o_cmp, sel = _nsa_cmp_and_select(qb, qpos, ck_ref, cv_ref, tq, n_sel)
    sel_bf = sel.astype(BF16)
    j_own = (p_lo + tq - 1) // tk

    def slc_step(j, carry, causal):
        cols = pl.ds(pl.multiple_of(j * tk, tk), tk)
        kt, vt = _shared_tile(nkv_ref[0, 2 * HD:3 * HD, cols], nkv_ref[0, 3 * HD:4 * HD, cols])
        mask = _slc_mask(sel_bf, j * (tk // NSA_SEL_BLOCK), j * tk, tk, qpos, rows, causal)
        return _flash_step(qb, kt, vt, mask, scale, carry)

    slc = slc_step(j_own, _flash_init(rows), True)
    slc = lax.fori_loop(0, j_own, lambda j, c: slc_step(j, c, False), slc)

    def win_step(j, carry, mask_fn):
        cols = pl.ds(pl.multiple_of(j * tk, tk), tk)
        kt, vt = _shared_tile(wkv_ref[0, 0:HD, cols], wkv_ref[0, HD:2 * HD, cols])
        return _flash_step(qb, kt, vt, mask_fn(_kpos(j * tk, tk, rows)), scale, carry)

    win = win_step(j_own, _flash_init(rows), lambda kp: kp <= qpos)
    j_lo = jnp.maximum(p_lo - (NSA_WINDOW - 1), 0) // tk
    win = lax.fori_loop(j_lo, j_own, lambda j, c: win_step(j, c, lambda kp: kp > qpos - NSA_WINDOW), win)
    o_ref[0] = _nsa_combine(g_ref[0], o_cmp, slc, win, tq, masks)


def _nsa_sample_kernel(pt_ref, q_ref, *refs, n_pages, page, past, n_cmp, n_sel, w_eff):
    page_refs, new_ref = refs[:n_pages], refs[n_pages]
    (g_ref, wst_ref, wnew_ref, w1_ref, b1_ref, pe_ref, w2k_ref, w2v_ref,
     o_ref, rows_ref, ck_ref, cv_ref, sk_ref, sv_ref, wk_ref, wv_ref) = refs[n_pages + 1:]
    tq = q_ref.shape[1]
    rows = HEADS * tq
    masks = _lane_group_masks(GROUP_W, HD, HEADS)
    rows_ref[...] = jnp.zeros_like(rows_ref)
    for p in range(n_pages):
        rows_ref[p * page:(p + 1) * page, :] = _phase_rows(page_refs[p][0, 0, 0:LANES, :])
    rows_ref[n_pages * page:(n_pages + 1) * page, :] = _phase_rows(new_ref[0, 0:LANES, :])
    _nsa_compress(rows_ref, w1_ref, b1_ref, pe_ref, w2k_ref, w2v_ref, ck_ref, cv_ref, n_cmp)

    qb = _stack_groups(q_ref[0], masks).astype(BF16)
    qpos = _qpos(past, tq, rows)
    scale = HD ** -0.5
    o_cmp, sel = _nsa_cmp_and_select(qb, qpos, ck_ref, cv_ref, tq, n_sel)
    q = q_ref[0]
    qs = jnp.concatenate([q[:, h * HD:(h + 1) * HD] for h in range(HEADS)], axis=0).astype(BF16)
    ctx = sk_ref.shape[1]
    for p in range(n_pages + 1):
        src = (page_refs[p], (0, 0)) if p < n_pages else (new_ref, (0,))
        sk_ref[:, p * page:(p + 1) * page] = src[0][src[1] + (slice(2 * HD, 3 * HD), slice(None))].astype(BF16)
        sv_ref[:, p * page:(p + 1) * page] = src[0][src[1] + (slice(3 * HD, 4 * HD), slice(None))].astype(BF16)
    ei = lax.broadcasted_iota(jnp.int32, (LANES, ctx), 0)
    ej = lax.broadcasted_iota(jnp.int32, (LANES, ctx), 1) // NSA_SEL_BLOCK
    picked = _dot(sel.astype(BF16), jnp.where(ei == ej, 1.0, 0.0).astype(BF16))
    mask = (jnp.concatenate([picked] * HEADS, axis=0) > 0.5) & (_kpos(0, ctx, rows) <= qpos)
    slc = _softmax_context(_dot(qs, sk_ref[...]) * scale, mask, sv_ref[...])

    wk_ref[:, 0:w_eff] = wst_ref[0, 0, 0:HD, :].astype(BF16)
    wv_ref[:, 0:w_eff] = wst_ref[0, 0, HD:2 * HD, :].astype(BF16)
    wk_ref[:, w_eff:] = wnew_ref[0, 0:HD, :].astype(BF16)
    wv_ref[:, w_eff:] = wnew_ref[0, HD:2 * HD, :].astype(BF16)
    wmask = _win_mask(past - w_eff, wk_ref.shape[1], qpos, rows)
    win = _softmax_context(_dot(qs, wk_ref[...]) * scale, wmask, wv_ref[...])
    widen = lambda c: (c[0], c[1], jnp.concatenate([c[2]] * HEADS, axis=1))
    o_ref[0] = _nsa_combine(g_ref[0], o_cmp, widen(slc), widen(win), tq, masks)


def _nsa_cmp_weights(pe, w1, b1, w2):
    z = jnp.zeros((NSA_CMP_LEN, HD, NSA_CMP_HID), F32)
    w1cat = jnp.concatenate([jnp.concatenate([w1[0], z], axis=2), jnp.concatenate([z, w1[1]], axis=2)], axis=1)
    b1cat = jnp.concatenate([b1[0], b1[1]])[None, :]
    pecat = jnp.concatenate([pe[0], pe[1]], axis=1)
    z2 = jnp.zeros((NSA_CMP_HID, GROUP_W), F32)
    w2k4 = jnp.concatenate([jnp.tile(w2[0], (1, HEADS)), z2], axis=0)
    w2v4 = jnp.concatenate([z2, jnp.tile(w2[1], (1, HEADS))], axis=0)
    return w1cat.astype(BF16), b1cat, pecat, w2k4.astype(BF16), w2v4.astype(BF16)


def _cmp_specs(index_map):
    full = lambda *shape: pl.BlockSpec(shape, lambda *a: (0,) * len(shape))
    return [full(NSA_CMP_LEN, LANES, 2 * NSA_CMP_HID), full(1, 2 * NSA_CMP_HID),
            full(NSA_CMP_LEN, LANES), full(2 * NSA_CMP_HID, GROUP_W), full(2 * NSA_CMP_HID, GROUP_W)]


def _layer_norm(x, g, b):
    xc = x - jnp.mean(x, axis=-1, keepdims=True)
    var = jnp.mean(xc * xc, axis=-1, keepdims=True)
    return xc * lax.rsqrt(var + LN_EPS) * g + b


def _outproj_kernel(x_ref, a_ref, b_ref, c_ref, d_ref, w_ref, g_ref, be_ref, o_ref):
    mix = _dot(a_ref[...].astype(BF16), w_ref[0:256, :])
    mix = mix + _dot(b_ref[...].astype(BF16), w_ref[256:512, :])
    mix = mix + _dot(c_ref[...].astype(BF16), w_ref[512:768, :])
    mix = mix + _dot(d_ref[...].astype(BF16), w_ref[768:1024, :])
    o_ref[...] = _layer_norm(ALPHA * x_ref[...] + mix, g_ref[...], be_ref[...])


def _outproj_norm(x2d, outs, w_out_bf, gain, bias, tm):
    n = x2d.shape[0]
    row = lambda w: pl.BlockSpec((tm, w), lambda r: (r, 0))
    return pl.pallas_call(
        _outproj_kernel,
        grid=(n // tm,),
        in_specs=[row(D_MODEL), row(GROUP_W), row(GROUP_W), row(GROUP_W), row(GROUP_W),
                  pl.BlockSpec((D_MODEL, D_MODEL), lambda r: (0, 0)),
                  pl.BlockSpec((1, D_MODEL), lambda r: (0, 0)), pl.BlockSpec((1, D_MODEL), lambda r: (0, 0))],
        out_specs=row(D_MODEL),
        out_shape=jax.ShapeDtypeStruct((n, D_MODEL), F32),
        compiler_params=_params(1),
        name="outproj_ln",
    )(x2d, *outs, w_out_bf, gain, bias)


def _moe_kernel(h_ref, wr_ref, rb_ref, wgu_ref, wd_ref, g_ref, be_ref, o_ref, hb_ref, comb_ref, acc_ref):
    e = pl.program_id(1)

    @pl.when(e == 0)
    def _():
        hb = h_ref[...].astype(BF16)
        hb_ref[...] = hb
        logit = _dot(hb, wr_ref[...]) + rb_ref[...]
        lane = lax.broadcasted_iota(jnp.int32, logit.shape, 1)
        is_g = lane < N_GROUPS
        lg = jnp.where(is_g, logit, NEG_BIG)
        gmax = jnp.max(lg, axis=-1, keepdims=True)
        gsel = jnp.min(jnp.where(is_g & (lg == gmax), lane, LANES), axis=-1, keepdims=True)
        pg = 1.0 / jnp.sum(jnp.where(is_g, jnp.exp(lg - gmax), 0.0), axis=-1, keepdims=True)
        e0 = N_GROUPS + gsel * EXPERTS_PER_GROUP
        in_g = (lane >= e0) & (lane < e0 + EXPERTS_PER_GROUP)
        le = jnp.where(in_g, logit, NEG_BIG)
        v1 = jnp.max(le, axis=-1, keepdims=True)
        i1 = jnp.min(jnp.where(in_g & (le == v1), lane, LANES), axis=-1, keepdims=True)
        rest = in_g & (lane != i1)
        le2 = jnp.where(rest, logit, NEG_BIG)
        v2 = jnp.max(le2, axis=-1, keepdims=True)
        i2 = jnp.min(jnp.where(rest & (le2 == v2), lane, LANES), axis=-1, keepdims=True)
        ex = jnp.exp(v2 - v1)
        w1 = pg / (1.0 + ex)
        w2 = pg * ex / (1.0 + ex)
        comb_ref[...] = jnp.where(lane == i1, w1, 0.0) + jnp.where(lane == i2, w2, 0.0)
        acc_ref[...] = jnp.zeros_like(acc_ref)

    hids = []
    for k in range(EXPERTS_PER_STEP):
        gu = _dot(hb_ref[...], wgu_ref[k])
        gt, up = gu[:, :EXPERT_HID], gu[:, EXPERT_HID:]
        hid = gt / (1.0 + jnp.exp(-gt)) * up
        hids.append((hid * _col(comb_ref[...], N_GROUPS + e * EXPERTS_PER_STEP + k)).astype(BF16))
    acc_ref[...] += _dot(jnp.concatenate(hids, axis=1), wd_ref[...].reshape(EXPERTS_PER_STEP * EXPERT_HID, D_MODEL))

    @pl.when(e == N_EXPERTS // EXPERTS_PER_STEP - 1)
    def _():
        o_ref[...] = _layer_norm(ALPHA * h_ref[...] + acc_ref[...], g_ref[...], be_ref[...])


def _moe_norm(h2d, wr, rb, wgu, wd, gain, bias, tm):
    n = h2d.shape[0]
    assert n % tm == 0
    return pl.pallas_call(
        _moe_kernel,
        grid=(n // tm, N_EXPERTS // EXPERTS_PER_STEP),
        in_specs=[pl.BlockSpec((tm, D_MODEL), lambda r, e: (r, 0)),
                  pl.BlockSpec((D_MODEL, LANES), lambda r, e: (0, 0)),
                  pl.BlockSpec((1, LANES), lambda r, e: (0, 0)),
                  pl.BlockSpec((EXPERTS_PER_STEP, D_MODEL, 2 * EXPERT_HID), lambda r, e: (e, 0, 0)),
                  pl.BlockSpec((EXPERTS_PER_STEP, EXPERT_HID, D_MODEL), lambda r, e: (e, 0, 0)),
                  pl.BlockSpec((1, D_MODEL), lambda r, e: (0, 0)),
                  pl.BlockSpec((1, D_MODEL), lambda r, e: (0, 0))],
        out_specs=pl.BlockSpec((tm, D_MODEL), lambda r, e: (r, 0)),
        out_shape=jax.ShapeDtypeStruct((n, D_MODEL), F32),
        scratch_shapes=[pltpu.VMEM((tm, D_MODEL), BF16), pltpu.VMEM((tm, LANES), F32),
                        pltpu.VMEM((tm, D_MODEL), F32)],
        compiler_params=_params(2),
        name="moe_ln",
    )(h2d, wr, rb, wgu, wd, gain, bias)


def _layer_weights(l, w_in, w_out, pe, w1, b1, w2, lam_p, subln, ln_g, ln_b, rgw, rgb, rew, reb, wg, wu, wd):
    wr = jnp.concatenate([rgw[l], rew[l], jnp.zeros((D_MODEL, LANES - N_GROUPS - N_EXPERTS), F32)], axis=1)
    rb = jnp.concatenate([rgb[l], reb[l], jnp.zeros((LANES - N_GROUPS - N_EXPERTS,), F32)])[None, :]
    wq, wkv = _proj_weights(w_in[l])
    return dict(
        wq=wq, wkv=wkv, wkv_t=wkv.T, wall=jnp.concatenate([wq, wkv], axis=1),
        w_out=w_out[l].astype(BF16),
        cmp=_nsa_cmp_weights(pe[l], w1[l], b1[l], w2[l]),
        lam_p=lam_p[l],
        subln4=jnp.tile(subln[l], HEADS)[None, :],
        lam_init=0.8 - 0.6 * math.exp(-0.3 * l),
        g1=ln_g[l, 0][None, :], b1=ln_b[l, 0][None, :], g2=ln_g[l, 1][None, :], b2=ln_b[l, 1][None, :],
        wr=wr.astype(BF16), rb=rb,
        wgu=jnp.concatenate([wg[l], wu[l]], axis=2).astype(BF16),
        wd=wd[l].astype(BF16),
    )


def _ffn(x, outs, lw, tm, tm_ffn):
    b, t, _ = x.shape
    x2d = x.reshape(b * t, D_MODEL)
    h = _outproj_norm(x2d, [o.reshape(b * t, GROUP_W) for o in outs], lw["w_out"], lw["g1"], lw["b1"], tm)
    y = _moe_norm(h, lw["wr"], lw["rb"], lw["wgu"], lw["wd"], lw["g2"], lw["b2"], tm_ffn)
    return y.reshape(b, t, D_MODEL)


def _new_tile(rows, b, t, page):
    w = rows.shape[-1]
    return jnp.pad(jnp.transpose(rows.reshape(b, t, w), (0, 2, 1)), ((0, 0), (0, 0), (0, page - t)))


def kernel(x_prompt, x_sample, cache_sb_kv, cache_nsa_kv, cache_diff_kv, cache_moba_kv, state_nsa_win, page_table, w_in, w_out, nsa_cmp_pe, nsa_cmp_w1, nsa_cmp_b1, nsa_cmp_w2, diff_lambda, diff_subln, ln_gain, ln_bias, router_group_w, router_group_b, router_expert_w, router_expert_b, expert_w_gate, expert_w_up, expert_w_down):
    bp, tp, _ = x_prompt.shape
    bs, ts, _ = x_sample.shape
    depth = w_in.shape[0]
    n_phys, page = cache_sb_kv.shape[1], cache_sb_kv.shape[2]
    n_pages = page_table.shape[1]
    past = n_pages * page
    w_eff = state_nsa_win.shape[2]
    l_s = past + ts
    l_s_pad = -(-l_s // MOBA_BLOCK) * MOBA_BLOCK
    tm_p = 256
    tm_s = min(256, bs * ts)
    assert tp % KV_TILE == 0 and tm_p % ts == 0 and (bs * ts) % tm_s == 0 and ts <= page
    assert page == LANES and past % MOBA_BLOCK == 0 and w_eff % LANES == 0 and w_eff == min(NSA_WINDOW, past)

    pool_sb = jnp.transpose(cache_sb_kv, (0, 1, 3, 4, 5, 2)).reshape(depth, n_phys, 2 * GROUP_W, page)
    pool_diff = jnp.transpose(cache_diff_kv, (0, 1, 3, 4, 5, 2)).reshape(depth, n_phys, 2 * GROUP_W, page)
    pool_moba = jnp.transpose(cache_moba_kv, (0, 1, 3, 4, 5, 2)).reshape(depth, n_phys, 2 * GROUP_W, page)
    pool_nsa = jnp.transpose(cache_nsa_kv, (0, 1, 3, 4, 2)).reshape(depth, n_phys, 4 * HD, page)
    win_state = jnp.transpose(state_nsa_win, (0, 1, 3, 4, 2)).reshape(depth, bs, 2 * HD, w_eff)

    tabs_p = _rope_tables(jnp.arange(tp, dtype=jnp.int32))
    tabs_s = _rope_tables(past + (jnp.arange(tm_s, dtype=jnp.int32) % ts))
    tabs_p_t, tabs_s_t = jnp.transpose(tabs_p, (0, 2, 1)), jnp.transpose(tabs_s, (0, 2, 1))
    n_cmp_p, n_sel_p = tp // NSA_STRIDE, tp // NSA_SEL_BLOCK
    n_cmp_s, n_sel_s = l_s_pad // NSA_STRIDE, l_s_pad // NSA_SEL_BLOCK
    ncp = lambda n: -(-n // LANES) * LANES

    y_p, y_s = x_prompt, x_sample
    kv_bufs, new_s = None, []
    for l in range(depth):
        lw = _layer_weights(l, w_in, w_out, nsa_cmp_pe, nsa_cmp_w1, nsa_cmp_b1, nsa_cmp_w2, diff_lambda,
                            diff_subln, ln_gain, ln_bias, router_group_w, router_group_b, router_expert_w,
                            router_expert_b, expert_w_gate, expert_w_up, expert_w_down)
        lam_specs = [pl.BlockSpec((4, DIFF_QK), lambda *a: (0, 0)), pl.BlockSpec((1, GROUP_W), lambda *a: (0, 0))]

        pp = _project(y_p.reshape(bp * tp, D_MODEL), lw["wq"], lw["wkv_t"], tabs_p, tabs_p_t, tm_p,
                      Q_SEGS, KV_SEGS, tp, depth, l, kv_bufs)
        kv_bufs = [pp[s_[0]] for s_ in KV_SEGS]
        q3 = lambda a: a.reshape(bp, tp, a.shape[-1])
        o_sb = _prompt_call(functools.partial(_sb_kernel, tq=256, tk=KV_TILE, pos0=0, chunk=HEADS * 256),
                            q3(pp["sbq"]), pp["sbkv"], l, (), (), 256, "sb_attn",
                            scratch=[pltpu.VMEM((HEADS * 256, GROUP_W), BF16),
                                     pltpu.VMEM((HEADS * 256, GROUP_W), F32),
                                     pltpu.VMEM((HEADS * 256, 1), F32)])
        o_nsa = _prompt_call(
            functools.partial(_nsa_kernel, tq=256, tk=KV_TILE, pos0=0, n_cmp=n_cmp_p, n_sel=n_sel_p),
            q3(pp["nq"]), pp["nkv"], l, (q3(pp["gate"]), pp["wkv"]) + lw["cmp"],
            [pl.BlockSpec((1, 256, LANES), lambda bb, i: (bb, i, 0)),
             pl.BlockSpec((None, 1, 2 * HD, tp), lambda bb, i: (l, bb, 0, 0))] + _cmp_specs(None), 256, "nsa_attn",
            scratch=[pltpu.VMEM((tp, LANES), F32), pltpu.VMEM((ncp(n_cmp_p), GROUP_W), F32),
                     pltpu.VMEM((ncp(n_cmp_p), GROUP_W), F32)])
        o_diff = _prompt_call(
            functools.partial(_diff_kernel, tq=128, tk=KV_TILE, pos0=0, lam_init=lw["lam_init"]),
            q3(pp["dq"]), pp["dkv"], l, (lw["lam_p"], lw["subln4"]), lam_specs, 128, "diff_attn")
        o_moba = _prompt_call(
            functools.partial(_moba_kernel, tq=256, tk=KV_TILE, pos0=0, n_blocks=tp // MOBA_BLOCK),
            q3(pp["mq"]), pp["mkv"], l, (), (), 256, "moba_attn", scratch=[pltpu.VMEM((LANES, GROUP_W), F32)])
        y_p = _ffn(y_p, (o_sb, o_nsa, o_diff, o_moba), lw, tm_p, math.gcd(1024, bp * tp))

        ps = _project(y_s.reshape(bs * ts, D_MODEL), lw["wall"], lw["wkv_t"], tabs_s, tabs_s_t, tm_s,
                      Q_SEGS + KV_SEGS, (), ts)
        s3 = lambda a: a.reshape(bs, ts, a.shape[-1])
        new_tile = lambda name: _new_tile(ps[name], bs, ts, page)
        common = dict(n_pages=n_pages, page=page, past=past)
        spb = SAMPLES_PER_STEP
        ctx_scratch = lambda ch: [((ch, past + page), BF16), ((ch, past + page), BF16)]
        o_sb = _sample_call(functools.partial(_sb_sample_kernel, **common), page_table, s3(ps["sbq"]),
                            pool_sb, l, new_tile("sbkv"), (), (), (), "sb_attn_s", scratch=ctx_scratch(GROUP_W))
        o_nsa = _sample_call(
            functools.partial(_nsa_sample_kernel, n_cmp=n_cmp_s, n_sel=n_sel_s, w_eff=w_eff, **common),
            page_table, s3(ps["nq"]), pool_nsa, l, new_tile("nkv"),
            (s3(ps["gate"]), win_state, new_tile("wkv")) + lw["cmp"],
            [pl.BlockSpec((spb, ts, LANES), lambda bb, pt: (bb, 0, 0)),
             pl.BlockSpec((1, spb, 2 * HD, w_eff), lambda bb, pt: (l, bb, 0, 0)),
             pl.BlockSpec((spb, 2 * HD, page), lambda bb, pt: (bb, 0, 0))] + _cmp_specs(None),
            (0, 1, 0, None, None, None, None, None), "nsa_attn_s",
            scratch=[((l_s_pad, LANES), F32), ((ncp(n_cmp_s), GROUP_W), F32), ((ncp(n_cmp_s), GROUP_W), F32)]
            + ctx_scratch(HD) + [((HD, w_eff + page), BF16), ((HD, w_eff + page), BF16)])
        o_diff = _sample_call(functools.partial(_diff_sample_kernel, lam_init=lw["lam_init"], **common),
                              page_table, s3(ps["dq"]), pool_diff, l, new_tile("dkv"),
                              (lw["lam_p"], lw["subln4"]), lam_specs, (None, None), "diff_attn_s",
                              scratch=ctx_scratch(GROUP_W))
        o_moba = _sample_call(functools.partial(_moba_sample_kernel, **common), page_table, s3(ps["mq"]),
                              pool_moba, l, new_tile("mkv"), (), (), (), "moba_attn_s",
                              scratch=ctx_scratch(GROUP_W))
        y_s = _ffn(y_s, (o_sb, o_nsa, o_diff, o_moba), lw, tm_s, math.gcd(1024, bs * ts))
        win_new = jnp.concatenate([win_state[l][:, :, ts:], new_tile("wkv")[:, :, :ts]], axis=2)
        new_s.append((ps["sbkv"], ps["nkv"], ps["dkv"], ps["mkv"], win_new))

    def heads_out(a, b, t):
        return jnp.transpose(a.reshape(depth, b, 2, HEADS, HD, t), (0, 1, 5, 2, 3, 4))

    def slots_out(a, b, t, n):
        return jnp.transpose(a.reshape(depth, b, n, HD, t), (0, 1, 4, 2, 3))

    st_p = [pp["sbkv"], pp["nkv"], pp["dkv"], pp["mkv"], pp["wkv"][:, :, :, tp - min(NSA_WINDOW, tp):]]
    st_s = [jnp.stack([r[k] for r in new_s]) for k in range(5)]
    outs_p = (heads_out(st_p[0], bp, tp), slots_out(st_p[1], bp, tp, 4), heads_out(st_p[2], bp, tp),
              heads_out(st_p[3], bp, tp), slots_out(st_p[4], bp, min(NSA_WINDOW, tp), 2))
    outs_s = (st_s[0].reshape(depth, bs, ts, 2, HEADS, HD), st_s[1].reshape(depth, bs, ts, 4, HD),
              st_s[2].reshape(depth, bs, ts, 2, HEADS, HD), st_s[3].reshape(depth, bs, ts, 2, HEADS, HD),
              slots_out(st_s[4], bs, w_eff, 2))
    return (y_p, y_s, *outs_p, *outs_s)
```

```python
import functools
import math

import numpy as np
import jax
import jax.numpy as jnp
from jax import lax
from jax.experimental import pallas as pl
from jax.experimental.pallas import tpu as pltpu

F32 = jnp.float32
BF16 = jnp.bfloat16

D_MODEL = 1024
HEADS = 4
HD = 64
GROUP_W = HEADS * HD
DIFF_QK = HD // 2
DEPTH_LAYERS = 2
ROPE_THETA = 500000.0
ROPE_FRACTION = 4
NSA_STRIDE = 16
NSA_CMP_LEN = 32
NSA_CMP_HID = 128
NSA_SEL_BLOCK = 64
NSA_TOPN = 16
NSA_WINDOW = 512
NSA_FORCE = 1.0e4
MOBA_BLOCK = 256
MOBA_TOPK = 3
N_GROUPS = 4
EXPERTS_PER_GROUP = 4
N_EXPERTS = 16
EXPERT_HID = 256
EXPERTS_PER_STEP = 2
ALPHA = (2 * DEPTH_LAYERS) ** 0.25
LN_EPS = 1e-5
RMS_EPS = 1e-5
NEG_BIG = -1e30
LANES = 128
KV_TILE = 256
SAMPLES_PER_STEP = 2
NSA_SAMPLES_PER_STEP = 4
VMEM_LIMIT = 56 * 1024 * 1024

Q_SEGS = (("sbq", 256, (0, 0)), ("nq", 256, (1, 1)), ("dq", 256, (3, 3)), ("mq", 256, (1, 1)),
          ("gate", 128, (4,)))
KV_SEGS = (("sbkv", 512, (0, 0, 0, 0)), ("nkv", 256, (2, 2)), ("wkv", 128, (2,)),
           ("dkv", 512, (3, 3, 0, 0)), ("mkv", 512, (1, 1, 0, 0)))
Q_COLS = sum(s[1] for s in Q_SEGS)
KV_COLS = sum(s[1] for s in KV_SEGS)


def _dot(a, b):
    return jnp.dot(a, b, preferred_element_type=F32)


def _dot_t(a, b):
    return lax.dot_general(a, b, (((1,), (1,)), ((), ())), preferred_element_type=F32)


def _split_dot(a, b_bf):
    hi = a.astype(BF16)
    lo = (a - hi.astype(F32)).astype(BF16)
    return _dot(hi, b_bf) + _dot(lo, b_bf)


def _lane_group_masks(width, group, n):
    lane = lax.broadcasted_iota(jnp.int32, (1, width), 1)
    return [jnp.where((lane // group) == g, 1.0, 0.0).astype(F32) for g in range(n)]


def _stack_groups(q, masks):
    return jnp.concatenate([q * m for m in masks], axis=0)


def _col(x, idx):
    lane = lax.broadcasted_iota(jnp.int32, x.shape, 1)
    return jnp.sum(jnp.where(lane == idx, x, 0.0), axis=-1, keepdims=True)


def _params(n_axes):
    return pltpu.CompilerParams(dimension_semantics=("arbitrary",) * n_axes, vmem_limit_bytes=VMEM_LIMIT)


def _rope_rows(y, tab_ref, kind):
    half = 4 if kind == 3 else 8
    t0 = (kind - 1) * 3
    return (y * tab_ref[t0] + pltpu.roll(y, LANES - half, 1) * tab_ref[t0 + 1]
            + pltpu.roll(y, half, 1) * tab_ref[t0 + 2])


def _rope_cols(y, tab_ref, kind):
    half = 4 if kind == 3 else 8
    t0 = (kind - 1) * 3
    return (y * tab_ref[t0] + pltpu.roll(y, LANES - half, 0) * tab_ref[t0 + 1]
            + pltpu.roll(y, half, 0) * tab_ref[t0 + 2])


def _proj_kernel(x_ref, w_ref, wt_ref, tab_ref, tabt_ref, *out_refs, row_segs, col_segs, n_carried):
    out_refs = out_refs[n_carried:]
    xb = x_ref[...].astype(BF16)
    off = 0
    n_row = len(row_segs)
    for (_, width, kinds), o_ref in zip(row_segs, out_refs[:n_row]):
        for c0 in range(0, width, 256):
            cw = min(256, width - c0)
            y2 = _dot(xb, w_ref[:, off + c0:off + c0 + cw])
            for cc in range(cw // LANES):
                y = y2[:, cc * LANES:(cc + 1) * LANES]
                kind = kinds[(c0 // LANES) + cc]
                if kind in (1, 2, 3):
                    y = _rope_rows(y, tab_ref, kind)
                elif kind == 4:
                    y = 1.0 / (1.0 + jnp.exp(-y))
                o_ref[:, c0 + cc * LANES:c0 + (cc + 1) * LANES] = y
        off += width
    off = 0
    for (_, width, kinds), o_ref in zip(col_segs, out_refs[n_row:]):
        for c0 in range(0, width, 256):
            cw = min(256, width - c0)
            y2 = _dot_t(wt_ref[off + c0:off + c0 + cw, :], xb)
            for cc in range(cw // LANES):
                y = y2[cc * LANES:(cc + 1) * LANES, :]
                kind = kinds[(c0 // LANES) + cc]
                if kind in (1, 2, 3):
                    y = _rope_cols(y, tabt_ref, kind)
                o_ref[0, c0 + cc * LANES:c0 + (cc + 1) * LANES, :] = y
        off += width


def _project(x2d, w_rows, w_cols, tabs, tabs_t, tm, row_segs, col_segs, seq, depth=1, layer=0, carried=None):
    n = x2d.shape[0]
    carried = list(carried or [])
    n_tab = tabs.shape[1] // tm
    per_seq = max(seq // tm, 1)
    in_specs = [pl.BlockSpec((tm, D_MODEL), lambda r: (r, 0)),
                pl.BlockSpec(w_rows.shape, lambda r: (0, 0)),
                pl.BlockSpec(w_cols.shape, lambda r: (0, 0)),
                pl.BlockSpec((9, tm, LANES), lambda r: (0, r % n_tab, 0)),
                pl.BlockSpec((9, LANES, tm), lambda r: (0, 0, r % n_tab))]
    out_specs = [pl.BlockSpec((tm, s[1]), lambda r: (r, 0)) for s in row_segs]
    in_specs += [pl.BlockSpec(memory_space=pl.ANY)] * len(carried)
    out_specs += [pl.BlockSpec((None, 1, s[1], tm), lambda r: (layer, r // per_seq, 0, r % per_seq))
                  for s in col_segs]
    out_shape = [jax.ShapeDtypeStruct((n, s[1]), F32) for s in row_segs]
    out_shape += [jax.ShapeDtypeStruct((depth, n // seq, s[1], seq), F32) for s in col_segs]
    outs = pl.pallas_call(
        functools.partial(_proj_kernel, row_segs=row_segs, col_segs=col_segs, n_carried=len(carried)),
        grid=(n // tm,),
        in_specs=in_specs, out_specs=out_specs, out_shape=out_shape,
        input_output_aliases={5 + k: len(row_segs) + k for k in range(len(carried))},
        compiler_params=_params(1),
        name="proj",
    )(x2d, w_rows, w_cols, tabs, tabs_t, *carried)
    return dict(zip([s[0] for s in row_segs + col_segs], outs))


def _rope_tables(pos):
    pos = pos.astype(F32)
    lane = np.arange(LANES)
    tabs = []
    for dim, first_only in ((HD, False), (HD, True), (DIFF_QK, False)):
        rot = dim // ROPE_FRACTION
        half = rot // 2
        inv = jnp.power(jnp.float32(ROPE_THETA), -jnp.arange(half, dtype=F32) * (2.0 / rot))
        ang = pos[:, None] * inv[None, :]
        cos, sin = jnp.cos(ang), jnp.sin(ang)
        c = lane % dim
        in_rot = c < rot
        if first_only:
            in_rot = in_rot & (lane < HD)
        first = in_rot & (c < half)
        second = in_rot & (c >= half)
        fidx = c % half
        cos_l, sin_l = cos[:, fidx], sin[:, fidx]
        tabs.append(jnp.where(in_rot[None, :], cos_l, 1.0))
        tabs.append(jnp.where(first[None, :], -sin_l, 0.0))
        tabs.append(jnp.where(second[None, :], sin_l, 0.0))
    return jnp.stack(tabs).astype(F32)


def _proj_weights(w_in_l):
    o = [int(v) for v in np.cumsum((0, 256, 256, 256, 256, 256, 128, 12, 256, 256, 256, 256, 256, 256))]
    sl = lambda a, b: w_in_l[:, a:b]
    gate = jnp.pad(sl(o[6], o[7]), ((0, 0), (0, LANES - 12)))
    wq = jnp.concatenate([sl(o[0], o[1]), sl(o[3], o[4]), sl(o[7], o[8]), sl(o[10], o[11]), gate], axis=1)
    wkv = jnp.concatenate([sl(o[1], o[3]), sl(o[4], o[5]), sl(o[5], o[6]), sl(o[8], o[10]), sl(o[11], o[13])],
                          axis=1)
    return wq.astype(BF16), wkv.astype(BF16)


def _qpos(p_lo, tq, rows):
    r = lax.broadcasted_iota(jnp.int32, (rows, 1), 0)
    return p_lo + (r & (tq - 1))


def _kpos(kpos0, tk, rows):
    return kpos0 + lax.broadcasted_iota(jnp.int32, (rows, tk), 1)


def _flash_step(qb, kt, vt, mask, scale, carry):
    m, l, acc = carry
    s = _dot(qb, kt) * scale
    if mask is not None:
        s = jnp.where(mask, s, NEG_BIG)
    m_new = jnp.maximum(m, jnp.max(s, axis=-1, keepdims=True))
    a = jnp.exp(m - m_new)
    p = jnp.exp(s - m_new)
    l = a * l + jnp.sum(p, axis=-1, keepdims=True)
    acc = a * acc + _dot_t(p.astype(BF16), vt)
    return m_new, l, acc


def _flash_init(rows):
    return (jnp.full((rows, 1), NEG_BIG, F32), jnp.zeros((rows, 1), F32), jnp.zeros((rows, GROUP_W), F32))


def _unstack_heads(rows_val, tq, masks):
    out = rows_val[0:tq] * masks[0]
    for h in range(1, HEADS):
        out = out + rows_val[h * tq:(h + 1) * tq] * masks[h]
    return out


def _later_matrix(tk):
    r_i = lax.broadcasted_iota(jnp.int32, (tk, tk), 0)
    c_i = lax.broadcasted_iota(jnp.int32, (tk, tk), 1)
    return jnp.where(r_i > c_i, 1.0, 0.0).astype(BF16)


def _tile_from_ref(ref, idx, j, tk):
    cols = pl.ds(pl.multiple_of(j * tk, tk), tk)
    return (ref[idx + (slice(0, GROUP_W), cols)].astype(BF16),
            ref[idx + (slice(GROUP_W, 2 * GROUP_W), cols)].astype(BF16))


def _page_tile(ref):
    idx = (0,) * (len(ref.shape) - 2)
    return (ref[idx + (slice(0, GROUP_W), slice(None))].astype(BF16),
            ref[idx + (slice(GROUP_W, 2 * GROUP_W), slice(None))].astype(BF16))


def _rows_to_tile(rows, page):
    t, w = rows.shape
    return jnp.concatenate([rows, jnp.zeros((page - t, w), F32)], axis=0).T.astype(BF16)


def _gather_context(page_refs, new_ref, kt_ref, vt_ref, page):
    for p, ref in enumerate(page_refs):
        kt, vt = _page_tile(ref)
        kt_ref[:, p * page:(p + 1) * page] = kt
        vt_ref[:, p * page:(p + 1) * page] = vt
    new_t = _rows_to_tile(new_ref[0], page)
    p = len(page_refs)
    kt_ref[:, p * page:(p + 1) * page] = new_t[0:GROUP_W]
    vt_ref[:, p * page:(p + 1) * page] = new_t[GROUP_W:2 * GROUP_W]


def _softmax_context(s, mask, vt):
    s = jnp.where(mask, s, NEG_BIG)
    m = jnp.max(s, axis=-1, keepdims=True)
    p = jnp.exp(s - m)
    return m, jnp.sum(p, axis=-1, keepdims=True), _dot_t(p.astype(BF16), vt)


def _sb_tile(qb_ref, acc_ref, tail_ref, kt, vt, later_mat, kpos0, p_lo, tq, causal, chunk):
    rows, tk = qb_ref.shape[0], kt.shape[1]
    for r0 in range(0, rows, chunk):
        rs = slice(r0, r0 + chunk)
        z = _dot(qb_ref[rs, :], kt)
        log_sig = jnp.minimum(z, 0.0) - jnp.log(1.0 + jnp.exp(-jnp.abs(z)))
        log_fail = log_sig - z
        if causal:
            qpos = p_lo + ((r0 + lax.broadcasted_iota(jnp.int32, (chunk, 1), 0)) & (tq - 1))
            mask = _kpos(kpos0, tk, chunk) < qpos
            log_fail = jnp.where(mask, log_fail, 0.0)
        later = _split_dot(log_fail, later_mat) + tail_ref[rs, :]
        w = jnp.exp(log_sig + later)
        if causal:
            w = jnp.where(mask, w, 0.0)
        acc_ref[rs, :] += _dot_t(w.astype(BF16), vt)
        tail_ref[rs, :] += jnp.sum(log_fail, axis=-1, keepdims=True)


def _sb_kernel(q_ref, kv_ref, o_ref, qb_ref, acc_ref, tail_ref, *, tq, tk, pos0, chunk):
    i = pl.program_id(1)
    p_lo = pos0 + i * tq
    masks = _lane_group_masks(GROUP_W, HD, HEADS)
    qb_ref[...] = _stack_groups(q_ref[0] * (HD ** -0.5), masks).astype(BF16)
    acc_ref[...] = jnp.zeros_like(acc_ref)
    tail_ref[...] = jnp.zeros_like(tail_ref)
    later_mat = _later_matrix(tk)
    j_own = (p_lo + tq - 1) // tk
    kt, vt = _tile_from_ref(kv_ref, (0,), j_own, tk)
    _sb_tile(qb_ref, acc_ref, tail_ref, kt, vt, later_mat, j_own * tk, p_lo, tq, True, chunk)

    def body(jj, _):
        j = j_own - 1 - jj
        kt, vt = _tile_from_ref(kv_ref, (0,), j, tk)
        _sb_tile(qb_ref, acc_ref, tail_ref, kt, vt, later_mat, j * tk, p_lo, tq, False, chunk)
        return 0

    lax.fori_loop(0, j_own, body, 0)
    o_ref[0] = _unstack_heads(acc_ref[...], tq, masks)


def _sb_sample_kernel(pt_ref, q_ref, *refs, n_pages, page, past):
    page_refs, new_ref, o_ref = refs[:n_pages], refs[n_pages], refs[n_pages + 1]
    tq = q_ref.shape[1]
    rows = HEADS * tq
    masks = _lane_group_masks(GROUP_W, HD, HEADS)
    kt_ref, vt_ref = refs[n_pages + 2:]
    _gather_context(page_refs, new_ref, kt_ref, vt_ref, page)
    qb = _stack_groups(q_ref[0] * (HD ** -0.5), masks).astype(BF16)
    qpos = _qpos(past, tq, rows)
    later_mat = _later_matrix(page)
    ctx = kt_ref.shape[1]
    z = _dot(qb, kt_ref[...])
    mask = _kpos(0, ctx, rows) < qpos
    log_sig = jnp.minimum(z, 0.0) - jnp.log(1.0 + jnp.exp(-jnp.abs(z)))
    log_fail = jnp.where(mask, log_sig - z, 0.0)
    tail = jnp.zeros((rows, 1), F32)
    later = [None] * (n_pages + 1)
    for p in reversed(range(n_pages + 1)):
        blk = log_fail[:, p * page:(p + 1) * page]
        later[p] = _split_dot(blk, later_mat) + tail
        tail = tail + jnp.sum(blk, axis=-1, keepdims=True)
    w = jnp.where(mask, jnp.exp(log_sig + jnp.concatenate(later, axis=1)), 0.0)
    o_ref[0] = _unstack_heads(_dot_t(w.astype(BF16), vt_ref[...]), tq, masks)


def _prompt_call(kern, q, kv_t, layer, extra, extra_specs, tq, name, scratch=()):
    b, t, _ = q.shape
    rows_kv, l = kv_t.shape[2], kv_t.shape[3]
    return pl.pallas_call(
        kern,
        grid=(b, t // tq),
        in_specs=[pl.BlockSpec((1, tq, GROUP_W), lambda bb, i: (bb, i, 0)),
                  pl.BlockSpec((None, 1, rows_kv, l), lambda bb, i: (layer, bb, 0, 0))] + list(extra_specs),
        out_specs=pl.BlockSpec((1, tq, GROUP_W), lambda bb, i: (bb, i, 0)),
        out_shape=jax.ShapeDtypeStruct((b, t, GROUP_W), F32),
        scratch_shapes=list(scratch),
        compiler_params=_params(2),
        name=name,
    )(q, kv_t, *extra)


def _sample_call(body, page_table, q, pool_t, layer, new_t, extra, extra_specs, extra_axes, name, scratch=(),
                 spb=SAMPLES_PER_STEP, staged=False):
    b, tq, _ = q.shape
    n_pages = page_table.shape[1]
    rows_kv, page = pool_t.shape[2], pool_t.shape[3]
    assert b % spb == 0

    def page_spec(s_, p):
        return pl.BlockSpec((1, 1, rows_kv, page), lambda bb, pt: (layer, pt[(bb * spb + s_) * n_pages + p], 0, 0))

    axes = [0] + [None] * (spb * n_pages) + [0] + list(extra_axes) + [0]
    n_scratch = len(scratch)

    def kern(pt_ref, *refs):
        io, scr = refs[:len(refs) - n_scratch], refs[len(refs) - n_scratch:]
        per_sample = []
        for s_ in range(spb):
            views = []
            for k, (r, ax) in enumerate(zip(io, axes)):
                if 1 <= k <= spb * n_pages:
                    if (k - 1) // n_pages == s_:
                        views.append(r)
                elif ax is None:
                    views.append(r)
                else:
                    views.append(r.at[(slice(None),) * ax + (pl.ds(s_, 1),)])
            per_sample.append(views + [r.at[s_] for r in scr])
        if staged:
            body(pt_ref, per_sample)
        else:
            for refs_s in per_sample:
                body(pt_ref, *refs_s)

    in_specs = ([pl.BlockSpec((spb, tq, GROUP_W), lambda bb, pt: (bb, 0, 0))]
                + [page_spec(s_, p) for s_ in range(spb) for p in range(n_pages)]
                + [pl.BlockSpec((spb, tq, rows_kv), lambda bb, pt: (bb, 0, 0))] + list(extra_specs))
    return pl.pallas_call(
        kern,
        grid_spec=pltpu.PrefetchScalarGridSpec(
            num_scalar_prefetch=1, grid=(b // spb,), in_specs=in_specs,
            out_specs=pl.BlockSpec((spb, tq, GROUP_W), lambda bb, pt: (bb, 0, 0)),
            scratch_shapes=[pltpu.VMEM((spb,) + shape, dt) for shape, dt in scratch]),
        out_shape=jax.ShapeDtypeStruct((b, tq, GROUP_W), F32),
        compiler_params=_params(1),
        name=name,
    )(page_table.reshape(-1), q, *([pool_t] * (spb * n_pages)), new_t, *extra)


def _diff_finish(carry, tq, lam_ref, sub_ref, lam_init):
    _, l, acc = carry
    head_masks = _lane_group_masks(GROUP_W, HD, HEADS)
    o = acc / l
    lp = lam_ref[...]
    lam = (jnp.exp(jnp.sum(lp[0:1] * lp[1:2], axis=-1, keepdims=True))
           - jnp.exp(jnp.sum(lp[2:3] * lp[3:4], axis=-1, keepdims=True)) + lam_init)
    out = jnp.zeros((tq, GROUP_W), F32)
    for h in range(HEADS):
        oh = (o[(2 * h) * tq:(2 * h + 1) * tq] - lam * o[(2 * h + 1) * tq:(2 * h + 2) * tq]) * head_masks[h]
        ms = jnp.sum(oh * oh, axis=-1, keepdims=True) * (1.0 / HD)
        out = out + oh * lax.rsqrt(ms + RMS_EPS)
    return out * sub_ref[...] * (1.0 - lam_init)


def _diff_kernel(q_ref, kv_ref, lam_ref, sub_ref, o_ref, *, tq, tk, pos0, lam_init):
    i = pl.program_id(1)
    p_lo = pos0 + i * tq
    rows = 2 * HEADS * tq
    qb = _stack_groups(q_ref[0], _lane_group_masks(GROUP_W, DIFF_QK, 2 * HEADS)).astype(BF16)
    qpos = _qpos(p_lo, tq, rows)
    j_own = (p_lo + tq - 1) // tk
    kt, vt = _tile_from_ref(kv_ref, (0,), j_own, tk)
    carry = _flash_step(qb, kt, vt, _kpos(j_own * tk, tk, rows) <= qpos, DIFF_QK ** -0.5, _flash_init(rows))

    def body(j, carry):
        kt, vt = _tile_from_ref(kv_ref, (0,), j, tk)
        return _flash_step(qb, kt, vt, None, DIFF_QK ** -0.5, carry)

    carry = lax.fori_loop(0, j_own, body, carry)
    o_ref[0] = _diff_finish(carry, tq, lam_ref, sub_ref, lam_init)


def _diff_sample_kernel(pt_ref, q_ref, *refs, n_pages, page, past, lam_init):
    page_refs, new_ref = refs[:n_pages], refs[n_pages]
    lam_ref, sub_ref, o_ref = refs[n_pages + 1:n_pages + 4]
    tq = q_ref.shape[1]
    rows = 2 * HEADS * tq
    qb = _stack_groups(q_ref[0], _lane_group_masks(GROUP_W, DIFF_QK, 2 * HEADS)).astype(BF16)
    qpos = _qpos(past, tq, rows)
    kt_ref, vt_ref = refs[n_pages + 4:]
    _gather_context(page_refs, new_ref, kt_ref, vt_ref, page)
    s = _dot(qb, kt_ref[...]) * (DIFF_QK ** -0.5)
    mask = _kpos(0, kt_ref.shape[1], rows) <= qpos
    carry = _softmax_context(s, mask, vt_ref[...])
    o_ref[0] = _diff_finish(carry, tq, lam_ref, sub_ref, lam_init)


def _topk_mask_t(score_t, k):
    row = lax.broadcasted_iota(jnp.int32, score_t.shape, 0)
    rank = jnp.zeros(score_t.shape, F32)
    for m in range(score_t.shape[0]):
        sm = score_t[m:m + 1, :]
        tie = jnp.where(row > m, 1.0, 0.0)
        rank = rank + jnp.where(sm > score_t, 1.0, jnp.where(sm == score_t, tie, 0.0))
    return jnp.where(rank < k, 1.0, 0.0)


def _rows_from_t(mask_t):
    n, cols = mask_t.shape
    return jnp.concatenate([mask_t, jnp.zeros((LANES - n, cols), F32)], axis=0).T


def _moba_select(qb, km_t, own, n_blocks):
    gate = _dot(qb, km_t.astype(BF16))
    rows = qb.shape[0]
    lane = lax.broadcasted_iota(jnp.int32, (rows, LANES), 1)
    rank = jnp.zeros((rows, LANES), F32)
    for m in range(n_blocks):
        gm = gate[:, m:m + 1]
        beats = (gm > gate) | ((gm == gate) & (m < lane))
        rank = rank + jnp.where(beats & (m < own), 1.0, 0.0)
    return jnp.where((rank < MOBA_TOPK) & (lane < own), 1.0, 0.0)


def _block_mean_col(kt_f32, m):
    lane = lax.broadcasted_iota(jnp.int32, (GROUP_W, LANES), 1)
    return jnp.where(lane == m, jnp.sum(kt_f32, axis=-1, keepdims=True), 0.0)


def _moba_select_t(qb, km_rows, own):
    gate_t = _dot_t(km_rows.astype(BF16), qb)
    past = lax.broadcasted_iota(jnp.int32, gate_t.shape, 0) < own
    pick_t = _topk_mask_t(jnp.where(past, gate_t, -jnp.inf), MOBA_TOPK)
    return _rows_from_t(jnp.where(past, pick_t, 0.0))


def _moba_kernel(q_ref, kv_ref, o_ref, km_ref, *, tq, tk, pos0, n_blocks):
    i = pl.program_id(1)
    p_lo = pos0 + i * tq
    rows = HEADS * tq
    masks = _lane_group_masks(GROUP_W, HD, HEADS)

    @pl.when(i == 0)
    def _():
        km = jnp.zeros((GROUP_W, LANES), F32)
        for m in range(n_blocks):
            km = km + _block_mean_col(kv_ref[0, 0:GROUP_W, m * MOBA_BLOCK:(m + 1) * MOBA_BLOCK], m)
        km_ref[...] = (km * (1.0 / MOBA_BLOCK)).T

    qb = _stack_groups(q_ref[0], masks).astype(BF16)
    qpos = _qpos(p_lo, tq, rows)
    own = p_lo // MOBA_BLOCK
    sel = _moba_select_t(qb, km_ref[0:16, :], own)

    kt, vt = _tile_from_ref(kv_ref, (0,), own, tk)
    carry = _flash_step(qb, kt, vt, _kpos(own * tk, tk, rows) <= qpos, HD ** -0.5, _flash_init(rows))

    def body(j, carry):
        kt, vt = _tile_from_ref(kv_ref, (0,), j, tk)
        return _flash_step(qb, kt, vt, _col(sel, j) > 0.5, HD ** -0.5, carry)

    _, l, acc = lax.fori_loop(0, own, body, carry)
    o_ref[0] = _unstack_heads(acc / l, tq, masks)


def _moba_sample_kernel(pt_ref, q_ref, *refs, n_pages, page, past):
    page_refs, new_ref, o_ref = refs[:n_pages], refs[n_pages], refs[n_pages + 1]
    tq = q_ref.shape[1]
    rows = HEADS * tq
    per_blk = MOBA_BLOCK // page
    own = past // MOBA_BLOCK
    masks = _lane_group_masks(GROUP_W, HD, HEADS)
    km = jnp.zeros((GROUP_W, LANES), F32)
    for p in range(n_pages):
        km = km + _block_mean_col(page_refs[p][0, 0, 0:GROUP_W, :], p // per_blk)
    km = km * (1.0 / MOBA_BLOCK)
    qb = _stack_groups(q_ref[0], masks).astype(BF16)
    qpos = _qpos(past, tq, rows)
    sel = _moba_select(qb, km, own, own)
    kt_ref, vt_ref = refs[n_pages + 2:]
    _gather_context(page_refs, new_ref, kt_ref, vt_ref, page)
    s = _dot(qb, kt_ref[...]) * (HD ** -0.5)
    ctx = kt_ref.shape[1]
    picked = [jnp.broadcast_to(sel[:, m:m + 1], (rows, MOBA_BLOCK)) for m in range(own)]
    picked.append(jnp.where(_kpos(past, ctx - past, rows) <= qpos, 1.0, 0.0))
    _, l, acc = _softmax_context(s, jnp.concatenate(picked, axis=1) > 0.5, vt_ref[...])
    o_ref[0] = _unstack_heads(acc / l, tq, masks)


def _gelu_tanh(x):
    return 0.5 * x * (1.0 + jnp.tanh(math.sqrt(2.0 / math.pi) * (x + 0.044715 * (x * x * x))))


def _phase_rows(tile_t):
    r = lax.broadcasted_iota(jnp.int32, (LANES, LANES), 0)
    c = lax.broadcasted_iota(jnp.int32, (LANES, LANES), 1)
    perm = jnp.where(c == NSA_STRIDE * (r % 8) + r // 8, 1.0, 0.0).astype(BF16)
    hi = tile_t.astype(BF16)
    lo = (tile_t - hi.astype(F32)).astype(BF16)
    return _dot_t(perm, hi) + _dot_t(perm, lo)


def _nsa_compress(rows_refs, w1_ref, b1_ref, pe_ref, w2k_ref, w2v_ref, ck_refs, cv_refs, n_cmp):
    row = lax.broadcasted_iota(jnp.int32, (n_cmp, 1), 0)
    tokens = []
    for rows_ref in rows_refs:
        first, second = [], []
        for l in range(NSA_STRIDE):
            x = jnp.concatenate([rows_ref[t * LANES + l * 8:t * LANES + l * 8 + 8, :] for t in range(n_cmp // 8)],
                                axis=0)
            xs = jnp.where(row == n_cmp - 1, 0.0, pltpu.roll(x, n_cmp - 1, 0))
            first.append((x + pe_ref[l:l + 1, :]).astype(BF16))
            second.append((xs + pe_ref[l + NSA_STRIDE:l + NSA_STRIDE + 1, :]).astype(BF16))
        tokens.append(jnp.concatenate(first + second, axis=1))
    hid = _dot(jnp.concatenate(tokens, axis=0), w1_ref[...].reshape(NSA_CMP_LEN * LANES, 2 * NSA_CMP_HID))
    hid = _gelu_tanh(hid + b1_ref[...]).astype(BF16)
    ck, cv = _dot(hid, w2k_ref[...]), _dot(hid, w2v_ref[...])
    for k, (ck_ref, cv_ref) in enumerate(zip(ck_refs, cv_refs)):
        ck_ref[...] = jnp.zeros_like(ck_ref)
        cv_ref[...] = jnp.zeros_like(cv_ref)
        ck_ref[0:n_cmp, :] = ck[k * n_cmp:(k + 1) * n_cmp]
        cv_ref[0:n_cmp, :] = cv[k * n_cmp:(k + 1) * n_cmp]


def _nsa_select_t(p_sum, p_first, n_sel):
    tq, ncp = p_sum.shape
    n_rows = -(-n_sel // 8) * 8
    gi = lax.broadcasted_iota(jnp.int32, (n_rows, ncp), 0)
    gj = lax.broadcasted_iota(jnp.int32, (n_rows, ncp), 1) // (NSA_SEL_BLOCK // NSA_STRIDE)
    group_t = jnp.where(gi == gj, 1.0, 0.0).astype(BF16)
    hi = p_sum.astype(BF16)
    lo = (p_sum - hi.astype(F32)).astype(BF16)
    imp_t = _dot_t(group_t, hi) + _dot_t(group_t, lo)
    row = lax.broadcasted_iota(jnp.int32, (n_rows, tq), 0)
    qblk = (p_first + lax.broadcasted_iota(jnp.int32, (n_rows, tq), 1)) // NSA_SEL_BLOCK
    valid = (row <= qblk) & (row < n_sel)
    forced = (row == 0) | (row == qblk) | (row == qblk - 1)
    score_t = jnp.where(valid, jnp.where(forced, NSA_FORCE, imp_t), -jnp.inf)
    pick_t = _topk_mask_t(score_t, min(NSA_TOPN, n_sel))
    return _rows_from_t(jnp.where(valid, pick_t, 0.0))


def _nsa_cmp_and_select(qb, qpos, ck_ref, cv_ref, tq, n_sel):
    rows = qb.shape[0]
    ncp = ck_ref.shape[0]
    s = _dot_t(qb, ck_ref[...].astype(BF16)) * (HD ** -0.5)
    c_end = lax.broadcasted_iota(jnp.int32, (rows, ncp), 1) * NSA_STRIDE + (NSA_CMP_LEN - 1)
    c_mask = c_end <= qpos
    s = jnp.where(c_mask, s, NEG_BIG)
    p = jnp.where(c_mask, jnp.exp(s - jnp.max(s, axis=-1, keepdims=True)), 0.0)
    den = jnp.sum(p, axis=-1, keepdims=True)
    p = p / jnp.where(den > 0, den, 1.0)
    o_cmp = _dot(p.astype(BF16), cv_ref[...].astype(BF16))
    p_sum = p[0:tq]
    for h in range(1, HEADS):
        p_sum = p_sum + p[h * tq:(h + 1) * tq]
    if tq % LANES == 0:
        return o_cmp, _nsa_select_t(p_sum, qpos[0:1, :], n_sel)
    gi = lax.broadcasted_iota(jnp.int32, (ncp, LANES), 0)
    gj = lax.broadcasted_iota(jnp.int32, (ncp, LANES), 1)
    group_mat = jnp.where((gi // (NSA_SEL_BLOCK // NSA_STRIDE)) == gj, 1.0, 0.0).astype(BF16)
    imp = _split_dot(p_sum, group_mat)
    qblk = qpos[0:tq] // NSA_SEL_BLOCK
    lane = lax.broadcasted_iota(jnp.int32, (tq, LANES), 1)
    valid = (lane <= qblk) & (lane < n_sel)
    forced = (lane == 0) | (lane == qblk) | (lane == qblk - 1)
    score = jnp.where(valid, jnp.where(forced, NSA_FORCE, imp), -jnp.inf)
    rank = jnp.zeros((tq, LANES), F32)
    for m in range(n_sel):
        sm = score[:, m:m + 1]
        beats = (sm > score) | ((sm == score) & (m < lane))
        rank = rank + jnp.where(beats, 1.0, 0.0)
    sel = jnp.where((rank < min(NSA_TOPN, n_sel)) & valid, 1.0, 0.0)
    return o_cmp, sel


def _shared_tile(k64, v64):
    return (jnp.concatenate([k64] * HEADS, axis=0).astype(BF16),
            jnp.concatenate([v64] * HEADS, axis=0).astype(BF16))


def _slc_mask(sel_bf, blk0, kpos0, tk, qpos, rows, causal):
    ei = lax.broadcasted_iota(jnp.int32, (LANES, tk), 0)
    ej = lax.broadcasted_iota(jnp.int32, (LANES, tk), 1) // NSA_SEL_BLOCK
    expand = jnp.where(ei == ej + blk0, 1.0, 0.0).astype(BF16)
    picked = _dot(sel_bf, expand)
    picked = jnp.concatenate([picked] * HEADS, axis=0) > 0.5
    return picked & (_kpos(kpos0, tk, rows) <= qpos) if causal else picked


def _win_mask(kpos0, tk, qpos, rows):
    kp = _kpos(kpos0, tk, rows)
    return (kp <= qpos) & (kp > qpos - NSA_WINDOW)


def _nsa_combine(g, o_cmp, slc, win, tq, masks):
    o_slc = slc[2] / slc[1]
    o_win = win[2] / win[1]
    out = jnp.zeros((tq, GROUP_W), F32)
    for h in range(HEADS):
        r0, r1 = h * tq, (h + 1) * tq
        mix = (g[:, 3 * h:3 * h + 1] * o_cmp[r0:r1] + g[:, 3 * h + 1:3 * h + 2] * o_slc[r0:r1]
               + g[:, 3 * h + 2:3 * h + 3] * o_win[r0:r1])
        out = out + mix * masks[h]
    return out


def _nsa_kernel(q_ref, nkv_ref, g_ref, wkv_ref, w1_ref, b1_ref, pe_ref, w2k_ref, w2v_ref,
                o_ref, rows_ref, ck_ref, cv_ref, *, tq, tk, pos0, n_cmp, n_sel):
    i = pl.program_id(1)
    p_lo = pos0 + i * tq
    rows = HEADS * tq
    masks = _lane_group_masks(GROUP_W, HD, HEADS)

    @pl.when(i == 0)
    def _():
        for c in range(nkv_ref.shape[2] // LANES):
            rows_ref[c * LANES:(c + 1) * LANES, :] = _phase_rows(nkv_ref[0, 0:LANES, c * LANES:(c + 1) * LANES])
        _nsa_compress([rows_ref], w1_ref, b1_ref, pe_ref, w2k_ref, w2v_ref, [ck_ref], [cv_ref], n_cmp)

    qb = _stack_groups(q_ref[0], masks).astype(BF16)
    qpos = _qpos(p_lo, tq, rows)
    scale = HD ** -0.5
    o_cmp, sel = _nsa_cmp_and_select(qb, qpos, ck_ref, cv_ref, tq, n_sel)
    sel_bf = sel.astype(BF16)
    j_own = (p_lo + tq - 1) // tk

    def slc_step(j, carry, causal):
        cols = pl.ds(pl.multiple_of(j * tk, tk), tk)
        kt, vt = _shared_tile(nkv_ref[0, 2 * HD:3 * HD, cols], nkv_ref[0, 3 * HD:4 * HD, cols])
        mask = _slc_mask(sel_bf, j * (tk // NSA_SEL_BLOCK), j * tk, tk, qpos, rows, causal)
        return _flash_step(qb, kt, vt, mask, scale, carry)

    slc = slc_step(j_own, _flash_init(rows), True)
    slc = lax.fori_loop(0, j_own, lambda j, c: slc_step(j, c, False), slc)

    def win_step(j, carry, mask_fn):
        cols = pl.ds(pl.multiple_of(j * tk, tk), tk)
        kt, vt = _shared_tile(wkv_ref[0, 0:HD, cols], wkv_ref[0, HD:2 * HD, cols])
        return _flash_step(qb, kt, vt, mask_fn(_kpos(j * tk, tk, rows)), scale, carry)

    win = win_step(j_own, _flash_init(rows), lambda kp: kp <= qpos)
    j_lo = jnp.maximum(p_lo - (NSA_WINDOW - 1), 0) // tk
    win = lax.fori_loop(j_lo, j_own, lambda j, c: win_step(j, c, lambda kp: kp > qpos - NSA_WINDOW), win)
    o_ref[0] = _nsa_combine(g_ref[0], o_cmp, slc, win, tq, masks)


def _nsa_sample_kernel(pt_ref, samples, *, n_pages, page, past, n_cmp, n_sel, w_eff):
    scale = HD ** -0.5
    masks = _lane_group_masks(GROUP_W, HD, HEADS)
    ctxs = []
    for refs in samples:
        q_ref, page_refs, new_ref = refs[0], refs[1:1 + n_pages], refs[1 + n_pages]
        (g_ref, wst_ref, wnew_ref, w1_ref, b1_ref, pe_ref, w2k_ref, w2v_ref,
         o_ref, rows_ref, ck_ref, cv_ref, sk_ref, sv_ref, wk_ref, wv_ref) = refs[2 + n_pages:]
        ctxs.append(dict(q_ref=q_ref, page_refs=page_refs, new_ref=new_ref, g_ref=g_ref, wst_ref=wst_ref,
                         wnew_ref=wnew_ref, cmp=(w1_ref, b1_ref, pe_ref, w2k_ref, w2v_ref), o_ref=o_ref,
                         rows_ref=rows_ref, ck_ref=ck_ref, cv_ref=cv_ref, sk_ref=sk_ref, sv_ref=sv_ref,
                         wk_ref=wk_ref, wv_ref=wv_ref))
    tq = ctxs[0]["q_ref"].shape[1]
    rows = HEADS * tq
    qpos = _qpos(past, tq, rows)
    assert tq <= NSA_STRIDE

    for c in ctxs:
        c["rows_ref"][...] = jnp.zeros_like(c["rows_ref"])
        for p in range(n_pages):
            c["rows_ref"][p * page:(p + 1) * page, :] = _phase_rows(c["page_refs"][p][0, 0, 0:LANES, :])
        for t in range(tq):
            c["rows_ref"][n_pages * page + 8 * t:n_pages * page + 8 * t + 1, :] = c["new_ref"][0, t:t + 1, 0:LANES]
    for c in ctxs:
        for p in range(n_pages):
            c["sk_ref"][:, p * page:(p + 1) * page] = c["page_refs"][p][0, 0, 2 * HD:3 * HD, :].astype(BF16)
            c["sv_ref"][:, p * page:(p + 1) * page] = c["page_refs"][p][0, 0, 3 * HD:4 * HD, :].astype(BF16)
        new_t = _rows_to_tile(c["new_ref"][0, :, 2 * HD:4 * HD], page)
        c["sk_ref"][:, n_pages * page:] = new_t[0:HD]
        c["sv_ref"][:, n_pages * page:] = new_t[HD:2 * HD]
        c["wk_ref"][:, 0:w_eff] = c["wst_ref"][0, 0, 0:HD, :].astype(BF16)
        c["wv_ref"][:, 0:w_eff] = c["wst_ref"][0, 0, HD:2 * HD, :].astype(BF16)
        wnew_t = _rows_to_tile(c["wnew_ref"][0], page)
        c["wk_ref"][:, w_eff:] = wnew_t[0:HD]
        c["wv_ref"][:, w_eff:] = wnew_t[HD:2 * HD]
        q = c["q_ref"][0]
        c["qb"] = _stack_groups(q, masks).astype(BF16)
        c["qs"] = jnp.concatenate([q[:, h * HD:(h + 1) * HD] for h in range(HEADS)], axis=0).astype(BF16)
    for c in ctxs:
        wmask = _win_mask(past - w_eff, c["wk_ref"].shape[1], qpos, rows)
        c["win"] = _softmax_context(_dot(c["qs"], c["wk_ref"][...]) * scale, wmask, c["wv_ref"][...])
    _nsa_compress([c["rows_ref"] for c in ctxs], *ctxs[0]["cmp"], [c["ck_ref"] for c in ctxs],
                  [c["cv_ref"] for c in ctxs], n_cmp)
    for c in ctxs:
        c["o_cmp"], c["sel"] = _nsa_cmp_and_select(c["qb"], qpos, c["ck_ref"], c["cv_ref"], tq, n_sel)
    ctx = ctxs[0]["sk_ref"].shape[1]
    ei = lax.broadcasted_iota(jnp.int32, (LANES, ctx), 0)
    ej = lax.broadcasted_iota(jnp.int32, (LANES, ctx), 1) // NSA_SEL_BLOCK
    expand = jnp.where(ei == ej, 1.0, 0.0).astype(BF16)
    causal = _kpos(0, ctx, rows) <= qpos
    widen = lambda t: (t[0], t[1], jnp.concatenate([t[2]] * HEADS, axis=1))
    for c in ctxs:
        picked = _dot(c["sel"].astype(BF16), expand)
        mask = (jnp.concatenate([picked] * HEADS, axis=0) > 0.5) & causal
        slc = _softmax_context(_dot(c["qs"], c["sk_ref"][...]) * scale, mask, c["sv_ref"][...])
        c["o_ref"][0] = _nsa_combine(c["g_ref"][0], c["o_cmp"], widen(slc), widen(c["win"]), tq, masks)


def _nsa_cmp_weights(pe, w1, b1, w2):
    z = jnp.zeros((NSA_CMP_LEN, HD, NSA_CMP_HID), F32)
    w1cat = jnp.concatenate([jnp.concatenate([w1[0], z], axis=2), jnp.concatenate([z, w1[1]], axis=2)], axis=1)
    b1cat = jnp.concatenate([b1[0], b1[1]])[None, :]
    pecat = jnp.concatenate([pe[0], pe[1]], axis=1)
    z2 = jnp.zeros((NSA_CMP_HID, GROUP_W), F32)
    w2k4 = jnp.concatenate([jnp.tile(w2[0], (1, HEADS)), z2], axis=0)
    w2v4 = jnp.concatenate([z2, jnp.tile(w2[1], (1, HEADS))], axis=0)
    return w1cat.astype(BF16), b1cat, pecat, w2k4.astype(BF16), w2v4.astype(BF16)


def _cmp_specs(index_map):
    full = lambda *shape: pl.BlockSpec(shape, lambda *a: (0,) * len(shape))
    return [full(NSA_CMP_LEN, LANES, 2 * NSA_CMP_HID), full(1, 2 * NSA_CMP_HID),
            full(NSA_CMP_LEN, LANES), full(2 * NSA_CMP_HID, GROUP_W), full(2 * NSA_CMP_HID, GROUP_W)]


def _layer_norm(x, g, b):
    xc = x - jnp.mean(x, axis=-1, keepdims=True)
    var = jnp.mean(xc * xc, axis=-1, keepdims=True)
    return xc * lax.rsqrt(var + LN_EPS) * g + b


def _outproj_kernel(x_ref, a_ref, b_ref, c_ref, d_ref, w_ref, g_ref, be_ref, o_ref):
    mix = _dot(a_ref[...].astype(BF16), w_ref[0:256, :])
    mix = mix + _dot(b_ref[...].astype(BF16), w_ref[256:512, :])
    mix = mix + _dot(c_ref[...].astype(BF16), w_ref[512:768, :])
    mix = mix + _dot(d_ref[...].astype(BF16), w_ref[768:1024, :])
    o_ref[...] = _layer_norm(ALPHA * x_ref[...] + mix, g_ref[...], be_ref[...])


def _outproj_norm(x2d, outs, w_out_bf, gain, bias, tm):
    n = x2d.shape[0]
    row = lambda w: pl.BlockSpec((tm, w), lambda r: (r, 0))
    return pl.pallas_call(
        _outproj_kernel,
        grid=(n // tm,),
        in_specs=[row(D_MODEL), row(GROUP_W), row(GROUP_W), row(GROUP_W), row(GROUP_W),
                  pl.BlockSpec((D_MODEL, D_MODEL), lambda r: (0, 0)),
                  pl.BlockSpec((1, D_MODEL), lambda r: (0, 0)), pl.BlockSpec((1, D_MODEL), lambda r: (0, 0))],
        out_specs=row(D_MODEL),
        out_shape=jax.ShapeDtypeStruct((n, D_MODEL), F32),
        compiler_params=_params(1),
        name="outproj_ln",
    )(x2d, *outs, w_out_bf, gain, bias)


def _moe_kernel(h_ref, wr_ref, rb_ref, wgu_ref, wd_ref, g_ref, be_ref, o_ref, hb_ref, comb_ref, acc_ref):
    e = pl.program_id(1)

    @pl.when(e == 0)
    def _():
        hb = h_ref[...].astype(BF16)
        hb_ref[...] = hb
        logit = _dot(hb, wr_ref[...]) + rb_ref[...]
        lane = lax.broadcasted_iota(jnp.int32, logit.shape, 1)
        is_g = lane < N_GROUPS
        lg = jnp.where(is_g, logit, NEG_BIG)
        gmax = jnp.max(lg, axis=-1, keepdims=True)
        gsel = jnp.min(jnp.where(is_g & (lg == gmax), lane, LANES), axis=-1, keepdims=True)
        pg = 1.0 / jnp.sum(jnp.where(is_g, jnp.exp(lg - gmax), 0.0), axis=-1, keepdims=True)
        e0 = N_GROUPS + gsel * EXPERTS_PER_GROUP
        in_g = (lane >= e0) & (lane < e0 + EXPERTS_PER_GROUP)
        le = jnp.where(in_g, logit, NEG_BIG)
        v1 = jnp.max(le, axis=-1, keepdims=True)
        i1 = jnp.min(jnp.where(in_g & (le == v1), lane, LANES), axis=-1, keepdims=True)
        rest = in_g & (lane != i1)
        le2 = jnp.where(rest, logit, NEG_BIG)
        v2 = jnp.max(le2, axis=-1, keepdims=True)
        i2 = jnp.min(jnp.where(rest & (le2 == v2), lane, LANES), axis=-1, keepdims=True)
        ex = jnp.exp(v2 - v1)
        w1 = pg / (1.0 + ex)
        w2 = pg * ex / (1.0 + ex)
        comb_ref[...] = jnp.where(lane == i1, w1, 0.0) + jnp.where(lane == i2, w2, 0.0)
        acc_ref[...] = jnp.zeros_like(acc_ref)

    hids = []
    for k in range(EXPERTS_PER_STEP):
        gu = _dot(hb_ref[...], wgu_ref[k])
        gt, up = gu[:, :EXPERT_HID], gu[:, EXPERT_HID:]
        hid = gt / (1.0 + jnp.exp(-gt)) * up
        hids.append((hid * _col(comb_ref[...], N_GROUPS + e * EXPERTS_PER_STEP + k)).astype(BF16))
    acc_ref[...] += _dot(jnp.concatenate(hids, axis=1), wd_ref[...].reshape(EXPERTS_PER_STEP * EXPERT_HID, D_MODEL))

    @pl.when(e == N_EXPERTS // EXPERTS_PER_STEP - 1)
    def _():
        o_ref[...] = _layer_norm(ALPHA * h_ref[...] + acc_ref[...], g_ref[...], be_ref[...])


def _moe_norm(h2d, wr, rb, wgu, wd, gain, bias, tm):
    n = h2d.shape[0]
    assert n % tm == 0
    return pl.pallas_call(
        _moe_kernel,
        grid=(n // tm, N_EXPERTS // EXPERTS_PER_STEP),
        in_specs=[pl.BlockSpec((tm, D_MODEL), lambda r, e: (r, 0)),
                  pl.BlockSpec((D_MODEL, LANES), lambda r, e: (0, 0)),
                  pl.BlockSpec((1, LANES), lambda r, e: (0, 0)),
                  pl.BlockSpec((EXPERTS_PER_STEP, D_MODEL, 2 * EXPERT_HID), lambda r, e: (e, 0, 0)),
                  pl.BlockSpec((EXPERTS_PER_STEP, EXPERT_HID, D_MODEL), lambda r, e: (e, 0, 0)),
                  pl.BlockSpec((1, D_MODEL), lambda r, e: (0, 0)),
                  pl.BlockSpec((1, D_MODEL), lambda r, e: (0, 0))],
        out_specs=pl.BlockSpec((tm, D_MODEL), lambda r, e: (r, 0)),
        out_shape=jax.ShapeDtypeStruct((n, D_MODEL), F32),
        scratch_shapes=[pltpu.VMEM((tm, D_MODEL), BF16), pltpu.VMEM((tm, LANES), F32),
                        pltpu.VMEM((tm, D_MODEL), F32)],
        compiler_params=_params(2),
        name="moe_ln",
    )(h2d, wr, rb, wgu, wd, gain, bias)


def _layer_weights(l, w_in, w_out, pe, w1, b1, w2, lam_p, subln, ln_g, ln_b, rgw, rgb, rew, reb, wg, wu, wd):
    wr = jnp.concatenate([rgw[l], rew[l], jnp.zeros((D_MODEL, LANES - N_GROUPS - N_EXPERTS), F32)], axis=1)
    rb = jnp.concatenate([rgb[l], reb[l], jnp.zeros((LANES - N_GROUPS - N_EXPERTS,), F32)])[None, :]
    wq, wkv = _proj_weights(w_in[l])
    return dict(
        wq=wq, wkv=wkv, wkv_t=wkv.T, wall=jnp.concatenate([wq, wkv], axis=1),
        w_out=w_out[l].astype(BF16),
        cmp=_nsa_cmp_weights(pe[l], w1[l], b1[l], w2[l]),
        lam_p=lam_p[l],
        subln4=jnp.tile(subln[l], HEADS)[None, :],
        lam_init=0.8 - 0.6 * math.exp(-0.3 * l),
        g1=ln_g[l, 0][None, :], b1=ln_b[l, 0][None, :], g2=ln_g[l, 1][None, :], b2=ln_b[l, 1][None, :],
        wr=wr.astype(BF16), rb=rb,
        wgu=jnp.concatenate([wg[l], wu[l]], axis=2).astype(BF16),
        wd=wd[l].astype(BF16),
    )


def _ffn(x, outs, lw, tm, tm_ffn):
    b, t, _ = x.shape
    x2d = x.reshape(b * t, D_MODEL)
    h = _outproj_norm(x2d, [o.reshape(b * t, GROUP_W) for o in outs], lw["w_out"], lw["g1"], lw["b1"], tm)
    y = _moe_norm(h, lw["wr"], lw["rb"], lw["wgu"], lw["wd"], lw["g2"], lw["b2"], tm_ffn)
    return y.reshape(b, t, D_MODEL)


def kernel(x_prompt, x_sample, cache_sb_kv, cache_nsa_kv, cache_diff_kv, cache_moba_kv, state_nsa_win, page_table, w_in, w_out, nsa_cmp_pe, nsa_cmp_w1, nsa_cmp_b1, nsa_cmp_w2, diff_lambda, diff_subln, ln_gain, ln_bias, router_group_w, router_group_b, router_expert_w, router_expert_b, expert_w_gate, expert_w_up, expert_w_down):
    bp, tp, _ = x_prompt.shape
    bs, ts, _ = x_sample.shape
    depth = w_in.shape[0]
    n_phys, page = cache_sb_kv.shape[1], cache_sb_kv.shape[2]
    n_pages = page_table.shape[1]
    past = n_pages * page
    w_eff = state_nsa_win.shape[2]
    l_s = past + ts
    l_s_pad = -(-l_s // MOBA_BLOCK) * MOBA_BLOCK
    tm_p = 256
    tm_s = min(256, bs * ts)
    assert tp % KV_TILE == 0 and tm_p % ts == 0 and (bs * ts) % tm_s == 0 and ts <= page
    assert page == LANES and past % MOBA_BLOCK == 0 and w_eff % LANES == 0 and w_eff == min(NSA_WINDOW, past)

    pool_sb = jnp.transpose(cache_sb_kv, (0, 1, 3, 4, 5, 2)).reshape(depth, n_phys, 2 * GROUP_W, page)
    pool_diff = jnp.transpose(cache_diff_kv, (0, 1, 3, 4, 5, 2)).reshape(depth, n_phys, 2 * GROUP_W, page)
    pool_moba = jnp.transpose(cache_moba_kv, (0, 1, 3, 4, 5, 2)).reshape(depth, n_phys, 2 * GROUP_W, page)
    pool_nsa = jnp.transpose(cache_nsa_kv, (0, 1, 3, 4, 2)).reshape(depth, n_phys, 4 * HD, page)
    win_state = jnp.transpose(state_nsa_win, (0, 1, 3, 4, 2)).reshape(depth, bs, 2 * HD, w_eff)

    tabs_p = _rope_tables(jnp.arange(tp, dtype=jnp.int32))
    tabs_s = _rope_tables(past + (jnp.arange(tm_s, dtype=jnp.int32) % ts))
    tabs_p_t, tabs_s_t = jnp.transpose(tabs_p, (0, 2, 1)), jnp.transpose(tabs_s, (0, 2, 1))
    n_cmp_p, n_sel_p = tp // NSA_STRIDE, tp // NSA_SEL_BLOCK
    n_cmp_s, n_sel_s = l_s_pad // NSA_STRIDE, l_s_pad // NSA_SEL_BLOCK
    ncp = lambda n: -(-n // LANES) * LANES

    y_p, y_s = x_prompt, x_sample
    kv_bufs, new_s = None, []
    for l in range(depth):
        lw = _layer_weights(l, w_in, w_out, nsa_cmp_pe, nsa_cmp_w1, nsa_cmp_b1, nsa_cmp_w2, diff_lambda,
                            diff_subln, ln_gain, ln_bias, router_group_w, router_group_b, router_expert_w,
                            router_expert_b, expert_w_gate, expert_w_up, expert_w_down)
        lam_specs = [pl.BlockSpec((4, DIFF_QK), lambda *a: (0, 0)), pl.BlockSpec((1, GROUP_W), lambda *a: (0, 0))]

        pp = _project(y_p.reshape(bp * tp, D_MODEL), lw["wq"], lw["wkv_t"], tabs_p, tabs_p_t, tm_p,
                      Q_SEGS, KV_SEGS, tp, depth, l, kv_bufs)
        kv_bufs = [pp[s_[0]] for s_ in KV_SEGS]
        q3 = lambda a: a.reshape(bp, tp, a.shape[-1])
        o_sb = _prompt_call(functools.partial(_sb_kernel, tq=256, tk=KV_TILE, pos0=0, chunk=HEADS * 256),
                            q3(pp["sbq"]), pp["sbkv"], l, (), (), 256, "sb_attn",
                            scratch=[pltpu.VMEM((HEADS * 256, GROUP_W), BF16),
                                     pltpu.VMEM((HEADS * 256, GROUP_W), F32),
                                     pltpu.VMEM((HEADS * 256, 1), F32)])
        o_nsa = _prompt_call(
            functools.partial(_nsa_kernel, tq=256, tk=KV_TILE, pos0=0, n_cmp=n_cmp_p, n_sel=n_sel_p),
            q3(pp["nq"]), pp["nkv"], l, (q3(pp["gate"]), pp["wkv"]) + lw["cmp"],
            [pl.BlockSpec((1, 256, LANES), lambda bb, i: (bb, i, 0)),
             pl.BlockSpec((None, 1, 2 * HD, tp), lambda bb, i: (l, bb, 0, 0))] + _cmp_specs(None), 256, "nsa_attn",
            scratch=[pltpu.VMEM((tp, LANES), F32), pltpu.VMEM((ncp(n_cmp_p), GROUP_W), F32),
                     pltpu.VMEM((ncp(n_cmp_p), GROUP_W), F32)])
        o_diff = _prompt_call(
            functools.partial(_diff_kernel, tq=128, tk=KV_TILE, pos0=0, lam_init=lw["lam_init"]),
            q3(pp["dq"]), pp["dkv"], l, (lw["lam_p"], lw["subln4"]), lam_specs, 128, "diff_attn")
        o_moba = _prompt_call(
            functools.partial(_moba_kernel, tq=256, tk=KV_TILE, pos0=0, n_blocks=tp // MOBA_BLOCK),
            q3(pp["mq"]), pp["mkv"], l, (), (), 256, "moba_attn", scratch=[pltpu.VMEM((LANES, GROUP_W), F32)])
        y_p = _ffn(y_p, (o_sb, o_nsa, o_diff, o_moba), lw, tm_p, math.gcd(1024, bp * tp))

        ps = _project(y_s.reshape(bs * ts, D_MODEL), lw["wall"], lw["wkv_t"], tabs_s, tabs_s_t, tm_s,
                      Q_SEGS + KV_SEGS, (), ts)
        s3 = lambda a: a.reshape(bs, ts, a.shape[-1])
        new_tile = lambda name: s3(ps[name])
        common = dict(n_pages=n_pages, page=page, past=past)
        ctx_scratch = lambda ch: [((ch, past + page), BF16), ((ch, past + page), BF16)]
        o_sb = _sample_call(functools.partial(_sb_sample_kernel, **common), page_table, s3(ps["sbq"]),
                            pool_sb, l, new_tile("sbkv"), (), (), (), "sb_attn_s", scratch=ctx_scratch(GROUP_W))
        o_nsa = _sample_call(
            functools.partial(_nsa_sample_kernel, n_cmp=n_cmp_s, n_sel=n_sel_s, w_eff=w_eff, **common),
            page_table, s3(ps["nq"]), pool_nsa, l, new_tile("nkv"),
            (s3(ps["gate"]), win_state, new_tile("wkv")) + lw["cmp"],
            [pl.BlockSpec((NSA_SAMPLES_PER_STEP, ts, LANES), lambda bb, pt: (bb, 0, 0)),
             pl.BlockSpec((1, NSA_SAMPLES_PER_STEP, 2 * HD, w_eff), lambda bb, pt: (l, bb, 0, 0)),
             pl.BlockSpec((NSA_SAMPLES_PER_STEP, ts, 2 * HD), lambda bb, pt: (bb, 0, 0))] + _cmp_specs(None),
            (0, 1, 0, None, None, None, None, None), "nsa_attn_s",
            scratch=[((l_s_pad, LANES), F32), ((ncp(n_cmp_s), GROUP_W), F32), ((ncp(n_cmp_s), GROUP_W), F32)]
            + ctx_scratch(HD) + [((HD, w_eff + page), BF16), ((HD, w_eff + page), BF16)],
            spb=NSA_SAMPLES_PER_STEP, staged=True)
        o_diff = _sample_call(functools.partial(_diff_sample_kernel, lam_init=lw["lam_init"], **common),
                              page_table, s3(ps["dq"]), pool_diff, l, new_tile("dkv"),
                              (lw["lam_p"], lw["subln4"]), lam_specs, (None, None), "diff_attn_s",
                              scratch=ctx_scratch(GROUP_W))
        o_moba = _sample_call(functools.partial(_moba_sample_kernel, **common), page_table, s3(ps["mq"]),
                              pool_moba, l, new_tile("mkv"), (), (), (), "moba_attn_s",
                              scratch=ctx_scratch(GROUP_W))
        y_s = _ffn(y_s, (o_sb, o_nsa, o_diff, o_moba), lw, tm_s, math.gcd(1024, bs * ts))
        win_new = jnp.concatenate([win_state[l][:, :, ts:], jnp.transpose(s3(ps["wkv"]), (0, 2, 1))], axis=2)
        new_s.append((ps["sbkv"], ps["nkv"], ps["dkv"], ps["mkv"], win_new))

    def heads_out(a, b, t):
        return jnp.transpose(a.reshape(depth, b, 2, HEADS, HD, t), (0, 1, 5, 2, 3, 4))

    def slots_out(a, b, t, n):
        return jnp.transpose(a.reshape(depth, b, n, HD, t), (0, 1, 4, 2, 3))

    st_p = [pp["sbkv"], pp["nkv"], pp["dkv"], pp["mkv"], pp["wkv"][:, :, :, tp - min(NSA_WINDOW, tp):]]
    st_s = [jnp.stack([r[k] for r in new_s]) for k in range(5)]
    outs_p = (heads_out(st_p[0], bp, tp), slots_out(st_p[1], bp, tp, 4), heads_out(st_p[2], bp, tp),
              heads_out(st_p[3], bp, tp), slots_out(st_p[4], bp, min(NSA_WINDOW, tp), 2))
    outs_s = (st_s[0].reshape(depth, bs, ts, 2, HEADS, HD), st_s[1].reshape(depth, bs, ts, 4, HD),
              st_s[2].reshape(depth, bs, ts, 2, HEADS, HD), st_s[3].reshape(depth, bs, ts, 2, HEADS, HD),
              slots_out(st_s[4], bs, w_eff, 2))
    return (y_p, y_s, *outs_p, *outs_s)
```

```python
import functools
import math

import numpy as np
import jax
import jax.numpy as jnp
from jax import lax
from jax.experimental import pallas as pl
from jax.experimental.pallas import tpu as pltpu

F32 = jnp.float32
BF16 = jnp.bfloat16

D_MODEL = 1024
HEADS = 4
HD = 64
GROUP_W = HEADS * HD
DIFF_QK = HD // 2
DEPTH_LAYERS = 2
ROPE_THETA = 500000.0
ROPE_FRACTION = 4
NSA_STRIDE = 16
NSA_CMP_LEN = 32
NSA_CMP_HID = 128
NSA_SEL_BLOCK = 64
NSA_TOPN = 16
NSA_WINDOW = 512
NSA_FORCE = 1.0e4
MOBA_BLOCK = 256
MOBA_TOPK = 3
N_GROUPS = 4
EXPERTS_PER_GROUP = 4
N_EXPERTS = 16
EXPERT_HID = 256
EXPERTS_PER_STEP = 2
ALPHA = (2 * DEPTH_LAYERS) ** 0.25
LN_EPS = 1e-5
RMS_EPS = 1e-5
NEG_BIG = -1e30
LANES = 128
KV_TILE = 256
SAMPLES_PER_STEP = 2
NSA_SAMPLES_PER_STEP = 4
VMEM_LIMIT = 56 * 1024 * 1024

Q_SEGS = (("sbq", 256, (0, 0)), ("nq", 256, (1, 1)), ("dq", 256, (3, 3)), ("mq", 256, (1, 1)),
          ("gate", 128, (4,)))
KV_SEGS = (("sbkv", 512, (0, 0, 0, 0)), ("nkv", 256, (2, 2)), ("wkv", 128, (2,)),
           ("dkv", 512, (3, 3, 0, 0)), ("mkv", 512, (1, 1, 0, 0)))
Q_COLS = sum(s[1] for s in Q_SEGS)
KV_COLS = sum(s[1] for s in KV_SEGS)


def _dot(a, b):
    return jnp.dot(a, b, preferred_element_type=F32)


def _dot_t(a, b):
    return lax.dot_general(a, b, (((1,), (1,)), ((), ())), preferred_element_type=F32)


def _split_dot(a, b_bf):
    hi = a.astype(BF16)
    lo = (a - hi.astype(F32)).astype(BF16)
    return _dot(hi, b_bf) + _dot(lo, b_bf)


def _lane_group_masks(width, group, n):
    lane = lax.broadcasted_iota(jnp.int32, (1, width), 1)
    return [jnp.where((lane // group) == g, 1.0, 0.0).astype(F32) for g in range(n)]


def _stack_groups(q, masks):
    return jnp.concatenate([q * m for m in masks], axis=0)


def _col(x, idx):
    lane = lax.broadcasted_iota(jnp.int32, x.shape, 1)
    return jnp.sum(jnp.where(lane == idx, x, 0.0), axis=-1, keepdims=True)


def _params(n_axes):
    return pltpu.CompilerParams(dimension_semantics=("arbitrary",) * n_axes, vmem_limit_bytes=VMEM_LIMIT)


def _rope_rows(y, tab_ref, kind):
    half = 4 if kind == 3 else 8
    t0 = (kind - 1) * 3
    return (y * tab_ref[t0] + pltpu.roll(y, LANES - half, 1) * tab_ref[t0 + 1]
            + pltpu.roll(y, half, 1) * tab_ref[t0 + 2])


def _rope_cols(y, tab_ref, kind):
    half = 4 if kind == 3 else 8
    t0 = (kind - 1) * 3
    return (y * tab_ref[t0] + pltpu.roll(y, LANES - half, 0) * tab_ref[t0 + 1]
            + pltpu.roll(y, half, 0) * tab_ref[t0 + 2])


def _proj_kernel(x_ref, w_ref, wt_ref, tab_ref, tabt_ref, *out_refs, row_segs, col_segs, n_carried):
    out_refs = out_refs[n_carried:]
    xb = x_ref[...].astype(BF16)
    off = 0
    n_row = len(row_segs)
    for (_, width, kinds), o_ref in zip(row_segs, out_refs[:n_row]):
        for c0 in range(0, width, 256):
            cw = min(256, width - c0)
            y2 = _dot(xb, w_ref[:, off + c0:off + c0 + cw])
            for cc in range(cw // LANES):
                y = y2[:, cc * LANES:(cc + 1) * LANES]
                kind = kinds[(c0 // LANES) + cc]
                if kind in (1, 2, 3):
                    y = _rope_rows(y, tab_ref, kind)
                elif kind == 4:
                    y = 1.0 / (1.0 + jnp.exp(-y))
                o_ref[:, c0 + cc * LANES:c0 + (cc + 1) * LANES] = y
        off += width
    off = 0
    for (_, width, kinds), o_ref in zip(col_segs, out_refs[n_row:]):
        for c0 in range(0, width, 256):
            cw = min(256, width - c0)
            y2 = _dot_t(wt_ref[off + c0:off + c0 + cw, :], xb)
            for cc in range(cw // LANES):
                y = y2[cc * LANES:(cc + 1) * LANES, :]
                kind = kinds[(c0 // LANES) + cc]
                if kind in (1, 2, 3):
                    y = _rope_cols(y, tabt_ref, kind)
                o_ref[0, c0 + cc * LANES:c0 + (cc + 1) * LANES, :] = y
        off += width


def _project(x2d, w_rows, w_cols, tabs, tabs_t, tm, row_segs, col_segs, seq, depth=1, layer=0, carried=None):
    n = x2d.shape[0]
    carried = list(carried or [])
    n_tab = tabs.shape[1] // tm
    per_seq = max(seq // tm, 1)
    in_specs = [pl.BlockSpec((tm, D_MODEL), lambda r: (r, 0)),
                pl.BlockSpec(w_rows.shape, lambda r: (0, 0)),
                pl.BlockSpec(w_cols.shape, lambda r: (0, 0)),
                pl.BlockSpec((9, tm, LANES), lambda r: (0, r % n_tab, 0)),
                pl.BlockSpec((9, LANES, tm), lambda r: (0, 0, r % n_tab))]
    out_specs = [pl.BlockSpec((tm, s[1]), lambda r: (r, 0)) for s in row_segs]
    in_specs += [pl.BlockSpec(memory_space=pl.ANY)] * len(carried)
    out_specs += [pl.BlockSpec((None, 1, s[1], tm), lambda r: (layer, r // per_seq, 0, r % per_seq))
                  for s in col_segs]
    out_shape = [jax.ShapeDtypeStruct((n, s[1]), F32) for s in row_segs]
    out_shape += [jax.ShapeDtypeStruct((depth, n // seq, s[1], seq), F32) for s in col_segs]
    outs = pl.pallas_call(
        functools.partial(_proj_kernel, row_segs=row_segs, col_segs=col_segs, n_carried=len(carried)),
        grid=(n // tm,),
        in_specs=in_specs, out_specs=out_specs, out_shape=out_shape,
        input_output_aliases={5 + k: len(row_segs) + k for k in range(len(carried))},
        compiler_params=_params(1),
        name="proj",
    )(x2d, w_rows, w_cols, tabs, tabs_t, *carried)
    return dict(zip([s[0] for s in row_segs + col_segs], outs))


def _rope_tables(pos):
    pos = pos.astype(F32)
    lane = np.arange(LANES)
    tabs = []
    for dim, first_only in ((HD, False), (HD, True), (DIFF_QK, False)):
        rot = dim // ROPE_FRACTION
        half = rot // 2
        inv = jnp.power(jnp.float32(ROPE_THETA), -jnp.arange(half, dtype=F32) * (2.0 / rot))
        ang = pos[:, None] * inv[None, :]
        cos, sin = jnp.cos(ang), jnp.sin(ang)
        c = lane % dim
        in_rot = c < rot
        if first_only:
            in_rot = in_rot & (lane < HD)
        first = in_rot & (c < half)
        second = in_rot & (c >= half)
        fidx = c % half
        cos_l, sin_l = cos[:, fidx], sin[:, fidx]
        tabs.append(jnp.where(in_rot[None, :], cos_l, 1.0))
        tabs.append(jnp.where(first[None, :], -sin_l, 0.0))
        tabs.append(jnp.where(second[None, :], sin_l, 0.0))
    return jnp.stack(tabs).astype(F32)


def _proj_weights(w_in_l):
    o = [int(v) for v in np.cumsum((0, 256, 256, 256, 256, 256, 128, 12, 256, 256, 256, 256, 256, 256))]
    sl = lambda a, b: w_in_l[:, a:b]
    gate = jnp.pad(sl(o[6], o[7]), ((0, 0), (0, LANES - 12)))
    wq = jnp.concatenate([sl(o[0], o[1]), sl(o[3], o[4]), sl(o[7], o[8]), sl(o[10], o[11]), gate], axis=1)
    wkv = jnp.concatenate([sl(o[1], o[3]), sl(o[4], o[5]), sl(o[5], o[6]), sl(o[8], o[10]), sl(o[11], o[13])],
                          axis=1)
    return wq.astype(BF16), wkv.astype(BF16)


def _qpos(p_lo, tq, rows):
    r = lax.broadcasted_iota(jnp.int32, (rows, 1), 0)
    return p_lo + (r & (tq - 1))


def _kpos(kpos0, tk, rows):
    return kpos0 + lax.broadcasted_iota(jnp.int32, (rows, tk), 1)


def _flash_step(qb, kt, vt, mask, scale, carry):
    m, l, acc = carry
    s = _dot(qb, kt) * scale
    if mask is not None:
        s = jnp.where(mask, s, NEG_BIG)
    m_new = jnp.maximum(m, jnp.max(s, axis=-1, keepdims=True))
    a = jnp.exp(m - m_new)
    p = jnp.exp(s - m_new)
    l = a * l + jnp.sum(p, axis=-1, keepdims=True)
    acc = a * acc + _dot_t(p.astype(BF16), vt)
    return m_new, l, acc


def _flash_init(rows):
    return (jnp.full((rows, 1), NEG_BIG, F32), jnp.zeros((rows, 1), F32), jnp.zeros((rows, GROUP_W), F32))


def _unstack_heads(rows_val, tq, masks):
    out = rows_val[0:tq] * masks[0]
    for h in range(1, HEADS):
        out = out + rows_val[h * tq:(h + 1) * tq] * masks[h]
    return out


def _later_matrix(tk):
    r_i = lax.broadcasted_iota(jnp.int32, (tk, tk), 0)
    c_i = lax.broadcasted_iota(jnp.int32, (tk, tk), 1)
    return jnp.where(r_i > c_i, 1.0, 0.0).astype(BF16)


def _tile_from_ref(ref, idx, j, tk):
    cols = pl.ds(pl.multiple_of(j * tk, tk), tk)
    return (ref[idx + (slice(0, GROUP_W), cols)].astype(BF16),
            ref[idx + (slice(GROUP_W, 2 * GROUP_W), cols)].astype(BF16))


def _page_tile(ref):
    idx = (0,) * (len(ref.shape) - 2)
    return (ref[idx + (slice(0, GROUP_W), slice(None))].astype(BF16),
            ref[idx + (slice(GROUP_W, 2 * GROUP_W), slice(None))].astype(BF16))


def _rows_to_tile(rows, page):
    t, w = rows.shape
    return jnp.concatenate([rows, jnp.zeros((page - t, w), F32)], axis=0).T.astype(BF16)


def _gather_context(page_refs, new_ref, kt_ref, vt_ref, page):
    for p, ref in enumerate(page_refs):
        kt, vt = _page_tile(ref)
        kt_ref[:, p * page:(p + 1) * page] = kt
        vt_ref[:, p * page:(p + 1) * page] = vt
    new_t = _rows_to_tile(new_ref[0], page)
    p = len(page_refs)
    kt_ref[:, p * page:(p + 1) * page] = new_t[0:GROUP_W]
    vt_ref[:, p * page:(p + 1) * page] = new_t[GROUP_W:2 * GROUP_W]


def _softmax_context(s, mask, vt):
    s = jnp.where(mask, s, NEG_BIG)
    m = jnp.max(s, axis=-1, keepdims=True)
    p = jnp.exp(s - m)
    return m, jnp.sum(p, axis=-1, keepdims=True), _dot_t(p.astype(BF16), vt)


def _sb_tile(qb_ref, acc_ref, tail_ref, kt, vt, later_mat, kpos0, p_lo, tq, causal, chunk):
    rows, tk = qb_ref.shape[0], kt.shape[1]
    for r0 in range(0, rows, chunk):
        rs = slice(r0, r0 + chunk)
        z = _dot(qb_ref[rs, :], kt)
        log_sig = jnp.minimum(z, 0.0) - jnp.log(1.0 + jnp.exp(-jnp.abs(z)))
        log_fail = log_sig - z
        if causal:
            qpos = p_lo + ((r0 + lax.broadcasted_iota(jnp.int32, (chunk, 1), 0)) & (tq - 1))
            mask = _kpos(kpos0, tk, chunk) < qpos
            log_fail = jnp.where(mask, log_fail, 0.0)
        later = _split_dot(log_fail, later_mat) + tail_ref[rs, :]
        w = jnp.exp(log_sig + later)
        if causal:
            w = jnp.where(mask, w, 0.0)
        acc_ref[rs, :] += _dot_t(w.astype(BF16), vt)
        tail_ref[rs, :] += jnp.sum(log_fail, axis=-1, keepdims=True)


def _sb_kernel(q_ref, kv_ref, o_ref, qb_ref, acc_ref, tail_ref, *, tq, tk, pos0, chunk):
    i = pl.program_id(1)
    p_lo = pos0 + i * tq
    masks = _lane_group_masks(GROUP_W, HD, HEADS)
    qb_ref[...] = _stack_groups(q_ref[0] * (HD ** -0.5), masks).astype(BF16)
    acc_ref[...] = jnp.zeros_like(acc_ref)
    tail_ref[...] = jnp.zeros_like(tail_ref)
    later_mat = _later_matrix(tk)
    j_own = (p_lo + tq - 1) // tk
    kt, vt = _tile_from_ref(kv_ref, (0,), j_own, tk)
    _sb_tile(qb_ref, acc_ref, tail_ref, kt, vt, later_mat, j_own * tk, p_lo, tq, True, chunk)

    def body(jj, _):
        j = j_own - 1 - jj
        kt, vt = _tile_from_ref(kv_ref, (0,), j, tk)
        _sb_tile(qb_ref, acc_ref, tail_ref, kt, vt, later_mat, j * tk, p_lo, tq, False, chunk)
        return 0

    lax.fori_loop(0, j_own, body, 0)
    o_ref[0] = _unstack_heads(acc_ref[...], tq, masks)


def _sb_sample_kernel(pt_ref, q_ref, *refs, n_pages, page, past):
    page_refs, new_ref, o_ref = refs[:n_pages], refs[n_pages], refs[n_pages + 1]
    tq = q_ref.shape[1]
    rows = HEADS * tq
    masks = _lane_group_masks(GROUP_W, HD, HEADS)
    kt_ref, vt_ref = refs[n_pages + 2:]
    _gather_context(page_refs, new_ref, kt_ref, vt_ref, page)
    qb = _stack_groups(q_ref[0] * (HD ** -0.5), masks).astype(BF16)
    qpos = _qpos(past, tq, rows)
    later_mat = _later_matrix(page)
    ctx = kt_ref.shape[1]
    z = _dot(qb, kt_ref[...])
    mask = _kpos(0, ctx, rows) < qpos
    log_sig = jnp.minimum(z, 0.0) - jnp.log(1.0 + jnp.exp(-jnp.abs(z)))
    log_fail = jnp.where(mask, log_sig - z, 0.0)
    tail = jnp.zeros((rows, 1), F32)
    later = [None] * (n_pages + 1)
    for p in reversed(range(n_pages + 1)):
        blk = log_fail[:, p * page:(p + 1) * page]
        later[p] = _split_dot(blk, later_mat) + tail
        tail = tail + jnp.sum(blk, axis=-1, keepdims=True)
    w = jnp.where(mask, jnp.exp(log_sig + jnp.concatenate(later, axis=1)), 0.0)
    o_ref[0] = _unstack_heads(_dot_t(w.astype(BF16), vt_ref[...]), tq, masks)


def _prompt_call(kern, q, kv_t, layer, extra, extra_specs, tq, name, scratch=()):
    b, t, _ = q.shape
    rows_kv, l = kv_t.shape[2], kv_t.shape[3]
    return pl.pallas_call(
        kern,
        grid=(b, t // tq),
        in_specs=[pl.BlockSpec((1, tq, GROUP_W), lambda bb, i: (bb, i, 0)),
                  pl.BlockSpec((None, 1, rows_kv, l), lambda bb, i: (layer, bb, 0, 0))] + list(extra_specs),
        out_specs=pl.BlockSpec((1, tq, GROUP_W), lambda bb, i: (bb, i, 0)),
        out_shape=jax.ShapeDtypeStruct((b, t, GROUP_W), F32),
        scratch_shapes=list(scratch),
        compiler_params=_params(2),
        name=name,
    )(q, kv_t, *extra)


def _sample_call(body, page_table, q, pool_t, layer, new_t, extra, extra_specs, extra_axes, name, scratch=(),
                 spb=SAMPLES_PER_STEP, staged=False):
    b, tq, _ = q.shape
    n_pages = page_table.shape[1]
    rows_kv, page = pool_t.shape[2], pool_t.shape[3]
    assert b % spb == 0

    def page_spec(s_, p):
        return pl.BlockSpec((1, 1, rows_kv, page), lambda bb, pt: (layer, pt[(bb * spb + s_) * n_pages + p], 0, 0))

    axes = [0] + [None] * (spb * n_pages) + [0] + list(extra_axes) + [0]
    n_scratch = len(scratch)

    def kern(pt_ref, *refs):
        io, scr = refs[:len(refs) - n_scratch], refs[len(refs) - n_scratch:]
        per_sample = []
        for s_ in range(spb):
            views = []
            for k, (r, ax) in enumerate(zip(io, axes)):
                if 1 <= k <= spb * n_pages:
                    if (k - 1) // n_pages == s_:
                        views.append(r)
                elif ax is None:
                    views.append(r)
                else:
                    views.append(r.at[(slice(None),) * ax + (pl.ds(s_, 1),)])
            per_sample.append(views + [r.at[s_] for r in scr])
        if staged:
            body(pt_ref, per_sample)
        else:
            for refs_s in per_sample:
                body(pt_ref, *refs_s)

    in_specs = ([pl.BlockSpec((spb, tq, GROUP_W), lambda bb, pt: (bb, 0, 0))]
                + [page_spec(s_, p) for s_ in range(spb) for p in range(n_pages)]
                + [pl.BlockSpec((spb, tq, rows_kv), lambda bb, pt: (bb, 0, 0))] + list(extra_specs))
    return pl.pallas_call(
        kern,
        grid_spec=pltpu.PrefetchScalarGridSpec(
            num_scalar_prefetch=1, grid=(b // spb,), in_specs=in_specs,
            out_specs=pl.BlockSpec((spb, tq, GROUP_W), lambda bb, pt: (bb, 0, 0)),
            scratch_shapes=[pltpu.VMEM((spb,) + shape, dt) for shape, dt in scratch]),
        out_shape=jax.ShapeDtypeStruct((b, tq, GROUP_W), F32),
        compiler_params=_params(1),
        name=name,
    )(page_table.reshape(-1), q, *([pool_t] * (spb * n_pages)), new_t, *extra)


def _diff_finish(carry, tq, lam_ref, sub_ref, lam_init):
    _, l, acc = carry
    head_masks = _lane_group_masks(GROUP_W, HD, HEADS)
    o = acc / l
    lp = lam_ref[...]
    lam = (jnp.exp(jnp.sum(lp[0:1] * lp[1:2], axis=-1, keepdims=True))
           - jnp.exp(jnp.sum(lp[2:3] * lp[3:4], axis=-1, keepdims=True)) + lam_init)
    out = jnp.zeros((tq, GROUP_W), F32)
    for h in range(HEADS):
        oh = (o[(2 * h) * tq:(2 * h + 1) * tq] - lam * o[(2 * h + 1) * tq:(2 * h + 2) * tq]) * head_masks[h]
        ms = jnp.sum(oh * oh, axis=-1, keepdims=True) * (1.0 / HD)
        out = out + oh * lax.rsqrt(ms + RMS_EPS)
    return out * sub_ref[...] * (1.0 - lam_init)


def _diff_kernel(q_ref, kv_ref, lam_ref, sub_ref, o_ref, *, tq, tk, pos0, lam_init):
    i = pl.program_id(1)
    p_lo = pos0 + i * tq
    rows = 2 * HEADS * tq
    qb = _stack_groups(q_ref[0], _lane_group_masks(GROUP_W, DIFF_QK, 2 * HEADS)).astype(BF16)
    qpos = _qpos(p_lo, tq, rows)
    j_own = (p_lo + tq - 1) // tk
    kt, vt = _tile_from_ref(kv_ref, (0,), j_own, tk)
    carry = _flash_step(qb, kt, vt, _kpos(j_own * tk, tk, rows) <= qpos, DIFF_QK ** -0.5, _flash_init(rows))

    def body(width):
        def step(j, carry):
            kt, vt = _tile_from_ref(kv_ref, (0,), j, width)
            return _flash_step(qb, kt, vt, None, DIFF_QK ** -0.5, carry)
        return step

    carry = lax.fori_loop(0, j_own // 2, body(2 * tk), carry)
    carry = lax.fori_loop(j_own - j_own % 2, j_own, body(tk), carry)
    o_ref[0] = _diff_finish(carry, tq, lam_ref, sub_ref, lam_init)


def _diff_sample_kernel(pt_ref, q_ref, *refs, n_pages, page, past, lam_init):
    page_refs, new_ref = refs[:n_pages], refs[n_pages]
    lam_ref, sub_ref, o_ref = refs[n_pages + 1:n_pages + 4]
    tq = q_ref.shape[1]
    rows = 2 * HEADS * tq
    qb = _stack_groups(q_ref[0], _lane_group_masks(GROUP_W, DIFF_QK, 2 * HEADS)).astype(BF16)
    qpos = _qpos(past, tq, rows)
    kt_ref, vt_ref = refs[n_pages + 4:]
    _gather_context(page_refs, new_ref, kt_ref, vt_ref, page)
    s = _dot(qb, kt_ref[...]) * (DIFF_QK ** -0.5)
    mask = _kpos(0, kt_ref.shape[1], rows) <= qpos
    carry = _softmax_context(s, mask, vt_ref[...])
    o_ref[0] = _diff_finish(carry, tq, lam_ref, sub_ref, lam_init)


def _topk_mask_t(score_t, k):
    row = lax.broadcasted_iota(jnp.int32, score_t.shape, 0)
    rank = jnp.zeros(score_t.shape, F32)
    for m in range(score_t.shape[0]):
        sm = score_t[m:m + 1, :]
        tie = jnp.where(row > m, 1.0, 0.0)
        rank = rank + jnp.where(sm > score_t, 1.0, jnp.where(sm == score_t, tie, 0.0))
    return jnp.where(rank < k, 1.0, 0.0)


def _rows_from_t(mask_t):
    n, cols = mask_t.shape
    return jnp.concatenate([mask_t, jnp.zeros((LANES - n, cols), F32)], axis=0).T


def _moba_select(qb, km_t, own, n_blocks):
    gate = _dot(qb, km_t.astype(BF16))
    rows = qb.shape[0]
    lane = lax.broadcasted_iota(jnp.int32, (rows, LANES), 1)
    rank = jnp.zeros((rows, LANES), F32)
    for m in range(n_blocks):
        gm = gate[:, m:m + 1]
        beats = (gm > gate) | ((gm == gate) & (m < lane))
        rank = rank + jnp.where(beats & (m < own), 1.0, 0.0)
    return jnp.where((rank < MOBA_TOPK) & (lane < own), 1.0, 0.0)


def _block_mean_col(kt_f32, m):
    lane = lax.broadcasted_iota(jnp.int32, (GROUP_W, LANES), 1)
    return jnp.where(lane == m, jnp.sum(kt_f32, axis=-1, keepdims=True), 0.0)


def _moba_select_t(qb, km_rows, own):
    gate_t = _dot_t(km_rows.astype(BF16), qb)
    past = lax.broadcasted_iota(jnp.int32, gate_t.shape, 0) < own
    pick_t = _topk_mask_t(jnp.where(past, gate_t, -jnp.inf), MOBA_TOPK)
    return _rows_from_t(jnp.where(past, pick_t, 0.0))


def _moba_kernel(q_ref, kv_ref, o_ref, km_ref, *, tq, tk, pos0, n_blocks):
    i = pl.program_id(1)
    p_lo = pos0 + i * tq
    rows = HEADS * tq
    masks = _lane_group_masks(GROUP_W, HD, HEADS)

    @pl.when(i == 0)
    def _():
        km = jnp.zeros((GROUP_W, LANES), F32)
        for m in range(n_blocks):
            km = km + _block_mean_col(kv_ref[0, 0:GROUP_W, m * MOBA_BLOCK:(m + 1) * MOBA_BLOCK], m)
        km_ref[...] = (km * (1.0 / MOBA_BLOCK)).T

    qb = _stack_groups(q_ref[0], masks).astype(BF16)
    qpos = _qpos(p_lo, tq, rows)
    own = p_lo // MOBA_BLOCK
    sel = _moba_select_t(qb, km_ref[0:16, :], own)

    kt, vt = _tile_from_ref(kv_ref, (0,), own, tk)
    carry = _flash_step(qb, kt, vt, _kpos(own * tk, tk, rows) <= qpos, HD ** -0.5, _flash_init(rows))

    def body(n):
        def step(j, carry):
            kt, vt = _tile_from_ref(kv_ref, (0,), j, n * tk)
            picks = [jnp.broadcast_to(_col(sel, n * j + k), (rows, tk)) for k in range(n)]
            mask = (picks[0] if n == 1 else jnp.concatenate(picks, axis=1)) > 0.5
            return _flash_step(qb, kt, vt, mask, HD ** -0.5, carry)
        return step

    carry = lax.fori_loop(0, own // 2, body(2), carry)
    _, l, acc = lax.fori_loop(own - own % 2, own, body(1), carry)
    o_ref[0] = _unstack_heads(acc / l, tq, masks)


def _moba_sample_kernel(pt_ref, q_ref, *refs, n_pages, page, past):
    page_refs, new_ref, o_ref = refs[:n_pages], refs[n_pages], refs[n_pages + 1]
    tq = q_ref.shape[1]
    rows = HEADS * tq
    per_blk = MOBA_BLOCK // page
    own = past // MOBA_BLOCK
    masks = _lane_group_masks(GROUP_W, HD, HEADS)
    km = jnp.zeros((GROUP_W, LANES), F32)
    for p in range(n_pages):
        km = km + _block_mean_col(page_refs[p][0, 0, 0:GROUP_W, :], p // per_blk)
    km = km * (1.0 / MOBA_BLOCK)
    qb = _stack_groups(q_ref[0], masks).astype(BF16)
    qpos = _qpos(past, tq, rows)
    sel = _moba_select(qb, km, own, own)
    kt_ref, vt_ref = refs[n_pages + 2:]
    _gather_context(page_refs, new_ref, kt_ref, vt_ref, page)
    s = _dot(qb, kt_ref[...]) * (HD ** -0.5)
    ctx = kt_ref.shape[1]
    picked = [jnp.broadcast_to(sel[:, m:m + 1], (rows, MOBA_BLOCK)) for m in range(own)]
    picked.append(jnp.where(_kpos(past, ctx - past, rows) <= qpos, 1.0, 0.0))
    _, l, acc = _softmax_context(s, jnp.concatenate(picked, axis=1) > 0.5, vt_ref[...])
    o_ref[0] = _unstack_heads(acc / l, tq, masks)


def _gelu_tanh(x):
    return 0.5 * x * (1.0 + jnp.tanh(math.sqrt(2.0 / math.pi) * (x + 0.044715 * (x * x * x))))


def _phase_rows(tile_t):
    r = lax.broadcasted_iota(jnp.int32, (LANES, LANES), 0)
    c = lax.broadcasted_iota(jnp.int32, (LANES, LANES), 1)
    perm = jnp.where(c == NSA_STRIDE * (r % 8) + r // 8, 1.0, 0.0).astype(BF16)
    hi = tile_t.astype(BF16)
    lo = (tile_t - hi.astype(F32)).astype(BF16)
    return _dot_t(perm, hi) + _dot_t(perm, lo)


def _nsa_compress(rows_refs, w1_ref, b1_ref, pe_ref, w2k_ref, w2v_ref, ck_refs, cv_refs, n_cmp):
    row = lax.broadcasted_iota(jnp.int32, (n_cmp, 1), 0)
    tokens = []
    for rows_ref in rows_refs:
        first, second = [], []
        for l in range(NSA_STRIDE):
            x = jnp.concatenate([rows_ref[t * LANES + l * 8:t * LANES + l * 8 + 8, :] for t in range(n_cmp // 8)],
                                axis=0)
            xs = jnp.where(row == n_cmp - 1, 0.0, pltpu.roll(x, n_cmp - 1, 0))
            first.append((x + pe_ref[l:l + 1, :]).astype(BF16))
            second.append((xs + pe_ref[l + NSA_STRIDE:l + NSA_STRIDE + 1, :]).astype(BF16))
        tokens.append(jnp.concatenate(first + second, axis=1))
    hid = _dot(jnp.concatenate(tokens, axis=0), w1_ref[...].reshape(NSA_CMP_LEN * LANES, 2 * NSA_CMP_HID))
    hid = _gelu_tanh(hid + b1_ref[...]).astype(BF16)
    ck, cv = _dot(hid, w2k_ref[...]), _dot(hid, w2v_ref[...])
    for k, (ck_ref, cv_ref) in enumerate(zip(ck_refs, cv_refs)):
        ck_ref[...] = jnp.zeros_like(ck_ref)
        cv_ref[...] = jnp.zeros_like(cv_ref)
        ck_ref[0:n_cmp, :] = ck[k * n_cmp:(k + 1) * n_cmp]
        cv_ref[0:n_cmp, :] = cv[k * n_cmp:(k + 1) * n_cmp]


def _nsa_select_t(p_sum, p_first, n_sel):
    tq, ncp = p_sum.shape
    n_rows = -(-n_sel // 8) * 8
    gi = lax.broadcasted_iota(jnp.int32, (n_rows, ncp), 0)
    gj = lax.broadcasted_iota(jnp.int32, (n_rows, ncp), 1) // (NSA_SEL_BLOCK // NSA_STRIDE)
    group_t = jnp.where(gi == gj, 1.0, 0.0).astype(BF16)
    hi = p_sum.astype(BF16)
    lo = (p_sum - hi.astype(F32)).astype(BF16)
    imp_t = _dot_t(group_t, hi) + _dot_t(group_t, lo)
    row = lax.broadcasted_iota(jnp.int32, (n_rows, tq), 0)
    qblk = (p_first + lax.broadcasted_iota(jnp.int32, (n_rows, tq), 1)) // NSA_SEL_BLOCK
    valid = (row <= qblk) & (row < n_sel)
    forced = (row == 0) | (row == qblk) | (row == qblk - 1)
    score_t = jnp.where(valid, jnp.where(forced, NSA_FORCE, imp_t), -jnp.inf)
    pick_t = _topk_mask_t(score_t, min(NSA_TOPN, n_sel))
    return _rows_from_t(jnp.where(valid, pick_t, 0.0))


def _nsa_cmp_and_select(qb, qpos, ck_ref, cv_ref, tq, n_sel):
    rows = qb.shape[0]
    ncp = ck_ref.shape[0]
    s = _dot_t(qb, ck_ref[...].astype(BF16)) * (HD ** -0.5)
    c_end = lax.broadcasted_iota(jnp.int32, (rows, ncp), 1) * NSA_STRIDE + (NSA_CMP_LEN - 1)
    c_mask = c_end <= qpos
    s = jnp.where(c_mask, s, NEG_BIG)
    p = jnp.where(c_mask, jnp.exp(s - jnp.max(s, axis=-1, keepdims=True)), 0.0)
    den = jnp.sum(p, axis=-1, keepdims=True)
    p = p / jnp.where(den > 0, den, 1.0)
    o_cmp = _dot(p.astype(BF16), cv_ref[...].astype(BF16))
    p_sum = p[0:tq]
    for h in range(1, HEADS):
        p_sum = p_sum + p[h * tq:(h + 1) * tq]
    if tq % LANES == 0:
        return o_cmp, _nsa_select_t(p_sum, qpos[0:1, :], n_sel)
    gi = lax.broadcasted_iota(jnp.int32, (ncp, LANES), 0)
    gj = lax.broadcasted_iota(jnp.int32, (ncp, LANES), 1)
    group_mat = jnp.where((gi // (NSA_SEL_BLOCK // NSA_STRIDE)) == gj, 1.0, 0.0).astype(BF16)
    imp = _split_dot(p_sum, group_mat)
    qblk = qpos[0:tq] // NSA_SEL_BLOCK
    lane = lax.broadcasted_iota(jnp.int32, (tq, LANES), 1)
    valid = (lane <= qblk) & (lane < n_sel)
    forced = (lane == 0) | (lane == qblk) | (lane == qblk - 1)
    score = jnp.where(valid, jnp.where(forced, NSA_FORCE, imp), -jnp.inf)
    rank = jnp.zeros((tq, LANES), F32)
    for m in range(n_sel):
        sm = score[:, m:m + 1]
        beats = (sm > score) | ((sm == score) & (m < lane))
        rank = rank + jnp.where(beats, 1.0, 0.0)
    sel = jnp.where((rank < min(NSA_TOPN, n_sel)) & valid, 1.0, 0.0)
    return o_cmp, sel


def _shared_tile(k64, v64):
    return (jnp.concatenate([k64] * HEADS, axis=0).astype(BF16),
            jnp.concatenate([v64] * HEADS, axis=0).astype(BF16))


def _slc_mask(sel_bf, blk0, kpos0, tk, qpos, rows, causal):
    ei = lax.broadcasted_iota(jnp.int32, (LANES, tk), 0)
    ej = lax.broadcasted_iota(jnp.int32, (LANES, tk), 1) // NSA_SEL_BLOCK
    expand = jnp.where(ei == ej + blk0, 1.0, 0.0).astype(BF16)
    picked = _dot(sel_bf, expand)
    picked = jnp.concatenate([picked] * HEADS, axis=0) > 0.5
    return picked & (_kpos(kpos0, tk, rows) <= qpos) if causal else picked


def _win_mask(kpos0, tk, qpos, rows):
    kp = _kpos(kpos0, tk, rows)
    return (kp <= qpos) & (kp > qpos - NSA_WINDOW)


def _nsa_combine(g, o_cmp, slc, win, tq, masks):
    o_slc = slc[2] / slc[1]
    o_win = win[2] / win[1]
    out = jnp.zeros((tq, GROUP_W), F32)
    for h in range(HEADS):
        r0, r1 = h * tq, (h + 1) * tq
        mix = (g[:, 3 * h:3 * h + 1] * o_cmp[r0:r1] + g[:, 3 * h + 1:3 * h + 2] * o_slc[r0:r1]
               + g[:, 3 * h + 2:3 * h + 3] * o_win[r0:r1])
        out = out + mix * masks[h]
    return out


def _nsa_kernel(q_ref, nkv_ref, g_ref, wkv_ref, w1_ref, b1_ref, pe_ref, w2k_ref, w2v_ref,
                o_ref, rows_ref, ck_ref, cv_ref, *, tq, tk, pos0, n_cmp, n_sel):
    i = pl.program_id(1)
    p_lo = pos0 + i * tq
    rows = HEADS * tq
    masks = _lane_group_masks(GROUP_W, HD, HEADS)

    @pl.when(i == 0)
    def _():
        for c in range(nkv_ref.shape[2] // LANES):
            rows_ref[c * LANES:(c + 1) * LANES, :] = _phase_rows(nkv_ref[0, 0:LANES, c * LANES:(c + 1) * LANES])
        _nsa_compress([rows_ref], w1_ref, b1_ref, pe_ref, w2k_ref, w2v_ref, [ck_ref], [cv_ref], n_cmp)

    qb = _stack_groups(q_ref[0], masks).astype(BF16)
    qpos = _qpos(p_lo, tq, rows)
    scale = HD ** -0.5
    o_cmp, sel = _nsa_cmp_and_select(qb, qpos, ck_ref, cv_ref, tq, n_sel)
    sel_bf = sel.astype(BF16)
    j_own = (p_lo + tq - 1) // tk

    def slc_step(j, carry, causal):
        cols = pl.ds(pl.multiple_of(j * tk, tk), tk)
        kt, vt = _shared_tile(nkv_ref[0, 2 * HD:3 * HD, cols], nkv_ref[0, 3 * HD:4 * HD, cols])
        mask = _slc_mask(sel_bf, j * (tk // NSA_SEL_BLOCK), j * tk, tk, qpos, rows, causal)
        return _flash_step(qb, kt, vt, mask, scale, carry)

    slc = slc_step(j_own, _flash_init(rows), True)
    slc = lax.fori_loop(0, j_own, lambda j, c: slc_step(j, c, False), slc)

    def win_step(j, carry, mask_fn):
        cols = pl.ds(pl.multiple_of(j * tk, tk), tk)
        kt, vt = _shared_tile(wkv_ref[0, 0:HD, cols], wkv_ref[0, HD:2 * HD, cols])
        return _flash_step(qb, kt, vt, mask_fn(_kpos(j * tk, tk, rows)), scale, carry)

    win = win_step(j_own, _flash_init(rows), lambda kp: kp <= qpos)
    j_lo = jnp.maximum(p_lo - (NSA_WINDOW - 1), 0) // tk
    win = lax.fori_loop(j_lo, j_own, lambda j, c: win_step(j, c, lambda kp: kp > qpos - NSA_WINDOW), win)
    o_ref[0] = _nsa_combine(g_ref[0], o_cmp, slc, win, tq, masks)


def _nsa_sample_kernel(pt_ref, samples, *, n_pages, page, past, n_cmp, n_sel, w_eff):
    scale = HD ** -0.5
    masks = _lane_group_masks(GROUP_W, HD, HEADS)
    ctxs = []
    for refs in samples:
        q_ref, page_refs, new_ref = refs[0], refs[1:1 + n_pages], refs[1 + n_pages]
        (g_ref, wst_ref, wnew_ref, w1_ref, b1_ref, pe_ref, w2k_ref, w2v_ref,
         o_ref, rows_ref, ck_ref, cv_ref, sk_ref, sv_ref, wk_ref, wv_ref) = refs[2 + n_pages:]
        ctxs.append(dict(q_ref=q_ref, page_refs=page_refs, new_ref=new_ref, g_ref=g_ref, wst_ref=wst_ref,
                         wnew_ref=wnew_ref, cmp=(w1_ref, b1_ref, pe_ref, w2k_ref, w2v_ref), o_ref=o_ref,
                         rows_ref=rows_ref, ck_ref=ck_ref, cv_ref=cv_ref, sk_ref=sk_ref, sv_ref=sv_ref,
                         wk_ref=wk_ref, wv_ref=wv_ref))
    tq = ctxs[0]["q_ref"].shape[1]
    rows = HEADS * tq
    qpos = _qpos(past, tq, rows)
    assert tq <= NSA_STRIDE

    for c in ctxs:
        c["rows_ref"][...] = jnp.zeros_like(c["rows_ref"])
        for p in range(n_pages):
            c["rows_ref"][p * page:(p + 1) * page, :] = _phase_rows(c["page_refs"][p][0, 0, 0:LANES, :])
        for t in range(tq):
            c["rows_ref"][n_pages * page + 8 * t:n_pages * page + 8 * t + 1, :] = c["new_ref"][0, t:t + 1, 0:LANES]
    for c in ctxs:
        for p in range(n_pages):
            c["sk_ref"][:, p * page:(p + 1) * page] = c["page_refs"][p][0, 0, 2 * HD:3 * HD, :].astype(BF16)
            c["sv_ref"][:, p * page:(p + 1) * page] = c["page_refs"][p][0, 0, 3 * HD:4 * HD, :].astype(BF16)
        new_t = _rows_to_tile(c["new_ref"][0, :, 2 * HD:4 * HD], page)
        c["sk_ref"][:, n_pages * page:] = new_t[0:HD]
        c["sv_ref"][:, n_pages * page:] = new_t[HD:2 * HD]
        c["wk_ref"][:, 0:w_eff] = c["wst_ref"][0, 0, 0:HD, :].astype(BF16)
        c["wv_ref"][:, 0:w_eff] = c["wst_ref"][0, 0, HD:2 * HD, :].astype(BF16)
        wnew_t = _rows_to_tile(c["wnew_ref"][0], page)
        c["wk_ref"][:, w_eff:] = wnew_t[0:HD]
        c["wv_ref"][:, w_eff:] = wnew_t[HD:2 * HD]
        q = c["q_ref"][0]
        c["qb"] = _stack_groups(q, masks).astype(BF16)
        c["qs"] = jnp.concatenate([q[:, h * HD:(h + 1) * HD] for h in range(HEADS)], axis=0).astype(BF16)
    for c in ctxs:
        wmask = _win_mask(past - w_eff, c["wk_ref"].shape[1], qpos, rows)
        c["win"] = _softmax_context(_dot(c["qs"], c["wk_ref"][...]) * scale, wmask, c["wv_ref"][...])
    _nsa_compress([c["rows_ref"] for c in ctxs], *ctxs[0]["cmp"], [c["ck_ref"] for c in ctxs],
                  [c["cv_ref"] for c in ctxs], n_cmp)
    for c in ctxs:
        c["o_cmp"], c["sel"] = _nsa_cmp_and_select(c["qb"], qpos, c["ck_ref"], c["cv_ref"], tq, n_sel)
    ctx = ctxs[0]["sk_ref"].shape[1]
    ei = lax.broadcasted_iota(jnp.int32, (LANES, ctx), 0)
    ej = lax.broadcasted_iota(jnp.int32, (LANES, ctx), 1) // NSA_SEL_BLOCK
    expand = jnp.where(ei == ej, 1.0, 0.0).astype(BF16)
    causal = _kpos(0, ctx, rows) <= qpos
    widen = lambda t: (t[0], t[1], jnp.concatenate([t[2]] * HEADS, axis=1))
    for c in ctxs:
        picked = _dot(c["sel"].astype(BF16), expand)
        mask = (jnp.concatenate([picked] * HEADS, axis=0) > 0.5) & causal
        slc = _softmax_context(_dot(c["qs"], c["sk_ref"][...]) * scale, mask, c["sv_ref"][...])
        c["o_ref"][0] = _nsa_combine(c["g_ref"][0], c["o_cmp"], widen(slc), widen(c["win"]), tq, masks)


def _nsa_cmp_weights(pe, w1, b1, w2):
    z = jnp.zeros((NSA_CMP_LEN, HD, NSA_CMP_HID), F32)
    w1cat = jnp.concatenate([jnp.concatenate([w1[0], z], axis=2), jnp.concatenate([z, w1[1]], axis=2)], axis=1)
    b1cat = jnp.concatenate([b1[0], b1[1]])[None, :]
    pecat = jnp.concatenate([pe[0], pe[1]], axis=1)
    z2 = jnp.zeros((NSA_CMP_HID, GROUP_W), F32)
    w2k4 = jnp.concatenate([jnp.tile(w2[0], (1, HEADS)), z2], axis=0)
    w2v4 = jnp.concatenate([z2, jnp.tile(w2[1], (1, HEADS))], axis=0)
    return w1cat.astype(BF16), b1cat, pecat, w2k4.astype(BF16), w2v4.astype(BF16)


def _cmp_specs(index_map):
    full = lambda *shape: pl.BlockSpec(shape, lambda *a: (0,) * len(shape))
    return [full(NSA_CMP_LEN, LANES, 2 * NSA_CMP_HID), full(1, 2 * NSA_CMP_HID),
            full(NSA_CMP_LEN, LANES), full(2 * NSA_CMP_HID, GROUP_W), full(2 * NSA_CMP_HID, GROUP_W)]


def _layer_norm(x, g, b):
    xc = x - jnp.mean(x, axis=-1, keepdims=True)
    var = jnp.mean(xc * xc, axis=-1, keepdims=True)
    return xc * lax.rsqrt(var + LN_EPS) * g + b


def _outproj_kernel(x_ref, a_ref, b_ref, c_ref, d_ref, w_ref, g_ref, be_ref, o_ref):
    mix = _dot(a_ref[...].astype(BF16), w_ref[0:256, :])
    mix = mix + _dot(b_ref[...].astype(BF16), w_ref[256:512, :])
    mix = mix + _dot(c_ref[...].astype(BF16), w_ref[512:768, :])
    mix = mix + _dot(d_ref[...].astype(BF16), w_ref[768:1024, :])
    o_ref[...] = _layer_norm(ALPHA * x_ref[...] + mix, g_ref[...], be_ref[...])


def _outproj_norm(x2d, outs, w_out_bf, gain, bias, tm):
    n = x2d.shape[0]
    row = lambda w: pl.BlockSpec((tm, w), lambda r: (r, 0))
    return pl.pallas_call(
        _outproj_kernel,
        grid=(n // tm,),
        in_specs=[row(D_MODEL), row(GROUP_W), row(GROUP_W), row(GROUP_W), row(GROUP_W),
                  pl.BlockSpec((D_MODEL, D_MODEL), lambda r: (0, 0)),
                  pl.BlockSpec((1, D_MODEL), lambda r: (0, 0)), pl.BlockSpec((1, D_MODEL), lambda r: (0, 0))],
        out_specs=row(D_MODEL),
        out_shape=jax.ShapeDtypeStruct((n, D_MODEL), F32),
        compiler_params=_params(1),
        name="outproj_ln",
    )(x2d, *outs, w_out_bf, gain, bias)


def _moe_kernel(h_ref, wr_ref, rb_ref, wgu_ref, wd_ref, g_ref, be_ref, o_ref, hb_ref, comb_ref, acc_ref):
    e = pl.program_id(1)

    @pl.when(e == 0)
    def _():
        hb = h_ref[...].astype(BF16)
        hb_ref[...] = hb
        logit = _dot(hb, wr_ref[...]) + rb_ref[...]
        lane = lax.broadcasted_iota(jnp.int32, logit.shape, 1)
        is_g = lane < N_GROUPS
        lg = jnp.where(is_g, logit, NEG_BIG)
        gmax = jnp.max(lg, axis=-1, keepdims=True)
        gsel = jnp.min(jnp.where(is_g & (lg == gmax), lane, LANES), axis=-1, keepdims=True)
        pg = 1.0 / jnp.sum(jnp.where(is_g, jnp.exp(lg - gmax), 0.0), axis=-1, keepdims=True)
        e0 = N_GROUPS + gsel * EXPERTS_PER_GROUP
        in_g = (lane >= e0) & (lane < e0 + EXPERTS_PER_GROUP)
        le = jnp.where(in_g, logit, NEG_BIG)
        v1 = jnp.max(le, axis=-1, keepdims=True)
        i1 = jnp.min(jnp.where(in_g & (le == v1), lane, LANES), axis=-1, keepdims=True)
        rest = in_g & (lane != i1)
        le2 = jnp.where(rest, logit, NEG_BIG)
        v2 = jnp.max(le2, axis=-1, keepdims=True)
        i2 = jnp.min(jnp.where(rest & (le2 == v2), lane, LANES), axis=-1, keepdims=True)
        ex = jnp.exp(v2 - v1)
        w1 = pg / (1.0 + ex)
        w2 = pg * ex / (1.0 + ex)
        comb_ref[...] = jnp.where(lane == i1, w1, 0.0) + jnp.where(lane == i2, w2, 0.0)
        acc_ref[...] = jnp.zeros_like(acc_ref)

    hids = []
    for k in range(EXPERTS_PER_STEP):
        gu = _dot(hb_ref[...], wgu_ref[k])
        gt, up = gu[:, :EXPERT_HID], gu[:, EXPERT_HID:]
        hid = gt / (1.0 + jnp.exp(-gt)) * up
        hids.append((hid * _col(comb_ref[...], N_GROUPS + e * EXPERTS_PER_STEP + k)).astype(BF16))
    acc_ref[...] += _dot(jnp.concatenate(hids, axis=1), wd_ref[...].reshape(EXPERTS_PER_STEP * EXPERT_HID, D_MODEL))

    @pl.when(e == N_EXPERTS // EXPERTS_PER_STEP - 1)
    def _():
        o_ref[...] = _layer_norm(ALPHA * h_ref[...] + acc_ref[...], g_ref[...], be_ref[...])


def _moe_norm(h2d, wr, rb, wgu, wd, gain, bias, tm):
    n = h2d.shape[0]
    assert n % tm == 0
    return pl.pallas_call(
        _moe_kernel,
        grid=(n // tm, N_EXPERTS // EXPERTS_PER_STEP),
        in_specs=[pl.BlockSpec((tm, D_MODEL), lambda r, e: (r, 0)),
                  pl.BlockSpec((D_MODEL, LANES), lambda r, e: (0, 0)),
                  pl.BlockSpec((1, LANES), lambda r, e: (0, 0)),
                  pl.BlockSpec((EXPERTS_PER_STEP, D_MODEL, 2 * EXPERT_HID), lambda r, e: (e, 0, 0)),
                  pl.BlockSpec((EXPERTS_PER_STEP, EXPERT_HID, D_MODEL), lambda r, e: (e, 0, 0)),
                  pl.BlockSpec((1, D_MODEL), lambda r, e: (0, 0)),
                  pl.BlockSpec((1, D_MODEL), lambda r, e: (0, 0))],
        out_specs=pl.BlockSpec((tm, D_MODEL), lambda r, e: (r, 0)),
        out_shape=jax.ShapeDtypeStruct((n, D_MODEL), F32),
        scratch_shapes=[pltpu.VMEM((tm, D_MODEL), BF16), pltpu.VMEM((tm, LANES), F32),
                        pltpu.VMEM((tm, D_MODEL), F32)],
        compiler_params=_params(2),
        name="moe_ln",
    )(h2d, wr, rb, wgu, wd, gain, bias)


def _layer_weights(l, w_in, w_out, pe, w1, b1, w2, lam_p, subln, ln_g, ln_b, rgw, rgb, rew, reb, wg, wu, wd):
    wr = jnp.concatenate([rgw[l], rew[l], jnp.zeros((D_MODEL, LANES - N_GROUPS - N_EXPERTS), F32)], axis=1)
    rb = jnp.concatenate([rgb[l], reb[l], jnp.zeros((LANES - N_GROUPS - N_EXPERTS,), F32)])[None, :]
    wq, wkv = _proj_weights(w_in[l])
    return dict(
        wq=wq, wkv=wkv, wkv_t=wkv.T, wall=jnp.concatenate([wq, wkv], axis=1),
        w_out=w_out[l].astype(BF16),
        cmp=_nsa_cmp_weights(pe[l], w1[l], b1[l], w2[l]),
        lam_p=lam_p[l],
        subln4=jnp.tile(subln[l], HEADS)[None, :],
        lam_init=0.8 - 0.6 * math.exp(-0.3 * l),
        g1=ln_g[l, 0][None, :], b1=ln_b[l, 0][None, :], g2=ln_g[l, 1][None, :], b2=ln_b[l, 1][None, :],
        wr=wr.astype(BF16), rb=rb,
        wgu=jnp.concatenate([wg[l], wu[l]], axis=2).astype(BF16),
        wd=wd[l].astype(BF16),
    )


def _ffn(x, outs, lw, tm, tm_ffn):
    b, t, _ = x.shape
    x2d = x.reshape(b * t, D_MODEL)
    h = _outproj_norm(x2d, [o.reshape(b * t, GROUP_W) for o in outs], lw["w_out"], lw["g1"], lw["b1"], tm)
    y = _moe_norm(h, lw["wr"], lw["rb"], lw["wgu"], lw["wd"], lw["g2"], lw["b2"], tm_ffn)
    return y.reshape(b, t, D_MODEL)


def kernel(x_prompt, x_sample, cache_sb_kv, cache_nsa_kv, cache_diff_kv, cache_moba_kv, state_nsa_win, page_table, w_in, w_out, nsa_cmp_pe, nsa_cmp_w1, nsa_cmp_b1, nsa_cmp_w2, diff_lambda, diff_subln, ln_gain, ln_bias, router_group_w, router_group_b, router_expert_w, router_expert_b, expert_w_gate, expert_w_up, expert_w_down):
    bp, tp, _ = x_prompt.shape
    bs, ts, _ = x_sample.shape
    depth = w_in.shape[0]
    n_phys, page = cache_sb_kv.shape[1], cache_sb_kv.shape[2]
    n_pages = page_table.shape[1]
    past = n_pages * page
    w_eff = state_nsa_win.shape[2]
    l_s = past + ts
    l_s_pad = -(-l_s // MOBA_BLOCK) * MOBA_BLOCK
    tm_p = 256
    tm_s = min(256, bs * ts)
    assert tp % KV_TILE == 0 and tm_p % ts == 0 and (bs * ts) % tm_s == 0 and ts <= page
    assert page == LANES and past % MOBA_BLOCK == 0 and w_eff % LANES == 0 and w_eff == min(NSA_WINDOW, past)

    pool_sb = jnp.transpose(cache_sb_kv, (0, 1, 3, 4, 5, 2)).reshape(depth, n_phys, 2 * GROUP_W, page)
    pool_diff = jnp.transpose(cache_diff_kv, (0, 1, 3, 4, 5, 2)).reshape(depth, n_phys, 2 * GROUP_W, page)
    pool_moba = jnp.transpose(cache_moba_kv, (0, 1, 3, 4, 5, 2)).reshape(depth, n_phys, 2 * GROUP_W, page)
    pool_nsa = jnp.transpose(cache_nsa_kv, (0, 1, 3, 4, 2)).reshape(depth, n_phys, 4 * HD, page)
    win_state = jnp.transpose(state_nsa_win, (0, 1, 3, 4, 2)).reshape(depth, bs, 2 * HD, w_eff)

    tabs_p = _rope_tables(jnp.arange(tp, dtype=jnp.int32))
    tabs_s = _rope_tables(past + (jnp.arange(tm_s, dtype=jnp.int32) % ts))
    tabs_p_t, tabs_s_t = jnp.transpose(tabs_p, (0, 2, 1)), jnp.transpose(tabs_s, (0, 2, 1))
    n_cmp_p, n_sel_p = tp // NSA_STRIDE, tp // NSA_SEL_BLOCK
    n_cmp_s, n_sel_s = l_s_pad // NSA_STRIDE, l_s_pad // NSA_SEL_BLOCK
    ncp = lambda n: -(-n // LANES) * LANES

    y_p, y_s = x_prompt, x_sample
    kv_bufs, new_s = None, []
    for l in range(depth):
        lw = _layer_weights(l, w_in, w_out, nsa_cmp_pe, nsa_cmp_w1, nsa_cmp_b1, nsa_cmp_w2, diff_lambda,
                            diff_subln, ln_gain, ln_bias, router_group_w, router_group_b, router_expert_w,
                            router_expert_b, expert_w_gate, expert_w_up, expert_w_down)
        lam_specs = [pl.BlockSpec((4, DIFF_QK), lambda *a: (0, 0)), pl.BlockSpec((1, GROUP_W), lambda *a: (0, 0))]

        pp = _project(y_p.reshape(bp * tp, D_MODEL), lw["wq"], lw["wkv_t"], tabs_p, tabs_p_t, tm_p,
                      Q_SEGS, KV_SEGS, tp, depth, l, kv_bufs)
        kv_bufs = [pp[s_[0]] for s_ in KV_SEGS]
        q3 = lambda a: a.reshape(bp, tp, a.shape[-1])
        o_sb = _prompt_call(functools.partial(_sb_kernel, tq=256, tk=KV_TILE, pos0=0, chunk=HEADS * 256),
                            q3(pp["sbq"]), pp["sbkv"], l, (), (), 256, "sb_attn",
                            scratch=[pltpu.VMEM((HEADS * 256, GROUP_W), BF16),
                                     pltpu.VMEM((HEADS * 256, GROUP_W), F32),
                                     pltpu.VMEM((HEADS * 256, 1), F32)])
        o_nsa = _prompt_call(
            functools.partial(_nsa_kernel, tq=256, tk=KV_TILE, pos0=0, n_cmp=n_cmp_p, n_sel=n_sel_p),
            q3(pp["nq"]), pp["nkv"], l, (q3(pp["gate"]), pp["wkv"]) + lw["cmp"],
            [pl.BlockSpec((1, 256, LANES), lambda bb, i: (bb, i, 0)),
             pl.BlockSpec((None, 1, 2 * HD, tp), lambda bb, i: (l, bb, 0, 0))] + _cmp_specs(None), 256, "nsa_attn",
            scratch=[pltpu.VMEM((tp, LANES), F32), pltpu.VMEM((ncp(n_cmp_p), GROUP_W), F32),
                     pltpu.VMEM((ncp(n_cmp_p), GROUP_W), F32)])
        o_diff = _prompt_call(
            functools.partial(_diff_kernel, tq=128, tk=KV_TILE, pos0=0, lam_init=lw["lam_init"]),
            q3(pp["dq"]), pp["dkv"], l, (lw["lam_p"], lw["subln4"]), lam_specs, 128, "diff_attn")
        o_moba = _prompt_call(
            functools.partial(_moba_kernel, tq=256, tk=KV_TILE, pos0=0, n_blocks=tp // MOBA_BLOCK),
            q3(pp["mq"]), pp["mkv"], l, (), (), 256, "moba_attn", scratch=[pltpu.VMEM((LANES, GROUP_W), F32)])
        y_p = _ffn(y_p, (o_sb, o_nsa, o_diff, o_moba), lw, tm_p, math.gcd(1024, bp * tp))

        ps = _project(y_s.reshape(bs * ts, D_MODEL), lw["wall"], lw["wkv_t"], tabs_s, tabs_s_t, tm_s,
                      Q_SEGS + KV_SEGS, (), ts)
        s3 = lambda a: a.reshape(bs, ts, a.shape[-1])
        new_tile = lambda name: s3(ps[name])
        common = dict(n_pages=n_pages, page=page, past=past)
        ctx_scratch = lambda ch: [((ch, past + page), BF16), ((ch, past + page), BF16)]
        o_sb = _sample_call(functools.partial(_sb_sample_kernel, **common), page_table, s3(ps["sbq"]),
                            pool_sb, l, new_tile("sbkv"), (), (), (), "sb_attn_s", scratch=ctx_scratch(GROUP_W))
        o_nsa = _sample_call(
            functools.partial(_nsa_sample_kernel, n_cmp=n_cmp_s, n_sel=n_sel_s, w_eff=w_eff, **common),
            page_table, s3(ps["nq"]), pool_nsa, l, new_tile("nkv"),
            (s3(ps["gate"]), win_state, new_tile("wkv")) + lw["cmp"],
            [pl.BlockSpec((NSA_SAMPLES_PER_STEP, ts, LANES), lambda bb, pt: (bb, 0, 0)),
             pl.BlockSpec((1, NSA_SAMPLES_PER_STEP, 2 * HD, w_eff), lambda bb, pt: (l, bb, 0, 0)),
             pl.BlockSpec((NSA_SAMPLES_PER_STEP, ts, 2 * HD), lambda bb, pt: (bb, 0, 0))] + _cmp_specs(None),
            (0, 1, 0, None, None, None, None, None), "nsa_attn_s",
            scratch=[((l_s_pad, LANES), F32), ((ncp(n_cmp_s), GROUP_W), F32), ((ncp(n_cmp_s), GROUP_W), F32)]
            + ctx_scratch(HD) + [((HD, w_eff + page), BF16), ((HD, w_eff + page), BF16)],
            spb=NSA_SAMPLES_PER_STEP, staged=True)
        o_diff = _sample_call(functools.partial(_diff_sample_kernel, lam_init=lw["lam_init"], **common),
                              page_table, s3(ps["dq"]), pool_diff, l, new_tile("dkv"),
                              (lw["lam_p"], lw["subln4"]), lam_specs, (None, None), "diff_attn_s",
                              scratch=ctx_scratch(GROUP_W))
        o_moba = _sample_call(functools.partial(_moba_sample_kernel, **common), page_table, s3(ps["mq"]),
                              pool_moba, l, new_tile("mkv"), (), (), (), "moba_attn_s",
                              scratch=ctx_scratch(GROUP_W))
        y_s = _ffn(y_s, (o_sb, o_nsa, o_diff, o_moba), lw, tm_s, math.gcd(1024, bs * ts))
        win_new = jnp.concatenate([win_state[l][:, :, ts:], jnp.transpose(s3(ps["wkv"]), (0, 2, 1))], axis=2)
        new_s.append((ps["sbkv"], ps["nkv"], ps["dkv"], ps["mkv"], win_new))

    def heads_out(a, b, t):
        return jnp.transpose(a.reshape(depth, b, 2, HEADS, HD, t), (0, 1, 5, 2, 3, 4))

    def slots_out(a, b, t, n):
        return jnp.transpose(a.reshape(depth, b, n, HD, t), (0, 1, 4, 2, 3))

    st_p = [pp["sbkv"], pp["nkv"], pp["dkv"], pp["mkv"], pp["wkv"][:, :, :, tp - min(NSA_WINDOW, tp):]]
    st_s = [jnp.stack([r[k] for r in new_s]) for k in range(5)]
    outs_p = (heads_out(st_p[0], bp, tp), slots_out(st_p[1], bp, tp, 4), heads_out(st_p[2], bp, tp),
              heads_out(st_p[3], bp, tp), slots_out(st_p[4], bp, min(NSA_WINDOW, tp), 2))
    outs_s = (st_s[0].reshape(depth, bs, ts, 2, HEADS, HD), st_s[1].reshape(depth, bs, ts, 4, HD),
              st_s[2].reshape(depth, bs, ts, 2, HEADS, HD), st_s[3].reshape(depth, bs, ts, 2, HEADS, HD),
              slots_out(st_s[4], bs, w_eff, 2))
    return (y_p, y_s, *outs_p, *outs_s)
```

```python
import functools
import math

import numpy as np
import jax
import jax.numpy as jnp
from jax import lax
from jax.experimental import pallas as pl
from jax.experimental.pallas import tpu as pltpu

F32 = jnp.float32
BF16 = jnp.bfloat16

D_MODEL = 1024
HEADS = 4
HD = 64
GROUP_W = HEADS * HD
DIFF_QK = HD // 2
DEPTH_LAYERS = 2
ROPE_THETA = 500000.0
ROPE_FRACTION = 4
NSA_STRIDE = 16
NSA_CMP_LEN = 32
NSA_CMP_HID = 128
NSA_SEL_BLOCK = 64
NSA_TOPN = 16
NSA_WINDOW = 512
NSA_FORCE = 1.0e4
MOBA_BLOCK = 256
MOBA_TOPK = 3
N_GROUPS = 4
EXPERTS_PER_GROUP = 4
N_EXPERTS = 16
EXPERT_HID = 256
EXPERTS_PER_STEP = 2
ALPHA = (2 * DEPTH_LAYERS) ** 0.25
LN_EPS = 1e-5
RMS_EPS = 1e-5
NEG_BIG = -1e30
LANES = 128
KV_TILE = 256
SAMPLES_PER_STEP = 4
NSA_SAMPLES_PER_STEP = 4
VMEM_LIMIT = 56 * 1024 * 1024

Q_SEGS = (("sbq", 256, (0, 0)), ("nq", 256, (1, 1)), ("dq", 256, (3, 3)), ("mq", 256, (1, 1)),
          ("gate", 128, (4,)))
KV_SEGS = (("sbkv", 512, (0, 0, 0, 0)), ("nkv", 256, (2, 2)), ("wkv", 128, (2,)),
           ("dkv", 512, (3, 3, 0, 0)), ("mkv", 512, (1, 1, 0, 0)))
Q_COLS = sum(s[1] for s in Q_SEGS)
KV_COLS = sum(s[1] for s in KV_SEGS)


def _dot(a, b):
    return jnp.dot(a, b, preferred_element_type=F32)


def _dot_t(a, b):
    return lax.dot_general(a, b, (((1,), (1,)), ((), ())), preferred_element_type=F32)


def _split_dot(a, b_bf):
    hi = a.astype(BF16)
    lo = (a - hi.astype(F32)).astype(BF16)
    return _dot(hi, b_bf) + _dot(lo, b_bf)


def _lane_group_masks(width, group, n):
    lane = lax.broadcasted_iota(jnp.int32, (1, width), 1)
    return [jnp.where((lane // group) == g, 1.0, 0.0).astype(F32) for g in range(n)]


def _stack_groups(q, masks):
    return jnp.concatenate([q * m for m in masks], axis=0)


def _col(x, idx):
    lane = lax.broadcasted_iota(jnp.int32, x.shape, 1)
    return jnp.sum(jnp.where(lane == idx, x, 0.0), axis=-1, keepdims=True)


def _params(n_axes):
    return pltpu.CompilerParams(dimension_semantics=("arbitrary",) * n_axes, vmem_limit_bytes=VMEM_LIMIT)


def _rope_rows(y, tab_ref, kind):
    half = 4 if kind == 3 else 8
    t0 = (kind - 1) * 3
    return (y * tab_ref[t0] + pltpu.roll(y, LANES - half, 1) * tab_ref[t0 + 1]
            + pltpu.roll(y, half, 1) * tab_ref[t0 + 2])


def _rope_cols(y, tab_ref, kind):
    half = 4 if kind == 3 else 8
    t0 = (kind - 1) * 3
    return (y * tab_ref[t0] + pltpu.roll(y, LANES - half, 0) * tab_ref[t0 + 1]
            + pltpu.roll(y, half, 0) * tab_ref[t0 + 2])


def _proj_kernel(x_ref, w_ref, wt_ref, tab_ref, tabt_ref, *out_refs, row_segs, col_segs, n_carried):
    out_refs = out_refs[n_carried:]
    xb = x_ref[...].astype(BF16)
    off = 0
    n_row = len(row_segs)
    for (_, width, kinds), o_ref in zip(row_segs, out_refs[:n_row]):
        for c0 in range(0, width, 256):
            cw = min(256, width - c0)
            y2 = _dot(xb, w_ref[:, off + c0:off + c0 + cw])
            for cc in range(cw // LANES):
                y = y2[:, cc * LANES:(cc + 1) * LANES]
                kind = kinds[(c0 // LANES) + cc]
                if kind in (1, 2, 3):
                    y = _rope_rows(y, tab_ref, kind)
                elif kind == 4:
                    y = 1.0 / (1.0 + jnp.exp(-y))
                o_ref[:, c0 + cc * LANES:c0 + (cc + 1) * LANES] = y
        off += width
    off = 0
    for (_, width, kinds), o_ref in zip(col_segs, out_refs[n_row:]):
        for c0 in range(0, width, 256):
            cw = min(256, width - c0)
            y2 = _dot_t(wt_ref[off + c0:off + c0 + cw, :], xb)
            for cc in range(cw // LANES):
                y = y2[cc * LANES:(cc + 1) * LANES, :]
                kind = kinds[(c0 // LANES) + cc]
                if kind in (1, 2, 3):
                    y = _rope_cols(y, tabt_ref, kind)
                o_ref[0, c0 + cc * LANES:c0 + (cc + 1) * LANES, :] = y
        off += width


def _project(x2d, w_rows, w_cols, tabs, tabs_t, tm, row_segs, col_segs, seq, depth=1, layer=0, carried=None):
    n = x2d.shape[0]
    carried = list(carried or [])
    n_tab = tabs.shape[1] // tm
    per_seq = max(seq // tm, 1)
    in_specs = [pl.BlockSpec((tm, D_MODEL), lambda r: (r, 0)),
                pl.BlockSpec(w_rows.shape, lambda r: (0, 0)),
                pl.BlockSpec(w_cols.shape, lambda r: (0, 0)),
                pl.BlockSpec((9, tm, LANES), lambda r: (0, r % n_tab, 0)),
                pl.BlockSpec((9, LANES, tm), lambda r: (0, 0, r % n_tab))]
    out_specs = [pl.BlockSpec((tm, s[1]), lambda r: (r, 0)) for s in row_segs]
    in_specs += [pl.BlockSpec(memory_space=pl.ANY)] * len(carried)
    out_specs += [pl.BlockSpec((None, 1, s[1], tm), lambda r: (layer, r // per_seq, 0, r % per_seq))
                  for s in col_segs]
    out_shape = [jax.ShapeDtypeStruct((n, s[1]), F32) for s in row_segs]
    out_shape += [jax.ShapeDtypeStruct((depth, n // seq, s[1], seq), F32) for s in col_segs]
    outs = pl.pallas_call(
        functools.partial(_proj_kernel, row_segs=row_segs, col_segs=col_segs, n_carried=len(carried)),
        grid=(n // tm,),
        in_specs=in_specs, out_specs=out_specs, out_shape=out_shape,
        input_output_aliases={5 + k: len(row_segs) + k for k in range(len(carried))},
        compiler_params=_params(1),
        name="proj",
    )(x2d, w_rows, w_cols, tabs, tabs_t, *carried)
    return dict(zip([s[0] for s in row_segs + col_segs], outs))


def _rope_tables(pos):
    pos = pos.astype(F32)
    lane = np.arange(LANES)
    tabs = []
    for dim, first_only in ((HD, False), (HD, True), (DIFF_QK, False)):
        rot = dim // ROPE_FRACTION
        half = rot // 2
        inv = jnp.power(jnp.float32(ROPE_THETA), -jnp.arange(half, dtype=F32) * (2.0 / rot))
        ang = pos[:, None] * inv[None, :]
        cos, sin = jnp.cos(ang), jnp.sin(ang)
        c = lane % dim
        in_rot = c < rot
        if first_only:
            in_rot = in_rot & (lane < HD)
        first = in_rot & (c < half)
        second = in_rot & (c >= half)
        fidx = c % half
        cos_l, sin_l = cos[:, fidx], sin[:, fidx]
        tabs.append(jnp.where(in_rot[None, :], cos_l, 1.0))
        tabs.append(jnp.where(first[None, :], -sin_l, 0.0))
        tabs.append(jnp.where(second[None, :], sin_l, 0.0))
    return jnp.stack(tabs).astype(F32)


def _proj_weights(w_in_l):
    o = [int(v) for v in np.cumsum((0, 256, 256, 256, 256, 256, 128, 12, 256, 256, 256, 256, 256, 256))]
    sl = lambda a, b: w_in_l[:, a:b]
    gate = jnp.pad(sl(o[6], o[7]), ((0, 0), (0, LANES - 12)))
    wq = jnp.concatenate([sl(o[0], o[1]), sl(o[3], o[4]), sl(o[7], o[8]), sl(o[10], o[11]), gate], axis=1)
    wkv = jnp.concatenate([sl(o[1], o[3]), sl(o[4], o[5]), sl(o[5], o[6]), sl(o[8], o[10]), sl(o[11], o[13])],
                          axis=1)
    return wq.astype(BF16), wkv.astype(BF16)


def _qpos(p_lo, tq, rows):
    r = lax.broadcasted_iota(jnp.int32, (rows, 1), 0)
    return p_lo + (r & (tq - 1))


def _kpos(kpos0, tk, rows):
    return kpos0 + lax.broadcasted_iota(jnp.int32, (rows, tk), 1)


def _flash_step(qb, kt, vt, mask, scale, carry):
    m, l, acc = carry
    s = _dot(qb, kt) * scale
    if mask is not None:
        s = jnp.where(mask, s, NEG_BIG)
    m_new = jnp.maximum(m, jnp.max(s, axis=-1, keepdims=True))
    a = jnp.exp(m - m_new)
    p = jnp.exp(s - m_new)
    l = a * l + jnp.sum(p, axis=-1, keepdims=True)
    acc = a * acc + _dot_t(p.astype(BF16), vt)
    return m_new, l, acc


def _flash_init(rows):
    return (jnp.full((rows, 1), NEG_BIG, F32), jnp.zeros((rows, 1), F32), jnp.zeros((rows, GROUP_W), F32))


def _unstack_heads(rows_val, tq, masks):
    out = rows_val[0:tq] * masks[0]
    for h in range(1, HEADS):
        out = out + rows_val[h * tq:(h + 1) * tq] * masks[h]
    return out


def _later_matrix(tk):
    r_i = lax.broadcasted_iota(jnp.int32, (tk, tk), 0)
    c_i = lax.broadcasted_iota(jnp.int32, (tk, tk), 1)
    return jnp.where(r_i > c_i, 1.0, 0.0).astype(BF16)


def _tile_from_ref(ref, idx, j, tk):
    cols = pl.ds(pl.multiple_of(j * tk, tk), tk)
    return (ref[idx + (slice(0, GROUP_W), cols)].astype(BF16),
            ref[idx + (slice(GROUP_W, 2 * GROUP_W), cols)].astype(BF16))


def _page_tile(ref):
    idx = (0,) * (len(ref.shape) - 2)
    return (ref[idx + (slice(0, GROUP_W), slice(None))].astype(BF16),
            ref[idx + (slice(GROUP_W, 2 * GROUP_W), slice(None))].astype(BF16))


def _rows_to_tile(rows, page):
    t, w = rows.shape
    return jnp.concatenate([rows, jnp.zeros((page - t, w), F32)], axis=0).T.astype(BF16)


def _gather_context(page_refs, new_ref, kt_ref, vt_ref, page):
    for p, ref in enumerate(page_refs):
        kt, vt = _page_tile(ref)
        kt_ref[:, p * page:(p + 1) * page] = kt
        vt_ref[:, p * page:(p + 1) * page] = vt
    new_t = _rows_to_tile(new_ref[0], page)
    p = len(page_refs)
    kt_ref[:, p * page:(p + 1) * page] = new_t[0:GROUP_W]
    vt_ref[:, p * page:(p + 1) * page] = new_t[GROUP_W:2 * GROUP_W]


def _softmax_context(s, mask, vt):
    s = jnp.where(mask, s, NEG_BIG)
    m = jnp.max(s, axis=-1, keepdims=True)
    p = jnp.exp(s - m)
    return m, jnp.sum(p, axis=-1, keepdims=True), _dot_t(p.astype(BF16), vt)


def _sb_tile(qb_ref, acc_ref, tail_ref, kt, vt, later_mat, kpos0, p_lo, tq, causal, chunk):
    rows, tk = qb_ref.shape[0], kt.shape[1]
    for r0 in range(0, rows, chunk):
        rs = slice(r0, r0 + chunk)
        z = _dot(qb_ref[rs, :], kt)
        log_sig = jnp.minimum(z, 0.0) - jnp.log(1.0 + jnp.exp(-jnp.abs(z)))
        log_fail = log_sig - z
        if causal:
            qpos = p_lo + ((r0 + lax.broadcasted_iota(jnp.int32, (chunk, 1), 0)) & (tq - 1))
            mask = _kpos(kpos0, tk, chunk) < qpos
            log_fail = jnp.where(mask, log_fail, 0.0)
        later = _split_dot(log_fail, later_mat) + tail_ref[rs, :]
        w = jnp.exp(log_sig + later)
        if causal:
            w = jnp.where(mask, w, 0.0)
        acc_ref[rs, :] += _dot_t(w.astype(BF16), vt)
        tail_ref[rs, :] += jnp.sum(log_fail, axis=-1, keepdims=True)


def _sb_kernel(q_ref, kv_ref, o_ref, qb_ref, acc_ref, tail_ref, *, tq, tk, pos0, chunk):
    i = pl.program_id(1)
    p_lo = pos0 + i * tq
    masks = _lane_group_masks(GROUP_W, HD, HEADS)
    qb_ref[...] = _stack_groups(q_ref[0] * (HD ** -0.5), masks).astype(BF16)
    acc_ref[...] = jnp.zeros_like(acc_ref)
    tail_ref[...] = jnp.zeros_like(tail_ref)
    later_mat = _later_matrix(tk)
    j_own = (p_lo + tq - 1) // tk
    kt, vt = _tile_from_ref(kv_ref, (0,), j_own, tk)
    _sb_tile(qb_ref, acc_ref, tail_ref, kt, vt, later_mat, j_own * tk, p_lo, tq, True, chunk)

    def body(jj, _):
        j = j_own - 1 - jj
        kt, vt = _tile_from_ref(kv_ref, (0,), j, tk)
        _sb_tile(qb_ref, acc_ref, tail_ref, kt, vt, later_mat, j * tk, p_lo, tq, False, chunk)
        return 0

    lax.fori_loop(0, j_own, body, 0)
    o_ref[0] = _unstack_heads(acc_ref[...], tq, masks)


def _sb_sample_kernel(pt_ref, q_ref, *refs, n_pages, page, past):
    page_refs, new_ref, o_ref = refs[:n_pages], refs[n_pages], refs[n_pages + 1]
    tq = q_ref.shape[1]
    rows = HEADS * tq
    masks = _lane_group_masks(GROUP_W, HD, HEADS)
    kt_ref, vt_ref = refs[n_pages + 2:]
    _gather_context(page_refs, new_ref, kt_ref, vt_ref, page)
    qb = _stack_groups(q_ref[0] * (HD ** -0.5), masks).astype(BF16)
    qpos = _qpos(past, tq, rows)
    later_mat = _later_matrix(page)
    ctx = kt_ref.shape[1]
    z = _dot(qb, kt_ref[...])
    mask = _kpos(0, ctx, rows) < qpos
    log_sig = jnp.minimum(z, 0.0) - jnp.log(1.0 + jnp.exp(-jnp.abs(z)))
    log_fail = jnp.where(mask, log_sig - z, 0.0)
    tail = jnp.zeros((rows, 1), F32)
    later = [None] * (n_pages + 1)
    for p in reversed(range(n_pages + 1)):
        blk = log_fail[:, p * page:(p + 1) * page]
        later[p] = _split_dot(blk, later_mat) + tail
        tail = tail + jnp.sum(blk, axis=-1, keepdims=True)
    w = jnp.where(mask, jnp.exp(log_sig + jnp.concatenate(later, axis=1)), 0.0)
    o_ref[0] = _unstack_heads(_dot_t(w.astype(BF16), vt_ref[...]), tq, masks)


def _prompt_call(kern, q, kv_t, layer, extra, extra_specs, tq, name, scratch=()):
    b, t, _ = q.shape
    rows_kv, l = kv_t.shape[2], kv_t.shape[3]
    return pl.pallas_call(
        kern,
        grid=(b, t // tq),
        in_specs=[pl.BlockSpec((1, tq, GROUP_W), lambda bb, i: (bb, i, 0)),
                  pl.BlockSpec((None, 1, rows_kv, l), lambda bb, i: (layer, bb, 0, 0))] + list(extra_specs),
        out_specs=pl.BlockSpec((1, tq, GROUP_W), lambda bb, i: (bb, i, 0)),
        out_shape=jax.ShapeDtypeStruct((b, t, GROUP_W), F32),
        scratch_shapes=list(scratch),
        compiler_params=_params(2),
        name=name,
    )(q, kv_t, *extra)


def _sample_call(body, page_table, q, pool_t, layer, new_t, extra, extra_specs, extra_axes, name, scratch=(),
                 spb=SAMPLES_PER_STEP, staged=False, slab_out=None):
    b, tq, _ = q.shape
    n_pages = page_table.shape[1]
    rows_kv, page = pool_t.shape[2], pool_t.shape[3]
    assert b % spb == 0

    def page_spec(s_, p):
        return pl.BlockSpec((1, 1, rows_kv, page), lambda bb, pt: (layer, pt[(bb * spb + s_) * n_pages + p], 0, 0))

    axes = [0] + [None] * (spb * n_pages) + [0] + list(extra_axes) + [0]
    n_scratch = len(scratch)
    operands = [page_table.reshape(-1), q] + [pool_t] * (spb * n_pages) + [new_t] + list(extra)
    out_specs = pl.BlockSpec((spb, tq, GROUP_W), lambda bb, pt: (bb, 0, 0))
    out_shape = jax.ShapeDtypeStruct((b, tq, GROUP_W), F32)
    carried_specs, aliases = [], {}
    if slab_out is not None:
        depth, slab_shape, carried = slab_out
        zeros = (0,) * len(slab_shape)
        out_specs = [out_specs, pl.BlockSpec((None, spb) + slab_shape, lambda bb, pt: (layer, bb) + zeros)]
        out_shape = [out_shape, jax.ShapeDtypeStruct((depth, b) + slab_shape, F32)]
        axes.append(0)
        if carried is not None:
            aliases = {len(operands): 1}
            operands.append(carried)
            carried_specs = [pl.BlockSpec(memory_space=pl.ANY)]
    n_carried = len(carried_specs)

    def kern(pt_ref, *refs):
        io, scr = refs[:len(refs) - n_scratch], refs[len(refs) - n_scratch:]
        n_in = len(io) - (1 if slab_out is None else 2)
        io = io[:n_in - n_carried] + io[n_in:]
        per_sample = []
        for s_ in range(spb):
            views = []
            for k, (r, ax) in enumerate(zip(io, axes)):
                if 1 <= k <= spb * n_pages:
                    if (k - 1) // n_pages == s_:
                        views.append(r)
                elif ax is None:
                    views.append(r)
                else:
                    views.append(r.at[(slice(None),) * ax + (pl.ds(s_, 1),)])
            per_sample.append(views + [r.at[s_] for r in scr])
        if staged:
            body(pt_ref, per_sample)
        else:
            for refs_s in per_sample:
                body(pt_ref, *refs_s)

    in_specs = ([pl.BlockSpec((spb, tq, GROUP_W), lambda bb, pt: (bb, 0, 0))]
                + [page_spec(s_, p) for s_ in range(spb) for p in range(n_pages)]
                + [pl.BlockSpec((spb, tq, rows_kv), lambda bb, pt: (bb, 0, 0))] + list(extra_specs)
                + carried_specs)
    return pl.pallas_call(
        kern,
        grid_spec=pltpu.PrefetchScalarGridSpec(
            num_scalar_prefetch=1, grid=(b // spb,), in_specs=in_specs, out_specs=out_specs,
            scratch_shapes=[pltpu.VMEM((spb,) + shape, dt) for shape, dt in scratch]),
        out_shape=out_shape,
        input_output_aliases=aliases,
        compiler_params=_params(1),
        name=name,
    )(*operands)


def _diff_finish(carry, tq, lam_ref, sub_ref, lam_init):
    _, l, acc = carry
    head_masks = _lane_group_masks(GROUP_W, HD, HEADS)
    o = acc / l
    lp = lam_ref[...]
    lam = (jnp.exp(jnp.sum(lp[0:1] * lp[1:2], axis=-1, keepdims=True))
           - jnp.exp(jnp.sum(lp[2:3] * lp[3:4], axis=-1, keepdims=True)) + lam_init)
    out = jnp.zeros((tq, GROUP_W), F32)
    for h in range(HEADS):
        oh = (o[(2 * h) * tq:(2 * h + 1) * tq] - lam * o[(2 * h + 1) * tq:(2 * h + 2) * tq]) * head_masks[h]
        ms = jnp.sum(oh * oh, axis=-1, keepdims=True) * (1.0 / HD)
        out = out + oh * lax.rsqrt(ms + RMS_EPS)
    return out * sub_ref[...] * (1.0 - lam_init)


def _diff_kernel(q_ref, kv_ref, lam_ref, sub_ref, o_ref, *, tq, tk, pos0, lam_init):
    i = pl.program_id(1)
    p_lo = pos0 + i * tq
    rows = 2 * HEADS * tq
    qb = _stack_groups(q_ref[0], _lane_group_masks(GROUP_W, DIFF_QK, 2 * HEADS)).astype(BF16)
    qpos = _qpos(p_lo, tq, rows)
    j_own = (p_lo + tq - 1) // tk
    kt, vt = _tile_from_ref(kv_ref, (0,), j_own, tk)
    carry = _flash_step(qb, kt, vt, _kpos(j_own * tk, tk, rows) <= qpos, DIFF_QK ** -0.5, _flash_init(rows))

    def body(width):
        def step(j, carry):
            kt, vt = _tile_from_ref(kv_ref, (0,), j, width)
            return _flash_step(qb, kt, vt, None, DIFF_QK ** -0.5, carry)
        return step

    carry = lax.fori_loop(0, j_own // 2, body(2 * tk), carry)
    carry = lax.fori_loop(j_own - j_own % 2, j_own, body(tk), carry)
    o_ref[0] = _diff_finish(carry, tq, lam_ref, sub_ref, lam_init)


def _diff_sample_kernel(pt_ref, q_ref, *refs, n_pages, page, past, lam_init):
    page_refs, new_ref = refs[:n_pages], refs[n_pages]
    lam_ref, sub_ref, o_ref = refs[n_pages + 1:n_pages + 4]
    tq = q_ref.shape[1]
    rows = 2 * HEADS * tq
    qb = _stack_groups(q_ref[0], _lane_group_masks(GROUP_W, DIFF_QK, 2 * HEADS)).astype(BF16)
    qpos = _qpos(past, tq, rows)
    kt_ref, vt_ref = refs[n_pages + 4:]
    _gather_context(page_refs, new_ref, kt_ref, vt_ref, page)
    s = _dot(qb, kt_ref[...]) * (DIFF_QK ** -0.5)
    mask = _kpos(0, kt_ref.shape[1], rows) <= qpos
    carry = _softmax_context(s, mask, vt_ref[...])
    o_ref[0] = _diff_finish(carry, tq, lam_ref, sub_ref, lam_init)


def _topk_mask_t(score_t, k):
    row = lax.broadcasted_iota(jnp.int32, score_t.shape, 0)
    rank = jnp.zeros(score_t.shape, F32)
    for m in range(score_t.shape[0]):
        sm = score_t[m:m + 1, :]
        tie = jnp.where(row > m, 1.0, 0.0)
        rank = rank + jnp.where(sm > score_t, 1.0, jnp.where(sm == score_t, tie, 0.0))
    return jnp.where(rank < k, 1.0, 0.0)


def _rows_from_t(mask_t):
    n, cols = mask_t.shape
    return jnp.concatenate([mask_t, jnp.zeros((LANES - n, cols), F32)], axis=0).T


def _moba_select(qb, km_t, own, n_blocks):
    gate = _dot(qb, km_t.astype(BF16))
    rows = qb.shape[0]
    lane = lax.broadcasted_iota(jnp.int32, (rows, LANES), 1)
    rank = jnp.zeros((rows, LANES), F32)
    for m in range(n_blocks):
        gm = gate[:, m:m + 1]
        beats = (gm > gate) | ((gm == gate) & (m < lane))
        rank = rank + jnp.where(beats & (m < own), 1.0, 0.0)
    return jnp.where((rank < MOBA_TOPK) & (lane < own), 1.0, 0.0)


def _block_mean_col(kt_f32, m):
    lane = lax.broadcasted_iota(jnp.int32, (GROUP_W, LANES), 1)
    return jnp.where(lane == m, jnp.sum(kt_f32, axis=-1, keepdims=True), 0.0)


def _moba_select_t(qb, km_rows, own):
    gate_t = _dot_t(km_rows.astype(BF16), qb)
    past = lax.broadcasted_iota(jnp.int32, gate_t.shape, 0) < own
    pick_t = _topk_mask_t(jnp.where(past, gate_t, -jnp.inf), MOBA_TOPK)
    return _rows_from_t(jnp.where(past, pick_t, 0.0))


def _moba_kernel(q_ref, kv_ref, o_ref, km_ref, *, tq, tk, pos0, n_blocks):
    i = pl.program_id(1)
    p_lo = pos0 + i * tq
    rows = HEADS * tq
    masks = _lane_group_masks(GROUP_W, HD, HEADS)

    @pl.when(i == 0)
    def _():
        km = jnp.zeros((GROUP_W, LANES), F32)
        for m in range(n_blocks):
            km = km + _block_mean_col(kv_ref[0, 0:GROUP_W, m * MOBA_BLOCK:(m + 1) * MOBA_BLOCK], m)
        km_ref[...] = (km * (1.0 / MOBA_BLOCK)).T

    qb = _stack_groups(q_ref[0], masks).astype(BF16)
    qpos = _qpos(p_lo, tq, rows)
    own = p_lo // MOBA_BLOCK
    sel = _moba_select_t(qb, km_ref[0:16, :], own)

    kt, vt = _tile_from_ref(kv_ref, (0,), own, tk)
    carry = _flash_step(qb, kt, vt, _kpos(own * tk, tk, rows) <= qpos, HD ** -0.5, _flash_init(rows))

    def body(n):
        def step(j, carry):
            kt, vt = _tile_from_ref(kv_ref, (0,), j, n * tk)
            picks = [jnp.broadcast_to(_col(sel, n * j + k), (rows, tk)) for k in range(n)]
            mask = (picks[0] if n == 1 else jnp.concatenate(picks, axis=1)) > 0.5
            return _flash_step(qb, kt, vt, mask, HD ** -0.5, carry)
        return step

    carry = lax.fori_loop(0, own // 2, body(2), carry)
    _, l, acc = lax.fori_loop(own - own % 2, own, body(1), carry)
    o_ref[0] = _unstack_heads(acc / l, tq, masks)


def _moba_sample_kernel(pt_ref, q_ref, *refs, n_pages, page, past):
    page_refs, new_ref, o_ref = refs[:n_pages], refs[n_pages], refs[n_pages + 1]
    tq = q_ref.shape[1]
    rows = HEADS * tq
    per_blk = MOBA_BLOCK // page
    own = past // MOBA_BLOCK
    masks = _lane_group_masks(GROUP_W, HD, HEADS)
    km = jnp.zeros((GROUP_W, LANES), F32)
    for p in range(n_pages):
        km = km + _block_mean_col(page_refs[p][0, 0, 0:GROUP_W, :], p // per_blk)
    km = km * (1.0 / MOBA_BLOCK)
    qb = _stack_groups(q_ref[0], masks).astype(BF16)
    qpos = _qpos(past, tq, rows)
    sel = _moba_select(qb, km, own, own)
    kt_ref, vt_ref = refs[n_pages + 2:]
    _gather_context(page_refs, new_ref, kt_ref, vt_ref, page)
    s = _dot(qb, kt_ref[...]) * (HD ** -0.5)
    ctx = kt_ref.shape[1]
    picked = [jnp.broadcast_to(sel[:, m:m + 1], (rows, MOBA_BLOCK)) for m in range(own)]
    picked.append(jnp.where(_kpos(past, ctx - past, rows) <= qpos, 1.0, 0.0))
    _, l, acc = _softmax_context(s, jnp.concatenate(picked, axis=1) > 0.5, vt_ref[...])
    o_ref[0] = _unstack_heads(acc / l, tq, masks)


def _gelu_tanh(x):
    return 0.5 * x * (1.0 + jnp.tanh(math.sqrt(2.0 / math.pi) * (x + 0.044715 * (x * x * x))))


def _phase_rows(tile_t):
    r = lax.broadcasted_iota(jnp.int32, (LANES, LANES), 0)
    c = lax.broadcasted_iota(jnp.int32, (LANES, LANES), 1)
    perm = jnp.where(c == NSA_STRIDE * (r % 8) + r // 8, 1.0, 0.0).astype(BF16)
    hi = tile_t.astype(BF16)
    lo = (tile_t - hi.astype(F32)).astype(BF16)
    return _dot_t(perm, hi) + _dot_t(perm, lo)


def _nsa_compress(rows_refs, w1_ref, b1_ref, pe_ref, w2k_ref, w2v_ref, ck_refs, cv_refs, n_cmp):
    row = lax.broadcasted_iota(jnp.int32, (n_cmp, 1), 0)
    tokens = []
    for rows_ref in rows_refs:
        first, second = [], []
        for l in range(NSA_STRIDE):
            x = jnp.concatenate([rows_ref[t * LANES + l * 8:t * LANES + l * 8 + 8, :] for t in range(n_cmp // 8)],
                                axis=0)
            xs = jnp.where(row == n_cmp - 1, 0.0, pltpu.roll(x, n_cmp - 1, 0))
            first.append((x + pe_ref[l:l + 1, :]).astype(BF16))
            second.append((xs + pe_ref[l + NSA_STRIDE:l + NSA_STRIDE + 1, :]).astype(BF16))
        tokens.append(jnp.concatenate(first + second, axis=1))
    hid = _dot(jnp.concatenate(tokens, axis=0), w1_ref[...].reshape(NSA_CMP_LEN * LANES, 2 * NSA_CMP_HID))
    hid = _gelu_tanh(hid + b1_ref[...]).astype(BF16)
    ck, cv = _dot(hid, w2k_ref[...]), _dot(hid, w2v_ref[...])
    for k, (ck_ref, cv_ref) in enumerate(zip(ck_refs, cv_refs)):
        ck_ref[...] = jnp.zeros_like(ck_ref)
        cv_ref[...] = jnp.zeros_like(cv_ref)
        ck_ref[0:n_cmp, :] = ck[k * n_cmp:(k + 1) * n_cmp]
        cv_ref[0:n_cmp, :] = cv[k * n_cmp:(k + 1) * n_cmp]


def _nsa_select_t(p_sum, p_first, n_sel):
    tq, ncp = p_sum.shape
    n_rows = -(-n_sel // 8) * 8
    gi = lax.broadcasted_iota(jnp.int32, (n_rows, ncp), 0)
    gj = lax.broadcasted_iota(jnp.int32, (n_rows, ncp), 1) // (NSA_SEL_BLOCK // NSA_STRIDE)
    group_t = jnp.where(gi == gj, 1.0, 0.0).astype(BF16)
    hi = p_sum.astype(BF16)
    lo = (p_sum - hi.astype(F32)).astype(BF16)
    imp_t = _dot_t(group_t, hi) + _dot_t(group_t, lo)
    row = lax.broadcasted_iota(jnp.int32, (n_rows, tq), 0)
    qblk = (p_first + lax.broadcasted_iota(jnp.int32, (n_rows, tq), 1)) // NSA_SEL_BLOCK
    valid = (row <= qblk) & (row < n_sel)
    forced = (row == 0) | (row == qblk) | (row == qblk - 1)
    score_t = jnp.where(valid, jnp.where(forced, NSA_FORCE, imp_t), -jnp.inf)
    pick_t = _topk_mask_t(score_t, min(NSA_TOPN, n_sel))
    return _rows_from_t(jnp.where(valid, pick_t, 0.0))


def _nsa_cmp_and_select(qb, qpos, ck_ref, cv_ref, tq, n_sel):
    rows = qb.shape[0]
    ncp = ck_ref.shape[0]
    s = _dot_t(qb, ck_ref[...].astype(BF16)) * (HD ** -0.5)
    c_end = lax.broadcasted_iota(jnp.int32, (rows, ncp), 1) * NSA_STRIDE + (NSA_CMP_LEN - 1)
    c_mask = c_end <= qpos
    s = jnp.where(c_mask, s, NEG_BIG)
    p = jnp.where(c_mask, jnp.exp(s - jnp.max(s, axis=-1, keepdims=True)), 0.0)
    den = jnp.sum(p, axis=-1, keepdims=True)
    p = p / jnp.where(den > 0, den, 1.0)
    o_cmp = _dot(p.astype(BF16), cv_ref[...].astype(BF16))
    p_sum = p[0:tq]
    for h in range(1, HEADS):
        p_sum = p_sum + p[h * tq:(h + 1) * tq]
    if tq % LANES == 0:
        return o_cmp, _nsa_select_t(p_sum, qpos[0:1, :], n_sel)
    gi = lax.broadcasted_iota(jnp.int32, (ncp, LANES), 0)
    gj = lax.broadcasted_iota(jnp.int32, (ncp, LANES), 1)
    group_mat = jnp.where((gi // (NSA_SEL_BLOCK // NSA_STRIDE)) == gj, 1.0, 0.0).astype(BF16)
    imp = _split_dot(p_sum, group_mat)
    qblk = qpos[0:tq] // NSA_SEL_BLOCK
    lane = lax.broadcasted_iota(jnp.int32, (tq, LANES), 1)
    valid = (lane <= qblk) & (lane < n_sel)
    forced = (lane == 0) | (lane == qblk) | (lane == qblk - 1)
    score = jnp.where(valid, jnp.where(forced, NSA_FORCE, imp), -jnp.inf)
    rank = jnp.zeros((tq, LANES), F32)
    for m in range(n_sel):
        sm = score[:, m:m + 1]
        beats = (sm > score) | ((sm == score) & (m < lane))
        rank = rank + jnp.where(beats, 1.0, 0.0)
    sel = jnp.where((rank < min(NSA_TOPN, n_sel)) & valid, 1.0, 0.0)
    return o_cmp, sel


def _shared_tile(k64, v64):
    return (jnp.concatenate([k64] * HEADS, axis=0).astype(BF16),
            jnp.concatenate([v64] * HEADS, axis=0).astype(BF16))


def _slc_mask(sel_bf, blk0, kpos0, tk, qpos, rows, causal):
    ei = lax.broadcasted_iota(jnp.int32, (LANES, tk), 0)
    ej = lax.broadcasted_iota(jnp.int32, (LANES, tk), 1) // NSA_SEL_BLOCK
    expand = jnp.where(ei == ej + blk0, 1.0, 0.0).astype(BF16)
    picked = _dot(sel_bf, expand)
    picked = jnp.concatenate([picked] * HEADS, axis=0) > 0.5
    return picked & (_kpos(kpos0, tk, rows) <= qpos) if causal else picked


def _win_mask(kpos0, tk, qpos, rows):
    kp = _kpos(kpos0, tk, rows)
    return (kp <= qpos) & (kp > qpos - NSA_WINDOW)


def _nsa_combine(g, o_cmp, slc, win, tq, masks):
    o_slc = slc[2] / slc[1]
    o_win = win[2] / win[1]
    out = jnp.zeros((tq, GROUP_W), F32)
    for h in range(HEADS):
        r0, r1 = h * tq, (h + 1) * tq
        mix = (g[:, 3 * h:3 * h + 1] * o_cmp[r0:r1] + g[:, 3 * h + 1:3 * h + 2] * o_slc[r0:r1]
               + g[:, 3 * h + 2:3 * h + 3] * o_win[r0:r1])
        out = out + mix * masks[h]
    return out


def _nsa_kernel(q_ref, nkv_ref, g_ref, wkv_ref, w1_ref, b1_ref, pe_ref, w2k_ref, w2v_ref,
                o_ref, rows_ref, ck_ref, cv_ref, *, tq, tk, pos0, n_cmp, n_sel):
    i = pl.program_id(1)
    p_lo = pos0 + i * tq
    rows = HEADS * tq
    masks = _lane_group_masks(GROUP_W, HD, HEADS)

    @pl.when(i == 0)
    def _():
        for c in range(nkv_ref.shape[2] // LANES):
            rows_ref[c * LANES:(c + 1) * LANES, :] = _phase_rows(nkv_ref[0, 0:LANES, c * LANES:(c + 1) * LANES])
        _nsa_compress([rows_ref], w1_ref, b1_ref, pe_ref, w2k_ref, w2v_ref, [ck_ref], [cv_ref], n_cmp)

    qb = _stack_groups(q_ref[0], masks).astype(BF16)
    qpos = _qpos(p_lo, tq, rows)
    scale = HD ** -0.5
    o_cmp, sel = _nsa_cmp_and_select(qb, qpos, ck_ref, cv_ref, tq, n_sel)
    sel_bf = sel.astype(BF16)
    j_own = (p_lo + tq - 1) // tk

    def slc_step(j, carry, causal):
        cols = pl.ds(pl.multiple_of(j * tk, tk), tk)
        kt, vt = _shared_tile(nkv_ref[0, 2 * HD:3 * HD, cols], nkv_ref[0, 3 * HD:4 * HD, cols])
        mask = _slc_mask(sel_bf, j * (tk // NSA_SEL_BLOCK), j * tk, tk, qpos, rows, causal)
        return _flash_step(qb, kt, vt, mask, scale, carry)

    slc = slc_step(j_own, _flash_init(rows), True)
    slc = lax.fori_loop(0, j_own, lambda j, c: slc_step(j, c, False), slc)

    def win_step(j, carry, mask_fn):
        cols = pl.ds(pl.multiple_of(j * tk, tk), tk)
        kt, vt = _shared_tile(wkv_ref[0, 0:HD, cols], wkv_ref[0, HD:2 * HD, cols])
        return _flash_step(qb, kt, vt, mask_fn(_kpos(j * tk, tk, rows)), scale, carry)

    win = win_step(j_own, _flash_init(rows), lambda kp: kp <= qpos)
    j_lo = jnp.maximum(p_lo - (NSA_WINDOW - 1), 0) // tk
    win = lax.fori_loop(j_lo, j_own, lambda j, c: win_step(j, c, lambda kp: kp > qpos - NSA_WINDOW), win)
    o_ref[0] = _nsa_combine(g_ref[0], o_cmp, slc, win, tq, masks)


def _nsa_sample_kernel(pt_ref, samples, *, n_pages, page, past, n_cmp, n_sel, w_eff):
    scale = HD ** -0.5
    masks = _lane_group_masks(GROUP_W, HD, HEADS)
    ctxs = []
    for refs in samples:
        q_ref, page_refs, new_ref = refs[0], refs[1:1 + n_pages], refs[1 + n_pages]
        (g_ref, wst_ref, wnew_ref, w1_ref, b1_ref, pe_ref, w2k_ref, w2v_ref,
         o_ref, wout_ref, rows_ref, ck_ref, cv_ref, sk_ref, sv_ref, wk_ref, wv_ref) = refs[2 + n_pages:]
        ctxs.append(dict(q_ref=q_ref, page_refs=page_refs, new_ref=new_ref, g_ref=g_ref, wst_ref=wst_ref,
                         wnew_ref=wnew_ref, cmp=(w1_ref, b1_ref, pe_ref, w2k_ref, w2v_ref), o_ref=o_ref,
                         wout_ref=wout_ref,
                         rows_ref=rows_ref, ck_ref=ck_ref, cv_ref=cv_ref, sk_ref=sk_ref, sv_ref=sv_ref,
                         wk_ref=wk_ref, wv_ref=wv_ref))
    tq = ctxs[0]["q_ref"].shape[1]
    rows = HEADS * tq
    qpos = _qpos(past, tq, rows)
    assert tq <= NSA_STRIDE

    for c in ctxs:
        c["rows_ref"][...] = jnp.zeros_like(c["rows_ref"])
        for p in range(n_pages):
            c["rows_ref"][p * page:(p + 1) * page, :] = _phase_rows(c["page_refs"][p][0, 0, 0:LANES, :])
        for t in range(tq):
            c["rows_ref"][n_pages * page + 8 * t:n_pages * page + 8 * t + 1, :] = c["new_ref"][0, t:t + 1, 0:LANES]
    for c in ctxs:
        for p in range(n_pages):
            c["sk_ref"][:, p * page:(p + 1) * page] = c["page_refs"][p][0, 0, 2 * HD:3 * HD, :].astype(BF16)
            c["sv_ref"][:, p * page:(p + 1) * page] = c["page_refs"][p][0, 0, 3 * HD:4 * HD, :].astype(BF16)
        new_t = _rows_to_tile(c["new_ref"][0, :, 2 * HD:4 * HD], page)
        c["sk_ref"][:, n_pages * page:] = new_t[0:HD]
        c["sv_ref"][:, n_pages * page:] = new_t[HD:2 * HD]
        c["wk_ref"][:, 0:w_eff] = c["wst_ref"][0, 0, 0:HD, :].astype(BF16)
        c["wv_ref"][:, 0:w_eff] = c["wst_ref"][0, 0, HD:2 * HD, :].astype(BF16)
        wnew = jnp.concatenate([c["wnew_ref"][0], jnp.zeros((page - tq, 2 * HD), F32)], axis=0).T
        c["wk_ref"][:, w_eff:] = wnew[0:HD].astype(BF16)
        c["wv_ref"][:, w_eff:] = wnew[HD:2 * HD].astype(BF16)
        c["wout_ref"][0] = jnp.concatenate([c["wst_ref"][0, 0], wnew], axis=1)[:, tq:tq + w_eff]
        q = c["q_ref"][0]
        c["qb"] = _stack_groups(q, masks).astype(BF16)
        c["qs"] = jnp.concatenate([q[:, h * HD:(h + 1) * HD] for h in range(HEADS)], axis=0).astype(BF16)
    for c in ctxs:
        wmask = _win_mask(past - w_eff, c["wk_ref"].shape[1], qpos, rows)
        c["win"] = _softmax_context(_dot(c["qs"], c["wk_ref"][...]) * scale, wmask, c["wv_ref"][...])
    _nsa_compress([c["rows_ref"] for c in ctxs], *ctxs[0]["cmp"], [c["ck_ref"] for c in ctxs],
                  [c["cv_ref"] for c in ctxs], n_cmp)
    for c in ctxs:
        c["o_cmp"], c["sel"] = _nsa_cmp_and_select(c["qb"], qpos, c["ck_ref"], c["cv_ref"], tq, n_sel)
    ctx = ctxs[0]["sk_ref"].shape[1]
    ei = lax.broadcasted_iota(jnp.int32, (LANES, ctx), 0)
    ej = lax.broadcasted_iota(jnp.int32, (LANES, ctx), 1) // NSA_SEL_BLOCK
    expand = jnp.where(ei == ej, 1.0, 0.0).astype(BF16)
    causal = _kpos(0, ctx, rows) <= qpos
    widen = lambda t: (t[0], t[1], jnp.concatenate([t[2]] * HEADS, axis=1))
    for c in ctxs:
        picked = _dot(c["sel"].astype(BF16), expand)
        mask = (jnp.concatenate([picked] * HEADS, axis=0) > 0.5) & causal
        slc = _softmax_context(_dot(c["qs"], c["sk_ref"][...]) * scale, mask, c["sv_ref"][...])
        c["o_ref"][0] = _nsa_combine(c["g_ref"][0], c["o_cmp"], widen(slc), widen(c["win"]), tq, masks)


def _nsa_cmp_weights(pe, w1, b1, w2):
    z = jnp.zeros((NSA_CMP_LEN, HD, NSA_CMP_HID), F32)
    w1cat = jnp.concatenate([jnp.concatenate([w1[0], z], axis=2), jnp.concatenate([z, w1[1]], axis=2)], axis=1)
    b1cat = jnp.concatenate([b1[0], b1[1]])[None, :]
    pecat = jnp.concatenate([pe[0], pe[1]], axis=1)
    z2 = jnp.zeros((NSA_CMP_HID, GROUP_W), F32)
    w2k4 = jnp.concatenate([jnp.tile(w2[0], (1, HEADS)), z2], axis=0)
    w2v4 = jnp.concatenate([z2, jnp.tile(w2[1], (1, HEADS))], axis=0)
    return w1cat.astype(BF16), b1cat, pecat, w2k4.astype(BF16), w2v4.astype(BF16)


def _cmp_specs(index_map):
    full = lambda *shape: pl.BlockSpec(shape, lambda *a: (0,) * len(shape))
    return [full(NSA_CMP_LEN, LANES, 2 * NSA_CMP_HID), full(1, 2 * NSA_CMP_HID),
            full(NSA_CMP_LEN, LANES), full(2 * NSA_CMP_HID, GROUP_W), full(2 * NSA_CMP_HID, GROUP_W)]


def _layer_norm(x, g, b):
    xc = x - jnp.mean(x, axis=-1, keepdims=True)
    var = jnp.mean(xc * xc, axis=-1, keepdims=True)
    return xc * lax.rsqrt(var + LN_EPS) * g + b


def _outproj_kernel(x_ref, a_ref, b_ref, c_ref, d_ref, w_ref, g_ref, be_ref, o_ref):
    mix = _dot(a_ref[...].astype(BF16), w_ref[0:256, :])
    mix = mix + _dot(b_ref[...].astype(BF16), w_ref[256:512, :])
    mix = mix + _dot(c_ref[...].astype(BF16), w_ref[512:768, :])
    mix = mix + _dot(d_ref[...].astype(BF16), w_ref[768:1024, :])
    o_ref[...] = _layer_norm(ALPHA * x_ref[...] + mix, g_ref[...], be_ref[...])


def _outproj_norm(x2d, outs, w_out_bf, gain, bias, tm):
    n = x2d.shape[0]
    row = lambda w: pl.BlockSpec((tm, w), lambda r: (r, 0))
    return pl.pallas_call(
        _outproj_kernel,
        grid=(n // tm,),
        in_specs=[row(D_MODEL), row(GROUP_W), row(GROUP_W), row(GROUP_W), row(GROUP_W),
                  pl.BlockSpec((D_MODEL, D_MODEL), lambda r: (0, 0)),
                  pl.BlockSpec((1, D_MODEL), lambda r: (0, 0)), pl.BlockSpec((1, D_MODEL), lambda r: (0, 0))],
        out_specs=row(D_MODEL),
        out_shape=jax.ShapeDtypeStruct((n, D_MODEL), F32),
        compiler_params=_params(1),
        name="outproj_ln",
    )(x2d, *outs, w_out_bf, gain, bias)


def _moe_kernel(h_ref, wr_ref, rb_ref, wgu_ref, wd_ref, g_ref, be_ref, o_ref, hb_ref, comb_ref, acc_ref):
    e = pl.program_id(1)

    @pl.when(e == 0)
    def _():
        hb = h_ref[...].astype(BF16)
        hb_ref[...] = hb
        logit = _dot(hb, wr_ref[...]) + rb_ref[...]
        lane = lax.broadcasted_iota(jnp.int32, logit.shape, 1)
        is_g = lane < N_GROUPS
        lg = jnp.where(is_g, logit, NEG_BIG)
        gmax = jnp.max(lg, axis=-1, keepdims=True)
        gsel = jnp.min(jnp.where(is_g & (lg == gmax), lane, LANES), axis=-1, keepdims=True)
        pg = 1.0 / jnp.sum(jnp.where(is_g, jnp.exp(lg - gmax), 0.0), axis=-1, keepdims=True)
        e0 = N_GROUPS + gsel * EXPERTS_PER_GROUP
        in_g = (lane >= e0) & (lane < e0 + EXPERTS_PER_GROUP)
        le = jnp.where(in_g, logit, NEG_BIG)
        v1 = jnp.max(le, axis=-1, keepdims=True)
        i1 = jnp.min(jnp.where(in_g & (le == v1), lane, LANES), axis=-1, keepdims=True)
        rest = in_g & (lane != i1)
        le2 = jnp.where(rest, logit, NEG_BIG)
        v2 = jnp.max(le2, axis=-1, keepdims=True)
        i2 = jnp.min(jnp.where(rest & (le2 == v2), lane, LANES), axis=-1, keepdims=True)
        ex = jnp.exp(v2 - v1)
        w1 = pg / (1.0 + ex)
        w2 = pg * ex / (1.0 + ex)
        comb_ref[...] = jnp.where(lane == i1, w1, 0.0) + jnp.where(lane == i2, w2, 0.0)
        acc_ref[...] = jnp.zeros_like(acc_ref)

    hids = []
    for k in range(EXPERTS_PER_STEP):
        gu = _dot(hb_ref[...], wgu_ref[k])
        gt, up = gu[:, :EXPERT_HID], gu[:, EXPERT_HID:]
        hid = gt / (1.0 + jnp.exp(-gt)) * up
        hids.append((hid * _col(comb_ref[...], N_GROUPS + e * EXPERTS_PER_STEP + k)).astype(BF16))
    acc_ref[...] += _dot(jnp.concatenate(hids, axis=1), wd_ref[...].reshape(EXPERTS_PER_STEP * EXPERT_HID, D_MODEL))

    @pl.when(e == N_EXPERTS // EXPERTS_PER_STEP - 1)
    def _():
        o_ref[...] = _layer_norm(ALPHA * h_ref[...] + acc_ref[...], g_ref[...], be_ref[...])


def _moe_norm(h2d, wr, rb, wgu, wd, gain, bias, tm):
    n = h2d.shape[0]
    assert n % tm == 0
    return pl.pallas_call(
        _moe_kernel,
        grid=(n // tm, N_EXPERTS // EXPERTS_PER_STEP),
        in_specs=[pl.BlockSpec((tm, D_MODEL), lambda r, e: (r, 0)),
                  pl.BlockSpec((D_MODEL, LANES), lambda r, e: (0, 0)),
                  pl.BlockSpec((1, LANES), lambda r, e: (0, 0)),
                  pl.BlockSpec((EXPERTS_PER_STEP, D_MODEL, 2 * EXPERT_HID), lambda r, e: (e, 0, 0)),
                  pl.BlockSpec((EXPERTS_PER_STEP, EXPERT_HID, D_MODEL), lambda r, e: (e, 0, 0)),
                  pl.BlockSpec((1, D_MODEL), lambda r, e: (0, 0)),
                  pl.BlockSpec((1, D_MODEL), lambda r, e: (0, 0))],
        out_specs=pl.BlockSpec((tm, D_MODEL), lambda r, e: (r, 0)),
        out_shape=jax.ShapeDtypeStruct((n, D_MODEL), F32),
        scratch_shapes=[pltpu.VMEM((tm, D_MODEL), BF16), pltpu.VMEM((tm, LANES), F32),
                        pltpu.VMEM((tm, D_MODEL), F32)],
        compiler_params=_params(2),
        name="moe_ln",
    )(h2d, wr, rb, wgu, wd, gain, bias)


def _layer_weights(l, w_in, w_out, pe, w1, b1, w2, lam_p, subln, ln_g, ln_b, rgw, rgb, rew, reb, wg, wu, wd):
    wr = jnp.concatenate([rgw[l], rew[l], jnp.zeros((D_MODEL, LANES - N_GROUPS - N_EXPERTS), F32)], axis=1)
    rb = jnp.concatenate([rgb[l], reb[l], jnp.zeros((LANES - N_GROUPS - N_EXPERTS,), F32)])[None, :]
    wq, wkv = _proj_weights(w_in[l])
    return dict(
        wq=wq, wkv=wkv, wkv_t=wkv.T, wall=jnp.concatenate([wq, wkv], axis=1),
        w_out=w_out[l].astype(BF16),
        cmp=_nsa_cmp_weights(pe[l], w1[l], b1[l], w2[l]),
        lam_p=lam_p[l],
        subln4=jnp.tile(subln[l], HEADS)[None, :],
        lam_init=0.8 - 0.6 * math.exp(-0.3 * l),
        g1=ln_g[l, 0][None, :], b1=ln_b[l, 0][None, :], g2=ln_g[l, 1][None, :], b2=ln_b[l, 1][None, :],
        wr=wr.astype(BF16), rb=rb,
        wgu=jnp.concatenate([wg[l], wu[l]], axis=2).astype(BF16),
        wd=wd[l].astype(BF16),
    )


def _ffn(x, outs, lw, tm, tm_ffn):
    b, t, _ = x.shape
    x2d = x.reshape(b * t, D_MODEL)
    h = _outproj_norm(x2d, [o.reshape(b * t, GROUP_W) for o in outs], lw["w_out"], lw["g1"], lw["b1"], tm)
    y = _moe_norm(h, lw["wr"], lw["rb"], lw["wgu"], lw["wd"], lw["g2"], lw["b2"], tm_ffn)
    return y.reshape(b, t, D_MODEL)


def kernel(x_prompt, x_sample, cache_sb_kv, cache_nsa_kv, cache_diff_kv, cache_moba_kv, state_nsa_win, page_table, w_in, w_out, nsa_cmp_pe, nsa_cmp_w1, nsa_cmp_b1, nsa_cmp_w2, diff_lambda, diff_subln, ln_gain, ln_bias, router_group_w, router_group_b, router_expert_w, router_expert_b, expert_w_gate, expert_w_up, expert_w_down):
    bp, tp, _ = x_prompt.shape
    bs, ts, _ = x_sample.shape
    depth = w_in.shape[0]
    n_phys, page = cache_sb_kv.shape[1], cache_sb_kv.shape[2]
    n_pages = page_table.shape[1]
    past = n_pages * page
    w_eff = state_nsa_win.shape[2]
    l_s = past + ts
    l_s_pad = -(-l_s // MOBA_BLOCK) * MOBA_BLOCK
    tm_p = 256
    tm_s = min(256, bs * ts)
    assert tp % KV_TILE == 0 and tm_p % ts == 0 and (bs * ts) % tm_s == 0 and ts <= page
    assert page == LANES and past % MOBA_BLOCK == 0 and w_eff % LANES == 0 and w_eff == min(NSA_WINDOW, past)

    pool_sb = jnp.transpose(cache_sb_kv, (0, 1, 3, 4, 5, 2)).reshape(depth, n_phys, 2 * GROUP_W, page)
    pool_diff = jnp.transpose(cache_diff_kv, (0, 1, 3, 4, 5, 2)).reshape(depth, n_phys, 2 * GROUP_W, page)
    pool_moba = jnp.transpose(cache_moba_kv, (0, 1, 3, 4, 5, 2)).reshape(depth, n_phys, 2 * GROUP_W, page)
    pool_nsa = jnp.transpose(cache_nsa_kv, (0, 1, 3, 4, 2)).reshape(depth, n_phys, 4 * HD, page)
    win_state = jnp.transpose(state_nsa_win, (0, 1, 3, 4, 2)).reshape(depth, bs, 2 * HD, w_eff)

    tabs_p = _rope_tables(jnp.arange(tp, dtype=jnp.int32))
    tabs_s = _rope_tables(past + (jnp.arange(tm_s, dtype=jnp.int32) % ts))
    tabs_p_t, tabs_s_t = jnp.transpose(tabs_p, (0, 2, 1)), jnp.transpose(tabs_s, (0, 2, 1))
    n_cmp_p, n_sel_p = tp // NSA_STRIDE, tp // NSA_SEL_BLOCK
    n_cmp_s, n_sel_s = l_s_pad // NSA_STRIDE, l_s_pad // NSA_SEL_BLOCK
    ncp = lambda n: -(-n // LANES) * LANES

    y_p, y_s = x_prompt, x_sample
    kv_bufs, win_buf, new_s = None, None, []
    for l in range(depth):
        lw = _layer_weights(l, w_in, w_out, nsa_cmp_pe, nsa_cmp_w1, nsa_cmp_b1, nsa_cmp_w2, diff_lambda,
                            diff_subln, ln_gain, ln_bias, router_group_w, router_group_b, router_expert_w,
                            router_expert_b, expert_w_gate, expert_w_up, expert_w_down)
        lam_specs = [pl.BlockSpec((4, DIFF_QK), lambda *a: (0, 0)), pl.BlockSpec((1, GROUP_W), lambda *a: (0, 0))]

        pp = _project(y_p.reshape(bp * tp, D_MODEL), lw["wq"], lw["wkv_t"], tabs_p, tabs_p_t, tm_p,
                      Q_SEGS, KV_SEGS, tp, depth, l, kv_bufs)
        kv_bufs = [pp[s_[0]] for s_ in KV_SEGS]
        q3 = lambda a: a.reshape(bp, tp, a.shape[-1])
        o_sb = _prompt_call(functools.partial(_sb_kernel, tq=256, tk=KV_TILE, pos0=0, chunk=HEADS * 256),
                            q3(pp["sbq"]), pp["sbkv"], l, (), (), 256, "sb_attn",
                            scratch=[pltpu.VMEM((HEADS * 256, GROUP_W), BF16),
                                     pltpu.VMEM((HEADS * 256, GROUP_W), F32),
                                     pltpu.VMEM((HEADS * 256, 1), F32)])
        o_nsa = _prompt_call(
            functools.partial(_nsa_kernel, tq=256, tk=KV_TILE, pos0=0, n_cmp=n_cmp_p, n_sel=n_sel_p),
            q3(pp["nq"]), pp["nkv"], l, (q3(pp["gate"]), pp["wkv"]) + lw["cmp"],
            [pl.BlockSpec((1, 256, LANES), lambda bb, i: (bb, i, 0)),
             pl.BlockSpec((None, 1, 2 * HD, tp), lambda bb, i: (l, bb, 0, 0))] + _cmp_specs(None), 256, "nsa_attn",
            scratch=[pltpu.VMEM((tp, LANES), F32), pltpu.VMEM((ncp(n_cmp_p), GROUP_W), F32),
                     pltpu.VMEM((ncp(n_cmp_p), GROUP_W), F32)])
        o_diff = _prompt_call(
            functools.partial(_diff_kernel, tq=128, tk=KV_TILE, pos0=0, lam_init=lw["lam_init"]),
            q3(pp["dq"]), pp["dkv"], l, (lw["lam_p"], lw["subln4"]), lam_specs, 128, "diff_attn")
        o_moba = _prompt_call(
            functools.partial(_moba_kernel, tq=256, tk=KV_TILE, pos0=0, n_blocks=tp // MOBA_BLOCK),
            q3(pp["mq"]), pp["mkv"], l, (), (), 256, "moba_attn", scratch=[pltpu.VMEM((LANES, GROUP_W), F32)])
        y_p = _ffn(y_p, (o_sb, o_nsa, o_diff, o_moba), lw, tm_p, math.gcd(1024, bp * tp))

        ps = _project(y_s.reshape(bs * ts, D_MODEL), lw["wall"], lw["wkv_t"], tabs_s, tabs_s_t, tm_s,
                      Q_SEGS + KV_SEGS, (), ts)
        s3 = lambda a: a.reshape(bs, ts, a.shape[-1])
        new_tile = lambda name: s3(ps[name])
        common = dict(n_pages=n_pages, page=page, past=past)
        ctx_scratch = lambda ch: [((ch, past + page), BF16), ((ch, past + page), BF16)]
        o_sb = _sample_call(functools.partial(_sb_sample_kernel, **common), page_table, s3(ps["sbq"]),
                            pool_sb, l, new_tile("sbkv"), (), (), (), "sb_attn_s", scratch=ctx_scratch(GROUP_W))
        o_nsa, win_buf = _sample_call(
            functools.partial(_nsa_sample_kernel, n_cmp=n_cmp_s, n_sel=n_sel_s, w_eff=w_eff, **common),
            page_table, s3(ps["nq"]), pool_nsa, l, new_tile("nkv"),
            (s3(ps["gate"]), win_state, new_tile("wkv")) + lw["cmp"],
            [pl.BlockSpec((NSA_SAMPLES_PER_STEP, ts, LANES), lambda bb, pt: (bb, 0, 0)),
             pl.BlockSpec((1, NSA_SAMPLES_PER_STEP, 2 * HD, w_eff), lambda bb, pt: (l, bb, 0, 0)),
             pl.BlockSpec((NSA_SAMPLES_PER_STEP, ts, 2 * HD), lambda bb, pt: (bb, 0, 0))] + _cmp_specs(None),
            (0, 1, 0, None, None, None, None, None), "nsa_attn_s",
            scratch=[((l_s_pad, LANES), F32), ((ncp(n_cmp_s), GROUP_W), F32), ((ncp(n_cmp_s), GROUP_W), F32)]
            + ctx_scratch(HD) + [((HD, w_eff + page), BF16), ((HD, w_eff + page), BF16)],
            spb=NSA_SAMPLES_PER_STEP, staged=True, slab_out=(depth, (2 * HD, w_eff), win_buf))
        o_diff = _sample_call(functools.partial(_diff_sample_kernel, lam_init=lw["lam_init"], **common),
                              page_table, s3(ps["dq"]), pool_diff, l, new_tile("dkv"),
                              (lw["lam_p"], lw["subln4"]), lam_specs, (None, None), "diff_attn_s",
                              scratch=ctx_scratch(GROUP_W))
        o_moba = _sample_call(functools.partial(_moba_sample_kernel, **common), page_table, s3(ps["mq"]),
                              pool_moba, l, new_tile("mkv"), (), (), (), "moba_attn_s",
                              scratch=ctx_scratch(GROUP_W))
        y_s = _ffn(y_s, (o_sb, o_nsa, o_diff, o_moba), lw, tm_s, math.gcd(1024, bs * ts))
        new_s.append((ps["sbkv"], ps["nkv"], ps["dkv"], ps["mkv"]))

    def heads_out(a, b, t):
        return jnp.transpose(a.reshape(depth, b, 2, HEADS, HD, t), (0, 1, 5, 2, 3, 4))

    def slots_out(a, b, t, n):
        return jnp.transpose(a.reshape(depth, b, n, HD, t), (0, 1, 4, 2, 3))

    st_p = [pp["sbkv"], pp["nkv"], pp["dkv"], pp["mkv"], pp["wkv"][:, :, :, tp - min(NSA_WINDOW, tp):]]
    st_s = [jnp.stack([r[k] for r in new_s]) for k in range(4)] + [win_buf]
    outs_p = (heads_out(st_p[0], bp, tp), slots_out(st_p[1], bp, tp, 4), heads_out(st_p[2], bp, tp),
              heads_out(st_p[3], bp, tp), slots_out(st_p[4], bp, min(NSA_WINDOW, tp), 2))
    outs_s = (st_s[0].reshape(depth, bs, ts, 2, HEADS, HD), st_s[1].reshape(depth, bs, ts, 4, HD),
              st_s[2].reshape(depth, bs, ts, 2, HEADS, HD), st_s[3].reshape(depth, bs, ts, 2, HEADS, HD),
              slots_out(st_s[4], bs, w_eff, 2))
    return (y_p, y_s, *outs_p, *outs_s)
```

```python
import functools
import math

import numpy as np
import jax
import jax.numpy as jnp
from jax import lax
from jax.experimental import pallas as pl
from jax.experimental.pallas import tpu as pltpu

F32 = jnp.float32
BF16 = jnp.bfloat16

D_MODEL = 1024
HEADS = 4
HD = 64
GROUP_W = HEADS * HD
DIFF_QK = HD // 2
DEPTH_LAYERS = 2
ROPE_THETA = 500000.0
ROPE_FRACTION = 4
NSA_STRIDE = 16
NSA_CMP_LEN = 32
NSA_CMP_HID = 128
NSA_SEL_BLOCK = 64
NSA_TOPN = 16
NSA_WINDOW = 512
NSA_FORCE = 1.0e4
MOBA_BLOCK = 256
MOBA_TOPK = 3
N_GROUPS = 4
EXPERTS_PER_GROUP = 4
N_EXPERTS = 16
EXPERT_HID = 256
EXPERTS_PER_STEP = 4
ALPHA = (2 * DEPTH_LAYERS) ** 0.25
LN_EPS = 1e-5
RMS_EPS = 1e-5
NEG_BIG = -1e30
LANES = 128
KV_TILE = 256
SAMPLES_PER_STEP = 4
NSA_SAMPLES_PER_STEP = 4
VMEM_LIMIT = 56 * 1024 * 1024

Q_SEGS = (("sbq", 256, (0, 0)), ("nq", 256, (1, 1)), ("dq", 256, (3, 3)), ("mq", 256, (1, 1)),
          ("gate", 128, (4,)))
KV_SEGS = (("sbkv", 512, (0, 0, 0, 0)), ("nkv", 256, (2, 2)), ("wkv", 128, (2,)),
           ("dkv", 512, (3, 3, 0, 0)), ("mkv", 512, (1, 1, 0, 0)))
Q_COLS = sum(s[1] for s in Q_SEGS)
KV_COLS = sum(s[1] for s in KV_SEGS)


def _dot(a, b):
    return jnp.dot(a, b, preferred_element_type=F32)


def _dot_t(a, b):
    return lax.dot_general(a, b, (((1,), (1,)), ((), ())), preferred_element_type=F32)


def _split_dot(a, b_bf):
    hi = a.astype(BF16)
    lo = (a - hi.astype(F32)).astype(BF16)
    return _dot(hi, b_bf) + _dot(lo, b_bf)


def _lane_group_masks(width, group, n):
    lane = lax.broadcasted_iota(jnp.int32, (1, width), 1)
    return [jnp.where((lane // group) == g, 1.0, 0.0).astype(F32) for g in range(n)]


def _stack_groups(q, masks):
    return jnp.concatenate([q * m for m in masks], axis=0)


def _col(x, idx):
    lane = lax.broadcasted_iota(jnp.int32, x.shape, 1)
    return jnp.sum(jnp.where(lane == idx, x, 0.0), axis=-1, keepdims=True)


def _params(n_axes):
    return pltpu.CompilerParams(dimension_semantics=("arbitrary",) * n_axes, vmem_limit_bytes=VMEM_LIMIT)


def _rope_rows(y, tab_ref, kind):
    half = 4 if kind == 3 else 8
    t0 = (kind - 1) * 3
    return (y * tab_ref[t0] + pltpu.roll(y, LANES - half, 1) * tab_ref[t0 + 1]
            + pltpu.roll(y, half, 1) * tab_ref[t0 + 2])


def _rope_cols(y, tab_ref, kind):
    half = 4 if kind == 3 else 8
    t0 = (kind - 1) * 3
    return (y * tab_ref[t0] + pltpu.roll(y, LANES - half, 0) * tab_ref[t0 + 1]
            + pltpu.roll(y, half, 0) * tab_ref[t0 + 2])


def _proj_kernel(x_ref, w_ref, wt_ref, tab_ref, tabt_ref, *out_refs, row_segs, col_segs, n_carried):
    out_refs = out_refs[n_carried:]
    xb = x_ref[...].astype(BF16)
    off = 0
    n_row = len(row_segs)
    for (_, width, kinds), o_ref in zip(row_segs, out_refs[:n_row]):
        for c0 in range(0, width, 256):
            cw = min(256, width - c0)
            y2 = _dot(xb, w_ref[:, off + c0:off + c0 + cw])
            for cc in range(cw // LANES):
                y = y2[:, cc * LANES:(cc + 1) * LANES]
                kind = kinds[(c0 // LANES) + cc]
                if kind in (1, 2, 3):
                    y = _rope_rows(y, tab_ref, kind)
                elif kind == 4:
                    y = 1.0 / (1.0 + jnp.exp(-y))
                o_ref[:, c0 + cc * LANES:c0 + (cc + 1) * LANES] = y
        off += width
    off = 0
    for (_, width, kinds), o_ref in zip(col_segs, out_refs[n_row:]):
        for c0 in range(0, width, 256):
            cw = min(256, width - c0)
            y2 = _dot_t(wt_ref[off + c0:off + c0 + cw, :], xb)
            for cc in range(cw // LANES):
                y = y2[cc * LANES:(cc + 1) * LANES, :]
                kind = kinds[(c0 // LANES) + cc]
                if kind in (1, 2, 3):
                    y = _rope_cols(y, tabt_ref, kind)
                o_ref[0, c0 + cc * LANES:c0 + (cc + 1) * LANES, :] = y
        off += width


def _project(x2d, w_rows, w_cols, tabs, tabs_t, tm, row_segs, col_segs, seq, depth=1, layer=0, carried=None):
    n = x2d.shape[0]
    carried = list(carried or [])
    n_tab = tabs.shape[1] // tm
    per_seq = max(seq // tm, 1)
    in_specs = [pl.BlockSpec((tm, D_MODEL), lambda r: (r, 0)),
                pl.BlockSpec(w_rows.shape, lambda r: (0, 0)),
                pl.BlockSpec(w_cols.shape, lambda r: (0, 0)),
                pl.BlockSpec((9, tm, LANES), lambda r: (0, r % n_tab, 0)),
                pl.BlockSpec((9, LANES, tm), lambda r: (0, 0, r % n_tab))]
    out_specs = [pl.BlockSpec((tm, s[1]), lambda r: (r, 0)) for s in row_segs]
    in_specs += [pl.BlockSpec(memory_space=pl.ANY)] * len(carried)
    out_specs += [pl.BlockSpec((None, 1, s[1], tm), lambda r: (layer, r // per_seq, 0, r % per_seq))
                  for s in col_segs]
    out_shape = [jax.ShapeDtypeStruct((n, s[1]), F32) for s in row_segs]
    out_shape += [jax.ShapeDtypeStruct((depth, n // seq, s[1], seq), F32) for s in col_segs]
    outs = pl.pallas_call(
        functools.partial(_proj_kernel, row_segs=row_segs, col_segs=col_segs, n_carried=len(carried)),
        grid=(n // tm,),
        in_specs=in_specs, out_specs=out_specs, out_shape=out_shape,
        input_output_aliases={5 + k: len(row_segs) + k for k in range(len(carried))},
        compiler_params=_params(1),
        name="proj",
    )(x2d, w_rows, w_cols, tabs, tabs_t, *carried)
    return dict(zip([s[0] for s in row_segs + col_segs], outs))


def _rope_tables(pos):
    pos = pos.astype(F32)
    lane = np.arange(LANES)
    tabs = []
    for dim, first_only in ((HD, False), (HD, True), (DIFF_QK, False)):
        rot = dim // ROPE_FRACTION
        half = rot // 2
        inv = jnp.power(jnp.float32(ROPE_THETA), -jnp.arange(half, dtype=F32) * (2.0 / rot))
        ang = pos[:, None] * inv[None, :]
        cos, sin = jnp.cos(ang), jnp.sin(ang)
        c = lane % dim
        in_rot = c < rot
        if first_only:
            in_rot = in_rot & (lane < HD)
        first = in_rot & (c < half)
        second = in_rot & (c >= half)
        fidx = c % half
        cos_l, sin_l = cos[:, fidx], sin[:, fidx]
        tabs.append(jnp.where(in_rot[None, :], cos_l, 1.0))
        tabs.append(jnp.where(first[None, :], -sin_l, 0.0))
        tabs.append(jnp.where(second[None, :], sin_l, 0.0))
    return jnp.stack(tabs).astype(F32)


def _proj_weights(w_in_l):
    o = [int(v) for v in np.cumsum((0, 256, 256, 256, 256, 256, 128, 12, 256, 256, 256, 256, 256, 256))]
    sl = lambda a, b: w_in_l[:, a:b]
    gate = jnp.pad(sl(o[6], o[7]), ((0, 0), (0, LANES - 12)))
    wq = jnp.concatenate([sl(o[0], o[1]), sl(o[3], o[4]), sl(o[7], o[8]), sl(o[10], o[11]), gate], axis=1)
    wkv = jnp.concatenate([sl(o[1], o[3]), sl(o[4], o[5]), sl(o[5], o[6]), sl(o[8], o[10]), sl(o[11], o[13])],
                          axis=1)
    return wq.astype(BF16), wkv.astype(BF16)


def _qpos(p_lo, tq, rows):
    r = lax.broadcasted_iota(jnp.int32, (rows, 1), 0)
    return p_lo + (r & (tq - 1))


def _kpos(kpos0, tk, rows):
    return kpos0 + lax.broadcasted_iota(jnp.int32, (rows, tk), 1)


def _flash_step(qb, kt, vt, mask, scale, carry):
    m, l, acc = carry
    s = _dot(qb, kt) * scale
    if mask is not None:
        s = jnp.where(mask, s, NEG_BIG)
    m_new = jnp.maximum(m, jnp.max(s, axis=-1, keepdims=True))
    a = jnp.exp(m - m_new)
    p = jnp.exp(s - m_new)
    l = a * l + jnp.sum(p, axis=-1, keepdims=True)
    acc = a * acc + _dot_t(p.astype(BF16), vt)
    return m_new, l, acc


def _flash_init(rows):
    return (jnp.full((rows, 1), NEG_BIG, F32), jnp.zeros((rows, 1), F32), jnp.zeros((rows, GROUP_W), F32))


def _unstack_heads(rows_val, tq, masks):
    out = rows_val[0:tq] * masks[0]
    for h in range(1, HEADS):
        out = out + rows_val[h * tq:(h + 1) * tq] * masks[h]
    return out


def _later_matrix(tk):
    r_i = lax.broadcasted_iota(jnp.int32, (tk, tk), 0)
    c_i = lax.broadcasted_iota(jnp.int32, (tk, tk), 1)
    return jnp.where(r_i > c_i, 1.0, 0.0).astype(BF16)


def _tile_from_ref(ref, idx, j, tk):
    cols = pl.ds(pl.multiple_of(j * tk, tk), tk)
    return (ref[idx + (slice(0, GROUP_W), cols)].astype(BF16),
            ref[idx + (slice(GROUP_W, 2 * GROUP_W), cols)].astype(BF16))


def _page_tile(ref):
    idx = (0,) * (len(ref.shape) - 2)
    return (ref[idx + (slice(0, GROUP_W), slice(None))].astype(BF16),
            ref[idx + (slice(GROUP_W, 2 * GROUP_W), slice(None))].astype(BF16))


def _rows_to_tile(rows, page):
    t, w = rows.shape
    return jnp.concatenate([rows, jnp.zeros((page - t, w), F32)], axis=0).T.astype(BF16)


def _gather_context(page_refs, new_ref, kt_ref, vt_ref, page):
    for p, ref in enumerate(page_refs):
        kt, vt = _page_tile(ref)
        kt_ref[:, p * page:(p + 1) * page] = kt
        vt_ref[:, p * page:(p + 1) * page] = vt
    new_t = _rows_to_tile(new_ref[0], page)
    p = len(page_refs)
    kt_ref[:, p * page:(p + 1) * page] = new_t[0:GROUP_W]
    vt_ref[:, p * page:(p + 1) * page] = new_t[GROUP_W:2 * GROUP_W]


def _softmax_context(s, mask, vt):
    s = jnp.where(mask, s, NEG_BIG)
    m = jnp.max(s, axis=-1, keepdims=True)
    p = jnp.exp(s - m)
    return m, jnp.sum(p, axis=-1, keepdims=True), _dot_t(p.astype(BF16), vt)


def _sb_tile(qb_ref, acc_ref, tail_ref, kt, vt, later_mat, kpos0, p_lo, tq, causal, chunk):
    rows, tk = qb_ref.shape[0], kt.shape[1]
    for r0 in range(0, rows, chunk):
        rs = slice(r0, r0 + chunk)
        z = _dot(qb_ref[rs, :], kt)
        log_sig = jnp.minimum(z, 0.0) - jnp.log(1.0 + jnp.exp(-jnp.abs(z)))
        log_fail = log_sig - z
        if causal:
            qpos = p_lo + ((r0 + lax.broadcasted_iota(jnp.int32, (chunk, 1), 0)) & (tq - 1))
            mask = _kpos(kpos0, tk, chunk) < qpos
            log_fail = jnp.where(mask, log_fail, 0.0)
        later = _split_dot(log_fail, later_mat) + tail_ref[rs, :]
        w = jnp.exp(log_sig + later)
        if causal:
            w = jnp.where(mask, w, 0.0)
        acc_ref[rs, :] += _dot_t(w.astype(BF16), vt)
        tail_ref[rs, :] += jnp.sum(log_fail, axis=-1, keepdims=True)


def _sb_kernel(q_ref, kv_ref, o_ref, qb_ref, acc_ref, tail_ref, *, tq, tk, pos0, chunk):
    i = pl.program_id(1)
    p_lo = pos0 + i * tq
    masks = _lane_group_masks(GROUP_W, HD, HEADS)
    qb_ref[...] = _stack_groups(q_ref[0] * (HD ** -0.5), masks).astype(BF16)
    acc_ref[...] = jnp.zeros_like(acc_ref)
    tail_ref[...] = jnp.zeros_like(tail_ref)
    later_mat = _later_matrix(tk)
    j_own = (p_lo + tq - 1) // tk
    kt, vt = _tile_from_ref(kv_ref, (0,), j_own, tk)
    _sb_tile(qb_ref, acc_ref, tail_ref, kt, vt, later_mat, j_own * tk, p_lo, tq, True, chunk)

    def body(jj, _):
        j = j_own - 1 - jj
        kt, vt = _tile_from_ref(kv_ref, (0,), j, tk)
        _sb_tile(qb_ref, acc_ref, tail_ref, kt, vt, later_mat, j * tk, p_lo, tq, False, chunk)
        return 0

    lax.fori_loop(0, j_own, body, 0)
    o_ref[0] = _unstack_heads(acc_ref[...], tq, masks)


def _sb_sample_kernel(pt_ref, q_ref, *refs, n_pages, page, past):
    page_refs, new_ref, o_ref = refs[:n_pages], refs[n_pages], refs[n_pages + 1]
    tq = q_ref.shape[1]
    rows = HEADS * tq
    masks = _lane_group_masks(GROUP_W, HD, HEADS)
    kt_ref, vt_ref = refs[n_pages + 2:]
    _gather_context(page_refs, new_ref, kt_ref, vt_ref, page)
    qb = _stack_groups(q_ref[0] * (HD ** -0.5), masks).astype(BF16)
    qpos = _qpos(past, tq, rows)
    later_mat = _later_matrix(page)
    ctx = kt_ref.shape[1]
    z = _dot(qb, kt_ref[...])
    mask = _kpos(0, ctx, rows) < qpos
    log_sig = jnp.minimum(z, 0.0) - jnp.log(1.0 + jnp.exp(-jnp.abs(z)))
    log_fail = jnp.where(mask, log_sig - z, 0.0)
    tail = jnp.zeros((rows, 1), F32)
    later = [None] * (n_pages + 1)
    for p in reversed(range(n_pages + 1)):
        blk = log_fail[:, p * page:(p + 1) * page]
        later[p] = _split_dot(blk, later_mat) + tail
        tail = tail + jnp.sum(blk, axis=-1, keepdims=True)
    w = jnp.where(mask, jnp.exp(log_sig + jnp.concatenate(later, axis=1)), 0.0)
    o_ref[0] = _unstack_heads(_dot_t(w.astype(BF16), vt_ref[...]), tq, masks)


def _prompt_call(kern, q, kv_t, layer, extra, extra_specs, tq, name, scratch=()):
    b, t, _ = q.shape
    rows_kv, l = kv_t.shape[2], kv_t.shape[3]
    return pl.pallas_call(
        kern,
        grid=(b, t // tq),
        in_specs=[pl.BlockSpec((1, tq, GROUP_W), lambda bb, i: (bb, i, 0)),
                  pl.BlockSpec((None, 1, rows_kv, l), lambda bb, i: (layer, bb, 0, 0))] + list(extra_specs),
        out_specs=pl.BlockSpec((1, tq, GROUP_W), lambda bb, i: (bb, i, 0)),
        out_shape=jax.ShapeDtypeStruct((b, t, GROUP_W), F32),
        scratch_shapes=list(scratch),
        compiler_params=_params(2),
        name=name,
    )(q, kv_t, *extra)


def _sample_call(body, page_table, q, pool_t, layer, new_t, extra, extra_specs, extra_axes, name, scratch=(),
                 spb=SAMPLES_PER_STEP, staged=False, slab_out=None):
    b, tq, _ = q.shape
    n_pages = page_table.shape[1]
    rows_kv, page = pool_t.shape[2], pool_t.shape[3]
    assert b % spb == 0

    def page_spec(s_, p):
        return pl.BlockSpec((1, 1, rows_kv, page), lambda bb, pt: (layer, pt[(bb * spb + s_) * n_pages + p], 0, 0))

    axes = [0] + [None] * (spb * n_pages) + [0] + list(extra_axes) + [0]
    n_scratch = len(scratch)
    operands = [page_table.reshape(-1), q] + [pool_t] * (spb * n_pages) + [new_t] + list(extra)
    out_specs = pl.BlockSpec((spb, tq, GROUP_W), lambda bb, pt: (bb, 0, 0))
    out_shape = jax.ShapeDtypeStruct((b, tq, GROUP_W), F32)
    carried_specs, aliases = [], {}
    if slab_out is not None:
        depth, slab_shape, carried = slab_out
        zeros = (0,) * len(slab_shape)
        out_specs = [out_specs, pl.BlockSpec((None, spb) + slab_shape, lambda bb, pt: (layer, bb) + zeros)]
        out_shape = [out_shape, jax.ShapeDtypeStruct((depth, b) + slab_shape, F32)]
        axes.append(0)
        if carried is not None:
            aliases = {len(operands): 1}
            operands.append(carried)
            carried_specs = [pl.BlockSpec(memory_space=pl.ANY)]
    n_carried = len(carried_specs)

    def kern(pt_ref, *refs):
        io, scr = refs[:len(refs) - n_scratch], refs[len(refs) - n_scratch:]
        n_in = len(io) - (1 if slab_out is None else 2)
        io = io[:n_in - n_carried] + io[n_in:]
        per_sample = []
        for s_ in range(spb):
            views = []
            for k, (r, ax) in enumerate(zip(io, axes)):
                if 1 <= k <= spb * n_pages:
                    if (k - 1) // n_pages == s_:
                        views.append(r)
                elif ax is None:
                    views.append(r)
                else:
                    views.append(r.at[(slice(None),) * ax + (pl.ds(s_, 1),)])
            per_sample.append(views + [r.at[s_] for r in scr])
        if staged:
            body(pt_ref, per_sample)
        else:
            for refs_s in per_sample:
                body(pt_ref, *refs_s)

    in_specs = ([pl.BlockSpec((spb, tq, GROUP_W), lambda bb, pt: (bb, 0, 0))]
                + [page_spec(s_, p) for s_ in range(spb) for p in range(n_pages)]
                + [pl.BlockSpec((spb, tq, rows_kv), lambda bb, pt: (bb, 0, 0))] + list(extra_specs)
                + carried_specs)
    return pl.pallas_call(
        kern,
        grid_spec=pltpu.PrefetchScalarGridSpec(
            num_scalar_prefetch=1, grid=(b // spb,), in_specs=in_specs, out_specs=out_specs,
            scratch_shapes=[pltpu.VMEM((spb,) + shape, dt) for shape, dt in scratch]),
        out_shape=out_shape,
        input_output_aliases=aliases,
        compiler_params=_params(1),
        name=name,
    )(*operands)


def _diff_finish(carry, tq, lam_ref, sub_ref, lam_init):
    _, l, acc = carry
    head_masks = _lane_group_masks(GROUP_W, HD, HEADS)
    o = acc / l
    lp = lam_ref[...]
    lam = (jnp.exp(jnp.sum(lp[0:1] * lp[1:2], axis=-1, keepdims=True))
           - jnp.exp(jnp.sum(lp[2:3] * lp[3:4], axis=-1, keepdims=True)) + lam_init)
    out = jnp.zeros((tq, GROUP_W), F32)
    for h in range(HEADS):
        oh = (o[(2 * h) * tq:(2 * h + 1) * tq] - lam * o[(2 * h + 1) * tq:(2 * h + 2) * tq]) * head_masks[h]
        ms = jnp.sum(oh * oh, axis=-1, keepdims=True) * (1.0 / HD)
        out = out + oh * lax.rsqrt(ms + RMS_EPS)
    return out * sub_ref[...] * (1.0 - lam_init)


def _diff_kernel(q_ref, kv_ref, lam_ref, sub_ref, o_ref, *, tq, tk, pos0, lam_init):
    i = pl.program_id(1)
    p_lo = pos0 + i * tq
    rows = 2 * HEADS * tq
    qb = _stack_groups(q_ref[0], _lane_group_masks(GROUP_W, DIFF_QK, 2 * HEADS)).astype(BF16)
    qpos = _qpos(p_lo, tq, rows)
    j_own = (p_lo + tq - 1) // tk
    kt, vt = _tile_from_ref(kv_ref, (0,), j_own, tk)
    carry = _flash_step(qb, kt, vt, _kpos(j_own * tk, tk, rows) <= qpos, DIFF_QK ** -0.5, _flash_init(rows))

    def body(width):
        def step(j, carry):
            kt, vt = _tile_from_ref(kv_ref, (0,), j, width)
            return _flash_step(qb, kt, vt, None, DIFF_QK ** -0.5, carry)
        return step

    carry = lax.fori_loop(0, j_own // 2, body(2 * tk), carry)
    carry = lax.fori_loop(j_own - j_own % 2, j_own, body(tk), carry)
    o_ref[0] = _diff_finish(carry, tq, lam_ref, sub_ref, lam_init)


def _diff_sample_kernel(pt_ref, q_ref, *refs, n_pages, page, past, lam_init):
    page_refs, new_ref = refs[:n_pages], refs[n_pages]
    lam_ref, sub_ref, o_ref = refs[n_pages + 1:n_pages + 4]
    tq = q_ref.shape[1]
    rows = 2 * HEADS * tq
    qb = _stack_groups(q_ref[0], _lane_group_masks(GROUP_W, DIFF_QK, 2 * HEADS)).astype(BF16)
    qpos = _qpos(past, tq, rows)
    kt_ref, vt_ref = refs[n_pages + 4:]
    _gather_context(page_refs, new_ref, kt_ref, vt_ref, page)
    s = _dot(qb, kt_ref[...]) * (DIFF_QK ** -0.5)
    mask = _kpos(0, kt_ref.shape[1], rows) <= qpos
    carry = _softmax_context(s, mask, vt_ref[...])
    o_ref[0] = _diff_finish(carry, tq, lam_ref, sub_ref, lam_init)


def _topk_mask_t(score_t, k):
    row = lax.broadcasted_iota(jnp.int32, score_t.shape, 0)
    rank = jnp.zeros(score_t.shape, F32)
    for m in range(score_t.shape[0]):
        sm = score_t[m:m + 1, :]
        tie = jnp.where(row > m, 1.0, 0.0)
        rank = rank + jnp.where(sm > score_t, 1.0, jnp.where(sm == score_t, tie, 0.0))
    return jnp.where(rank < k, 1.0, 0.0)


def _rows_from_t(mask_t):
    n, cols = mask_t.shape
    return jnp.concatenate([mask_t, jnp.zeros((LANES - n, cols), F32)], axis=0).T


def _moba_select(qb, km_t, own, n_blocks):
    gate = _dot(qb, km_t.astype(BF16))
    rows = qb.shape[0]
    lane = lax.broadcasted_iota(jnp.int32, (rows, LANES), 1)
    rank = jnp.zeros((rows, LANES), F32)
    for m in range(n_blocks):
        gm = gate[:, m:m + 1]
        beats = (gm > gate) | ((gm == gate) & (m < lane))
        rank = rank + jnp.where(beats & (m < own), 1.0, 0.0)
    return jnp.where((rank < MOBA_TOPK) & (lane < own), 1.0, 0.0)


def _block_mean_col(kt_f32, m):
    lane = lax.broadcasted_iota(jnp.int32, (GROUP_W, LANES), 1)
    return jnp.where(lane == m, jnp.sum(kt_f32, axis=-1, keepdims=True), 0.0)


def _moba_select_t(qb, km_rows, own):
    gate_t = _dot_t(km_rows.astype(BF16), qb)
    past = lax.broadcasted_iota(jnp.int32, gate_t.shape, 0) < own
    pick_t = _topk_mask_t(jnp.where(past, gate_t, -jnp.inf), MOBA_TOPK)
    return _rows_from_t(jnp.where(past, pick_t, 0.0))


def _moba_kernel(q_ref, kv_ref, o_ref, km_ref, *, tq, tk, pos0, n_blocks):
    i = pl.program_id(1)
    p_lo = pos0 + i * tq
    rows = HEADS * tq
    masks = _lane_group_masks(GROUP_W, HD, HEADS)

    @pl.when(i == 0)
    def _():
        km = jnp.zeros((GROUP_W, LANES), F32)
        for m in range(n_blocks):
            km = km + _block_mean_col(kv_ref[0, 0:GROUP_W, m * MOBA_BLOCK:(m + 1) * MOBA_BLOCK], m)
        km_ref[...] = (km * (1.0 / MOBA_BLOCK)).T

    qb = _stack_groups(q_ref[0], masks).astype(BF16)
    qpos = _qpos(p_lo, tq, rows)
    own = p_lo // MOBA_BLOCK
    sel = _moba_select_t(qb, km_ref[0:16, :], own)

    kt, vt = _tile_from_ref(kv_ref, (0,), own, tk)
    carry = _flash_step(qb, kt, vt, _kpos(own * tk, tk, rows) <= qpos, HD ** -0.5, _flash_init(rows))

    def body(n):
        def step(j, carry):
            kt, vt = _tile_from_ref(kv_ref, (0,), j, n * tk)
            picks = [jnp.broadcast_to(_col(sel, n * j + k), (rows, tk)) for k in range(n)]
            mask = (picks[0] if n == 1 else jnp.concatenate(picks, axis=1)) > 0.5
            return _flash_step(qb, kt, vt, mask, HD ** -0.5, carry)
        return step

    carry = lax.fori_loop(0, own // 2, body(2), carry)
    _, l, acc = lax.fori_loop(own - own % 2, own, body(1), carry)
    o_ref[0] = _unstack_heads(acc / l, tq, masks)


def _moba_sample_kernel(pt_ref, q_ref, *refs, n_pages, page, past):
    page_refs, new_ref, o_ref = refs[:n_pages], refs[n_pages], refs[n_pages + 1]
    tq = q_ref.shape[1]
    rows = HEADS * tq
    per_blk = MOBA_BLOCK // page
    own = past // MOBA_BLOCK
    masks = _lane_group_masks(GROUP_W, HD, HEADS)
    km = jnp.zeros((GROUP_W, LANES), F32)
    for p in range(n_pages):
        km = km + _block_mean_col(page_refs[p][0, 0, 0:GROUP_W, :], p // per_blk)
    km = km * (1.0 / MOBA_BLOCK)
    qb = _stack_groups(q_ref[0], masks).astype(BF16)
    qpos = _qpos(past, tq, rows)
    sel = _moba_select(qb, km, own, own)
    kt_ref, vt_ref = refs[n_pages + 2:]
    _gather_context(page_refs, new_ref, kt_ref, vt_ref, page)
    s = _dot(qb, kt_ref[...]) * (HD ** -0.5)
    ctx = kt_ref.shape[1]
    picked = [jnp.broadcast_to(sel[:, m:m + 1], (rows, MOBA_BLOCK)) for m in range(own)]
    picked.append(jnp.where(_kpos(past, ctx - past, rows) <= qpos, 1.0, 0.0))
    _, l, acc = _softmax_context(s, jnp.concatenate(picked, axis=1) > 0.5, vt_ref[...])
    o_ref[0] = _unstack_heads(acc / l, tq, masks)


def _gelu_tanh(x):
    return 0.5 * x * (1.0 + jnp.tanh(math.sqrt(2.0 / math.pi) * (x + 0.044715 * (x * x * x))))


def _phase_rows(tile_t):
    r = lax.broadcasted_iota(jnp.int32, (LANES, LANES), 0)
    c = lax.broadcasted_iota(jnp.int32, (LANES, LANES), 1)
    perm = jnp.where(c == NSA_STRIDE * (r % 8) + r // 8, 1.0, 0.0).astype(BF16)
    hi = tile_t.astype(BF16)
    lo = (tile_t - hi.astype(F32)).astype(BF16)
    return _dot_t(perm, hi) + _dot_t(perm, lo)


def _nsa_compress(rows_refs, w1_ref, b1_ref, pe_ref, w2k_ref, w2v_ref, ck_refs, cv_refs, n_cmp):
    row = lax.broadcasted_iota(jnp.int32, (n_cmp, 1), 0)
    tokens = []
    for rows_ref in rows_refs:
        first, second = [], []
        for l in range(NSA_STRIDE):
            x = jnp.concatenate([rows_ref[t * LANES + l * 8:t * LANES + l * 8 + 8, :] for t in range(n_cmp // 8)],
                                axis=0)
            xs = jnp.where(row == n_cmp - 1, 0.0, pltpu.roll(x, n_cmp - 1, 0))
            first.append((x + pe_ref[l:l + 1, :]).astype(BF16))
            second.append((xs + pe_ref[l + NSA_STRIDE:l + NSA_STRIDE + 1, :]).astype(BF16))
        tokens.append(jnp.concatenate(first + second, axis=1))
    hid = _dot(jnp.concatenate(tokens, axis=0), w1_ref[...].reshape(NSA_CMP_LEN * LANES, 2 * NSA_CMP_HID))
    hid = _gelu_tanh(hid + b1_ref[...]).astype(BF16)
    ck, cv = _dot(hid, w2k_ref[...]), _dot(hid, w2v_ref[...])
    for k, (ck_ref, cv_ref) in enumerate(zip(ck_refs, cv_refs)):
        ck_ref[...] = jnp.zeros_like(ck_ref)
        cv_ref[...] = jnp.zeros_like(cv_ref)
        ck_ref[0:n_cmp, :] = ck[k * n_cmp:(k + 1) * n_cmp]
        cv_ref[0:n_cmp, :] = cv[k * n_cmp:(k + 1) * n_cmp]


def _nsa_select_t(p_sum, p_first, n_sel):
    tq, ncp = p_sum.shape
    n_rows = -(-n_sel // 8) * 8
    gi = lax.broadcasted_iota(jnp.int32, (n_rows, ncp), 0)
    gj = lax.broadcasted_iota(jnp.int32, (n_rows, ncp), 1) // (NSA_SEL_BLOCK // NSA_STRIDE)
    group_t = jnp.where(gi == gj, 1.0, 0.0).astype(BF16)
    hi = p_sum.astype(BF16)
    lo = (p_sum - hi.astype(F32)).astype(BF16)
    imp_t = _dot_t(group_t, hi) + _dot_t(group_t, lo)
    row = lax.broadcasted_iota(jnp.int32, (n_rows, tq), 0)
    qblk = (p_first + lax.broadcasted_iota(jnp.int32, (n_rows, tq), 1)) // NSA_SEL_BLOCK
    valid = (row <= qblk) & (row < n_sel)
    forced = (row == 0) | (row == qblk) | (row == qblk - 1)
    score_t = jnp.where(valid, jnp.where(forced, NSA_FORCE, imp_t), -jnp.inf)
    pick_t = _topk_mask_t(score_t, min(NSA_TOPN, n_sel))
    return _rows_from_t(jnp.where(valid, pick_t, 0.0))


def _nsa_cmp_and_select(qb, qpos, ck_ref, cv_ref, tq, n_sel):
    rows = qb.shape[0]
    ncp = ck_ref.shape[0]
    s = _dot_t(qb, ck_ref[...].astype(BF16)) * (HD ** -0.5)
    c_end = lax.broadcasted_iota(jnp.int32, (rows, ncp), 1) * NSA_STRIDE + (NSA_CMP_LEN - 1)
    c_mask = c_end <= qpos
    s = jnp.where(c_mask, s, NEG_BIG)
    p = jnp.where(c_mask, jnp.exp(s - jnp.max(s, axis=-1, keepdims=True)), 0.0)
    den = jnp.sum(p, axis=-1, keepdims=True)
    p = p / jnp.where(den > 0, den, 1.0)
    o_cmp = _dot(p.astype(BF16), cv_ref[...].astype(BF16))
    p_sum = p[0:tq]
    for h in range(1, HEADS):
        p_sum = p_sum + p[h * tq:(h + 1) * tq]
    if tq % LANES == 0:
        return o_cmp, _nsa_select_t(p_sum, qpos[0:1, :], n_sel)
    gi = lax.broadcasted_iota(jnp.int32, (ncp, LANES), 0)
    gj = lax.broadcasted_iota(jnp.int32, (ncp, LANES), 1)
    group_mat = jnp.where((gi // (NSA_SEL_BLOCK // NSA_STRIDE)) == gj, 1.0, 0.0).astype(BF16)
    imp = _split_dot(p_sum, group_mat)
    qblk = qpos[0:tq] // NSA_SEL_BLOCK
    lane = lax.broadcasted_iota(jnp.int32, (tq, LANES), 1)
    valid = (lane <= qblk) & (lane < n_sel)
    forced = (lane == 0) | (lane == qblk) | (lane == qblk - 1)
    score = jnp.where(valid, jnp.where(forced, NSA_FORCE, imp), -jnp.inf)
    rank = jnp.zeros((tq, LANES), F32)
    for m in range(n_sel):
        sm = score[:, m:m + 1]
        beats = (sm > score) | ((sm == score) & (m < lane))
        rank = rank + jnp.where(beats, 1.0, 0.0)
    sel = jnp.where((rank < min(NSA_TOPN, n_sel)) & valid, 1.0, 0.0)
    return o_cmp, sel


def _shared_tile(k64, v64):
    return (jnp.concatenate([k64] * HEADS, axis=0).astype(BF16),
            jnp.concatenate([v64] * HEADS, axis=0).astype(BF16))


def _slc_mask(sel_bf, blk0, kpos0, tk, qpos, rows, causal):
    ei = lax.broadcasted_iota(jnp.int32, (LANES, tk), 0)
    ej = lax.broadcasted_iota(jnp.int32, (LANES, tk), 1) // NSA_SEL_BLOCK
    expand = jnp.where(ei == ej + blk0, 1.0, 0.0).astype(BF16)
    picked = _dot(sel_bf, expand)
    picked = jnp.concatenate([picked] * HEADS, axis=0) > 0.5
    return picked & (_kpos(kpos0, tk, rows) <= qpos) if causal else picked


def _win_mask(kpos0, tk, qpos, rows):
    kp = _kpos(kpos0, tk, rows)
    return (kp <= qpos) & (kp > qpos - NSA_WINDOW)


def _nsa_combine(g, o_cmp, slc, win, tq, masks):
    o_slc = slc[2] / slc[1]
    o_win = win[2] / win[1]
    out = jnp.zeros((tq, GROUP_W), F32)
    for h in range(HEADS):
        r0, r1 = h * tq, (h + 1) * tq
        mix = (g[:, 3 * h:3 * h + 1] * o_cmp[r0:r1] + g[:, 3 * h + 1:3 * h + 2] * o_slc[r0:r1]
               + g[:, 3 * h + 2:3 * h + 3] * o_win[r0:r1])
        out = out + mix * masks[h]
    return out


def _nsa_kernel(q_ref, nkv_ref, g_ref, wkv_ref, w1_ref, b1_ref, pe_ref, w2k_ref, w2v_ref,
                o_ref, rows_ref, ck_ref, cv_ref, *, tq, tk, pos0, n_cmp, n_sel):
    i = pl.program_id(1)
    p_lo = pos0 + i * tq
    rows = HEADS * tq
    masks = _lane_group_masks(GROUP_W, HD, HEADS)

    @pl.when(i == 0)
    def _():
        for c in range(nkv_ref.shape[2] // LANES):
            rows_ref[c * LANES:(c + 1) * LANES, :] = _phase_rows(nkv_ref[0, 0:LANES, c * LANES:(c + 1) * LANES])
        _nsa_compress([rows_ref], w1_ref, b1_ref, pe_ref, w2k_ref, w2v_ref, [ck_ref], [cv_ref], n_cmp)

    qb = _stack_groups(q_ref[0], masks).astype(BF16)
    qpos = _qpos(p_lo, tq, rows)
    scale = HD ** -0.5
    o_cmp, sel = _nsa_cmp_and_select(qb, qpos, ck_ref, cv_ref, tq, n_sel)
    sel_bf = sel.astype(BF16)
    j_own = (p_lo + tq - 1) // tk

    def slc_step(j, carry, causal):
        cols = pl.ds(pl.multiple_of(j * tk, tk), tk)
        kt, vt = _shared_tile(nkv_ref[0, 2 * HD:3 * HD, cols], nkv_ref[0, 3 * HD:4 * HD, cols])
        mask = _slc_mask(sel_bf, j * (tk // NSA_SEL_BLOCK), j * tk, tk, qpos, rows, causal)
        return _flash_step(qb, kt, vt, mask, scale, carry)

    slc = slc_step(j_own, _flash_init(rows), True)
    slc = lax.fori_loop(0, j_own, lambda j, c: slc_step(j, c, False), slc)

    def win_step(j, carry, mask_fn):
        cols = pl.ds(pl.multiple_of(j * tk, tk), tk)
        kt, vt = _shared_tile(wkv_ref[0, 0:HD, cols], wkv_ref[0, HD:2 * HD, cols])
        return _flash_step(qb, kt, vt, mask_fn(_kpos(j * tk, tk, rows)), scale, carry)

    win = win_step(j_own, _flash_init(rows), lambda kp: kp <= qpos)
    j_lo = jnp.maximum(p_lo - (NSA_WINDOW - 1), 0) // tk
    win = lax.fori_loop(j_lo, j_own, lambda j, c: win_step(j, c, lambda kp: kp > qpos - NSA_WINDOW), win)
    o_ref[0] = _nsa_combine(g_ref[0], o_cmp, slc, win, tq, masks)


def _nsa_sample_kernel(pt_ref, samples, *, n_pages, page, past, n_cmp, n_sel, w_eff):
    scale = HD ** -0.5
    masks = _lane_group_masks(GROUP_W, HD, HEADS)
    ctxs = []
    for refs in samples:
        q_ref, page_refs, new_ref = refs[0], refs[1:1 + n_pages], refs[1 + n_pages]
        (g_ref, wst_ref, wnew_ref, w1_ref, b1_ref, pe_ref, w2k_ref, w2v_ref,
         o_ref, wout_ref, rows_ref, ck_ref, cv_ref, sk_ref, sv_ref, wk_ref, wv_ref) = refs[2 + n_pages:]
        ctxs.append(dict(q_ref=q_ref, page_refs=page_refs, new_ref=new_ref, g_ref=g_ref, wst_ref=wst_ref,
                         wnew_ref=wnew_ref, cmp=(w1_ref, b1_ref, pe_ref, w2k_ref, w2v_ref), o_ref=o_ref,
                         wout_ref=wout_ref,
                         rows_ref=rows_ref, ck_ref=ck_ref, cv_ref=cv_ref, sk_ref=sk_ref, sv_ref=sv_ref,
                         wk_ref=wk_ref, wv_ref=wv_ref))
    tq = ctxs[0]["q_ref"].shape[1]
    rows = HEADS * tq
    qpos = _qpos(past, tq, rows)
    assert tq <= NSA_STRIDE

    for c in ctxs:
        c["rows_ref"][...] = jnp.zeros_like(c["rows_ref"])
        for p in range(n_pages):
            c["rows_ref"][p * page:(p + 1) * page, :] = _phase_rows(c["page_refs"][p][0, 0, 0:LANES, :])
        for t in range(tq):
            c["rows_ref"][n_pages * page + 8 * t:n_pages * page + 8 * t + 1, :] = c["new_ref"][0, t:t + 1, 0:LANES]
    for c in ctxs:
        for p in range(n_pages):
            c["sk_ref"][:, p * page:(p + 1) * page] = c["page_refs"][p][0, 0, 2 * HD:3 * HD, :].astype(BF16)
            c["sv_ref"][:, p * page:(p + 1) * page] = c["page_refs"][p][0, 0, 3 * HD:4 * HD, :].astype(BF16)
        new_t = _rows_to_tile(c["new_ref"][0, :, 2 * HD:4 * HD], page)
        c["sk_ref"][:, n_pages * page:] = new_t[0:HD]
        c["sv_ref"][:, n_pages * page:] = new_t[HD:2 * HD]
        c["wk_ref"][:, 0:w_eff] = c["wst_ref"][0, 0, 0:HD, :].astype(BF16)
        c["wv_ref"][:, 0:w_eff] = c["wst_ref"][0, 0, HD:2 * HD, :].astype(BF16)
        wnew = jnp.concatenate([c["wnew_ref"][0], jnp.zeros((page - tq, 2 * HD), F32)], axis=0).T
        c["wk_ref"][:, w_eff:] = wnew[0:HD].astype(BF16)
        c["wv_ref"][:, w_eff:] = wnew[HD:2 * HD].astype(BF16)
        c["wout_ref"][0] = jnp.concatenate([c["wst_ref"][0, 0], wnew], axis=1)[:, tq:tq + w_eff]
        q = c["q_ref"][0]
        c["qb"] = _stack_groups(q, masks).astype(BF16)
        c["qs"] = jnp.concatenate([q[:, h * HD:(h + 1) * HD] for h in range(HEADS)], axis=0).astype(BF16)
    for c in ctxs:
        wmask = _win_mask(past - w_eff, c["wk_ref"].shape[1], qpos, rows)
        c["win"] = _softmax_context(_dot(c["qs"], c["wk_ref"][...]) * scale, wmask, c["wv_ref"][...])
    _nsa_compress([c["rows_ref"] for c in ctxs], *ctxs[0]["cmp"], [c["ck_ref"] for c in ctxs],
                  [c["cv_ref"] for c in ctxs], n_cmp)
    for c in ctxs:
        c["o_cmp"], c["sel"] = _nsa_cmp_and_select(c["qb"], qpos, c["ck_ref"], c["cv_ref"], tq, n_sel)
    ctx = ctxs[0]["sk_ref"].shape[1]
    ei = lax.broadcasted_iota(jnp.int32, (LANES, ctx), 0)
    ej = lax.broadcasted_iota(jnp.int32, (LANES, ctx), 1) // NSA_SEL_BLOCK
    expand = jnp.where(ei == ej, 1.0, 0.0).astype(BF16)
    causal = _kpos(0, ctx, rows) <= qpos
    widen = lambda t: (t[0], t[1], jnp.concatenate([t[2]] * HEADS, axis=1))
    for c in ctxs:
        picked = _dot(c["sel"].astype(BF16), expand)
        mask = (jnp.concatenate([picked] * HEADS, axis=0) > 0.5) & causal
        slc = _softmax_context(_dot(c["qs"], c["sk_ref"][...]) * scale, mask, c["sv_ref"][...])
        c["o_ref"][0] = _nsa_combine(c["g_ref"][0], c["o_cmp"], widen(slc), widen(c["win"]), tq, masks)


def _nsa_cmp_weights(pe, w1, b1, w2):
    z = jnp.zeros((NSA_CMP_LEN, HD, NSA_CMP_HID), F32)
    w1cat = jnp.concatenate([jnp.concatenate([w1[0], z], axis=2), jnp.concatenate([z, w1[1]], axis=2)], axis=1)
    b1cat = jnp.concatenate([b1[0], b1[1]])[None, :]
    pecat = jnp.concatenate([pe[0], pe[1]], axis=1)
    z2 = jnp.zeros((NSA_CMP_HID, GROUP_W), F32)
    w2k4 = jnp.concatenate([jnp.tile(w2[0], (1, HEADS)), z2], axis=0)
    w2v4 = jnp.concatenate([z2, jnp.tile(w2[1], (1, HEADS))], axis=0)
    return w1cat.astype(BF16), b1cat, pecat, w2k4.astype(BF16), w2v4.astype(BF16)


def _cmp_specs(index_map):
    full = lambda *shape: pl.BlockSpec(shape, lambda *a: (0,) * len(shape))
    return [full(NSA_CMP_LEN, LANES, 2 * NSA_CMP_HID), full(1, 2 * NSA_CMP_HID),
            full(NSA_CMP_LEN, LANES), full(2 * NSA_CMP_HID, GROUP_W), full(2 * NSA_CMP_HID, GROUP_W)]


def _layer_norm(x, g, b):
    xc = x - jnp.mean(x, axis=-1, keepdims=True)
    var = jnp.mean(xc * xc, axis=-1, keepdims=True)
    return xc * lax.rsqrt(var + LN_EPS) * g + b


def _outproj_kernel(x_ref, a_ref, b_ref, c_ref, d_ref, w_ref, g_ref, be_ref, o_ref):
    mix = _dot(a_ref[...].astype(BF16), w_ref[0:256, :])
    mix = mix + _dot(b_ref[...].astype(BF16), w_ref[256:512, :])
    mix = mix + _dot(c_ref[...].astype(BF16), w_ref[512:768, :])
    mix = mix + _dot(d_ref[...].astype(BF16), w_ref[768:1024, :])
    o_ref[...] = _layer_norm(ALPHA * x_ref[...] + mix, g_ref[...], be_ref[...])


def _outproj_norm(x2d, outs, w_out_bf, gain, bias, tm):
    n = x2d.shape[0]
    row = lambda w: pl.BlockSpec((tm, w), lambda r: (r, 0))
    return pl.pallas_call(
        _outproj_kernel,
        grid=(n // tm,),
        in_specs=[row(D_MODEL), row(GROUP_W), row(GROUP_W), row(GROUP_W), row(GROUP_W),
                  pl.BlockSpec((D_MODEL, D_MODEL), lambda r: (0, 0)),
                  pl.BlockSpec((1, D_MODEL), lambda r: (0, 0)), pl.BlockSpec((1, D_MODEL), lambda r: (0, 0))],
        out_specs=row(D_MODEL),
        out_shape=jax.ShapeDtypeStruct((n, D_MODEL), F32),
        compiler_params=_params(1),
        name="outproj_ln",
    )(x2d, *outs, w_out_bf, gain, bias)


def _moe_kernel(h_ref, wr_ref, rb_ref, wgu_ref, wd_ref, g_ref, be_ref, o_ref, hb_ref, comb_ref, acc_ref):
    e = pl.program_id(1)

    @pl.when(e == 0)
    def _():
        hb = h_ref[...].astype(BF16)
        hb_ref[...] = hb
        logit = _dot(hb, wr_ref[...]) + rb_ref[...]
        lane = lax.broadcasted_iota(jnp.int32, logit.shape, 1)
        is_g = lane < N_GROUPS
        lg = jnp.where(is_g, logit, NEG_BIG)
        gmax = jnp.max(lg, axis=-1, keepdims=True)
        gsel = jnp.min(jnp.where(is_g & (lg == gmax), lane, LANES), axis=-1, keepdims=True)
        pg = 1.0 / jnp.sum(jnp.where(is_g, jnp.exp(lg - gmax), 0.0), axis=-1, keepdims=True)
        e0 = N_GROUPS + gsel * EXPERTS_PER_GROUP
        in_g = (lane >= e0) & (lane < e0 + EXPERTS_PER_GROUP)
        le = jnp.where(in_g, logit, NEG_BIG)
        v1 = jnp.max(le, axis=-1, keepdims=True)
        i1 = jnp.min(jnp.where(in_g & (le == v1), lane, LANES), axis=-1, keepdims=True)
        rest = in_g & (lane != i1)
        le2 = jnp.where(rest, logit, NEG_BIG)
        v2 = jnp.max(le2, axis=-1, keepdims=True)
        i2 = jnp.min(jnp.where(rest & (le2 == v2), lane, LANES), axis=-1, keepdims=True)
        ex = jnp.exp(v2 - v1)
        w1 = pg / (1.0 + ex)
        w2 = pg * ex / (1.0 + ex)
        comb_ref[...] = jnp.where(lane == i1, w1, 0.0) + jnp.where(lane == i2, w2, 0.0)
        acc_ref[...] = jnp.zeros_like(acc_ref)

    hids = []
    for k in range(EXPERTS_PER_STEP):
        gu = _dot(hb_ref[...], wgu_ref[k])
        gt, up = gu[:, :EXPERT_HID], gu[:, EXPERT_HID:]
        hid = gt / (1.0 + jnp.exp(-gt)) * up
        hids.append((hid * _col(comb_ref[...], N_GROUPS + e * EXPERTS_PER_STEP + k)).astype(BF16))
    acc_ref[...] += _dot(jnp.concatenate(hids, axis=1), wd_ref[...].reshape(EXPERTS_PER_STEP * EXPERT_HID, D_MODEL))

    @pl.when(e == N_EXPERTS // EXPERTS_PER_STEP - 1)
    def _():
        o_ref[...] = _layer_norm(ALPHA * h_ref[...] + acc_ref[...], g_ref[...], be_ref[...])


def _moe_norm(h2d, wr, rb, wgu, wd, gain, bias, tm):
    n = h2d.shape[0]
    assert n % tm == 0
    return pl.pallas_call(
        _moe_kernel,
        grid=(n // tm, N_EXPERTS // EXPERTS_PER_STEP),
        in_specs=[pl.BlockSpec((tm, D_MODEL), lambda r, e: (r, 0)),
                  pl.BlockSpec((D_MODEL, LANES), lambda r, e: (0, 0)),
                  pl.BlockSpec((1, LANES), lambda r, e: (0, 0)),
                  pl.BlockSpec((EXPERTS_PER_STEP, D_MODEL, 2 * EXPERT_HID), lambda r, e: (e, 0, 0)),
                  pl.BlockSpec((EXPERTS_PER_STEP, EXPERT_HID, D_MODEL), lambda r, e: (e, 0, 0)),
                  pl.BlockSpec((1, D_MODEL), lambda r, e: (0, 0)),
                  pl.BlockSpec((1, D_MODEL), lambda r, e: (0, 0))],
        out_specs=pl.BlockSpec((tm, D_MODEL), lambda r, e: (r, 0)),
        out_shape=jax.ShapeDtypeStruct((n, D_MODEL), F32),
        scratch_shapes=[pltpu.VMEM((tm, D_MODEL), BF16), pltpu.VMEM((tm, LANES), F32),
                        pltpu.VMEM((tm, D_MODEL), F32)],
        compiler_params=_params(2),
        name="moe_ln",
    )(h2d, wr, rb, wgu, wd, gain, bias)


def _layer_weights(l, w_in, w_out, pe, w1, b1, w2, lam_p, subln, ln_g, ln_b, rgw, rgb, rew, reb, wg, wu, wd):
    wr = jnp.concatenate([rgw[l], rew[l], jnp.zeros((D_MODEL, LANES - N_GROUPS - N_EXPERTS), F32)], axis=1)
    rb = jnp.concatenate([rgb[l], reb[l], jnp.zeros((LANES - N_GROUPS - N_EXPERTS,), F32)])[None, :]
    wq, wkv = _proj_weights(w_in[l])
    return dict(
        wq=wq, wkv=wkv, wkv_t=wkv.T, wall=jnp.concatenate([wq, wkv], axis=1),
        w_out=w_out[l].astype(BF16),
        cmp=_nsa_cmp_weights(pe[l], w1[l], b1[l], w2[l]),
        lam_p=lam_p[l],
        subln4=jnp.tile(subln[l], HEADS)[None, :],
        lam_init=0.8 - 0.6 * math.exp(-0.3 * l),
        g1=ln_g[l, 0][None, :], b1=ln_b[l, 0][None, :], g2=ln_g[l, 1][None, :], b2=ln_b[l, 1][None, :],
        wr=wr.astype(BF16), rb=rb,
        wgu=jnp.concatenate([wg[l], wu[l]], axis=2).astype(BF16),
        wd=wd[l].astype(BF16),
    )


def _ffn(x, outs, lw, tm, tm_ffn):
    b, t, _ = x.shape
    x2d = x.reshape(b * t, D_MODEL)
    h = _outproj_norm(x2d, [o.reshape(b * t, GROUP_W) for o in outs], lw["w_out"], lw["g1"], lw["b1"], tm)
    y = _moe_norm(h, lw["wr"], lw["rb"], lw["wgu"], lw["wd"], lw["g2"], lw["b2"], tm_ffn)
    return y.reshape(b, t, D_MODEL)


def kernel(x_prompt, x_sample, cache_sb_kv, cache_nsa_kv, cache_diff_kv, cache_moba_kv, state_nsa_win, page_table, w_in, w_out, nsa_cmp_pe, nsa_cmp_w1, nsa_cmp_b1, nsa_cmp_w2, diff_lambda, diff_subln, ln_gain, ln_bias, router_group_w, router_group_b, router_expert_w, router_expert_b, expert_w_gate, expert_w_up, expert_w_down):
    bp, tp, _ = x_prompt.shape
    bs, ts, _ = x_sample.shape
    depth = w_in.shape[0]
    n_phys, page = cache_sb_kv.shape[1], cache_sb_kv.shape[2]
    n_pages = page_table.shape[1]
    past = n_pages * page
    w_eff = state_nsa_win.shape[2]
    l_s = past + ts
    l_s_pad = -(-l_s // MOBA_BLOCK) * MOBA_BLOCK
    tm_p = 256
    tm_s = min(256, bs * ts)
    assert tp % KV_TILE == 0 and tm_p % ts == 0 and (bs * ts) % tm_s == 0 and ts <= page
    assert page == LANES and past % MOBA_BLOCK == 0 and w_eff % LANES == 0 and w_eff == min(NSA_WINDOW, past)

    pool_sb = jnp.transpose(cache_sb_kv, (0, 1, 3, 4, 5, 2)).reshape(depth, n_phys, 2 * GROUP_W, page)
    pool_diff = jnp.transpose(cache_diff_kv, (0, 1, 3, 4, 5, 2)).reshape(depth, n_phys, 2 * GROUP_W, page)
    pool_moba = jnp.transpose(cache_moba_kv, (0, 1, 3, 4, 5, 2)).reshape(depth, n_phys, 2 * GROUP_W, page)
    pool_nsa = jnp.transpose(cache_nsa_kv, (0, 1, 3, 4, 2)).reshape(depth, n_phys, 4 * HD, page)
    win_state = jnp.transpose(state_nsa_win, (0, 1, 3, 4, 2)).reshape(depth, bs, 2 * HD, w_eff)

    tabs_p = _rope_tables(jnp.arange(tp, dtype=jnp.int32))
    tabs_s = _rope_tables(past + (jnp.arange(tm_s, dtype=jnp.int32) % ts))
    tabs_p_t, tabs_s_t = jnp.transpose(tabs_p, (0, 2, 1)), jnp.transpose(tabs_s, (0, 2, 1))
    n_cmp_p, n_sel_p = tp // NSA_STRIDE, tp // NSA_SEL_BLOCK
    n_cmp_s, n_sel_s = l_s_pad // NSA_STRIDE, l_s_pad // NSA_SEL_BLOCK
    ncp = lambda n: -(-n // LANES) * LANES

    y_p, y_s = x_prompt, x_sample
    kv_bufs, win_buf, new_s = None, None, []
    for l in range(depth):
        lw = _layer_weights(l, w_in, w_out, nsa_cmp_pe, nsa_cmp_w1, nsa_cmp_b1, nsa_cmp_w2, diff_lambda,
                            diff_subln, ln_gain, ln_bias, router_group_w, router_group_b, router_expert_w,
                            router_expert_b, expert_w_gate, expert_w_up, expert_w_down)
        lam_specs = [pl.BlockSpec((4, DIFF_QK), lambda *a: (0, 0)), pl.BlockSpec((1, GROUP_W), lambda *a: (0, 0))]

        pp = _project(y_p.reshape(bp * tp, D_MODEL), lw["wq"], lw["wkv_t"], tabs_p, tabs_p_t, tm_p,
                      Q_SEGS, KV_SEGS, tp, depth, l, kv_bufs)
        kv_bufs = [pp[s_[0]] for s_ in KV_SEGS]
        q3 = lambda a: a.reshape(bp, tp, a.shape[-1])
        o_sb = _prompt_call(functools.partial(_sb_kernel, tq=256, tk=KV_TILE, pos0=0, chunk=HEADS * 256),
                            q3(pp["sbq"]), pp["sbkv"], l, (), (), 256, "sb_attn",
                            scratch=[pltpu.VMEM((HEADS * 256, GROUP_W), BF16),
                                     pltpu.VMEM((HEADS * 256, GROUP_W), F32),
                                     pltpu.VMEM((HEADS * 256, 1), F32)])
        o_nsa = _prompt_call(
            functools.partial(_nsa_kernel, tq=256, tk=KV_TILE, pos0=0, n_cmp=n_cmp_p, n_sel=n_sel_p),
            q3(pp["nq"]), pp["nkv"], l, (q3(pp["gate"]), pp["wkv"]) + lw["cmp"],
            [pl.BlockSpec((1, 256, LANES), lambda bb, i: (bb, i, 0)),
             pl.BlockSpec((None, 1, 2 * HD, tp), lambda bb, i: (l, bb, 0, 0))] + _cmp_specs(None), 256, "nsa_attn",
            scratch=[pltpu.VMEM((tp, LANES), F32), pltpu.VMEM((ncp(n_cmp_p), GROUP_W), F32),
                     pltpu.VMEM((ncp(n_cmp_p), GROUP_W), F32)])
        o_diff = _prompt_call(
            functools.partial(_diff_kernel, tq=128, tk=KV_TILE, pos0=0, lam_init=lw["lam_init"]),
            q3(pp["dq"]), pp["dkv"], l, (lw["lam_p"], lw["subln4"]), lam_specs, 128, "diff_attn")
        o_moba = _prompt_call(
            functools.partial(_moba_kernel, tq=256, tk=KV_TILE, pos0=0, n_blocks=tp // MOBA_BLOCK),
            q3(pp["mq"]), pp["mkv"], l, (), (), 256, "moba_attn", scratch=[pltpu.VMEM((LANES, GROUP_W), F32)])
        y_p = _ffn(y_p, (o_sb, o_nsa, o_diff, o_moba), lw, tm_p, math.gcd(1024, bp * tp))

        ps = _project(y_s.reshape(bs * ts, D_MODEL), lw["wall"], lw["wkv_t"], tabs_s, tabs_s_t, tm_s,
                      Q_SEGS + KV_SEGS, (), ts)
        s3 = lambda a: a.reshape(bs, ts, a.shape[-1])
        new_tile = lambda name: s3(ps[name])
        common = dict(n_pages=n_pages, page=page, past=past)
        ctx_scratch = lambda ch: [((ch, past + page), BF16), ((ch, past + page), BF16)]
        o_sb = _sample_call(functools.partial(_sb_sample_kernel, **common), page_table, s3(ps["sbq"]),
                            pool_sb, l, new_tile("sbkv"), (), (), (), "sb_attn_s", scratch=ctx_scratch(GROUP_W))
        o_nsa, win_buf = _sample_call(
            functools.partial(_nsa_sample_kernel, n_cmp=n_cmp_s, n_sel=n_sel_s, w_eff=w_eff, **common),
            page_table, s3(ps["nq"]), pool_nsa, l, new_tile("nkv"),
            (s3(ps["gate"]), win_state, new_tile("wkv")) + lw["cmp"],
            [pl.BlockSpec((NSA_SAMPLES_PER_STEP, ts, LANES), lambda bb, pt: (bb, 0, 0)),
             pl.BlockSpec((1, NSA_SAMPLES_PER_STEP, 2 * HD, w_eff), lambda bb, pt: (l, bb, 0, 0)),
             pl.BlockSpec((NSA_SAMPLES_PER_STEP, ts, 2 * HD), lambda bb, pt: (bb, 0, 0))] + _cmp_specs(None),
            (0, 1, 0, None, None, None, None, None), "nsa_attn_s",
            scratch=[((l_s_pad, LANES), F32), ((ncp(n_cmp_s), GROUP_W), F32), ((ncp(n_cmp_s), GROUP_W), F32)]
            + ctx_scratch(HD) + [((HD, w_eff + page), BF16), ((HD, w_eff + page), BF16)],
            spb=NSA_SAMPLES_PER_STEP, staged=True, slab_out=(depth, (2 * HD, w_eff), win_buf))
        o_diff = _sample_call(functools.partial(_diff_sample_kernel, lam_init=lw["lam_init"], **common),
                              page_table, s3(ps["dq"]), pool_diff, l, new_tile("dkv"),
                              (lw["lam_p"], lw["subln4"]), lam_specs, (None, None), "diff_attn_s",
                              scratch=ctx_scratch(GROUP_W))
        o_moba = _sample_call(functools.partial(_moba_sample_kernel, **common), page_table, s3(ps["mq"]),
                              pool_moba, l, new_tile("mkv"), (), (), (), "moba_attn_s",
                              scratch=ctx_scratch(GROUP_W))
        y_s = _ffn(y_s, (o_sb, o_nsa, o_diff, o_moba), lw, tm_s, math.gcd(1024, bs * ts))
        new_s.append((ps["sbkv"], ps["nkv"], ps["dkv"], ps["mkv"]))

    def heads_out(a, b, t):
        return jnp.transpose(a.reshape(depth, b, 2, HEADS, HD, t), (0, 1, 5, 2, 3, 4))

    def slots_out(a, b, t, n):
        return jnp.transpose(a.reshape(depth, b, n, HD, t), (0, 1, 4, 2, 3))

    st_p = [pp["sbkv"], pp["nkv"], pp["dkv"], pp["mkv"], pp["wkv"][:, :, :, tp - min(NSA_WINDOW, tp):]]
    st_s = [jnp.stack([r[k] for r in new_s]) for k in range(4)] + [win_buf]
    outs_p = (heads_out(st_p[0], bp, tp), slots_out(st_p[1], bp, tp, 4), heads_out(st_p[2], bp, tp),
              heads_out(st_p[3], bp, tp), slots_out(st_p[4], bp, min(NSA_WINDOW, tp), 2))
    outs_s = (st_s[0].reshape(depth, bs, ts, 2, HEADS, HD), st_s[1].reshape(depth, bs, ts, 4, HD),
              st_s[2].reshape(depth, bs, ts, 2, HEADS, HD), st_s[3].reshape(depth, bs, ts, 2, HEADS, HD),
              slots_out(st_s[4], bs, w_eff, 2))
    return (y_p, y_s, *outs_p, *outs_s)
```
